```python
import math
import jax, jax.numpy as jnp
from jax import lax
import numpy as np

D_MODEL = 1024
BATCH = 4
SEQ = 8192
DEPTH = 1
DEC_BATCH = 32
DEC_SEQ = 1
PAST_LEN = 16384
PAGE_SIZE = 128

D_MIX = 2 * D_MODEL
D_ATTN = D_MIX // 2
D_SSD = D_MIX - D_ATTN
HEAD_DIM = 64
N_ATTN_HEADS = D_ATTN // HEAD_DIM
PATTERNS = ((128, 1), (512, 4), (2048, 16))
WINDOW_MAX = 2048
BLK = 128
N_BUCKETS = 32
MAX_DISTANCE = WINDOW_MAX
SSD_HEAD_DIM = 64
N_SSD_HEADS = D_SSD // SSD_HEAD_DIM
D_STATE = 128
N_GROUPS = 2
HEADS_PER_GROUP = N_SSD_HEADS // N_GROUPS
CONV_W = 4
CONV_DIM = D_SSD + 2 * N_GROUPS * D_STATE
CHUNK = 128
DT_MIN = 0.001
DT_MAX = 0.1
D_IN_PROJ = 4 * D_ATTN + D_SSD + CONV_DIM + N_SSD_HEADS
EPS = 1e-6

kernel_name = "hybrid_dilated_swa_ssd_step"


def _rmsnorm(x, w):
    xf = x.astype(jnp.float32)
    y = xf * lax.rsqrt(jnp.mean(xf * xf, axis=-1, keepdims=True) + EPS)
    return (y * w.astype(jnp.float32)).astype(x.dtype)


def _t5_bucket(dist):
    max_exact = N_BUCKETS // 2
    d_f = jnp.maximum(dist, 1).astype(jnp.float32)
    large = max_exact + (jnp.log(d_f / max_exact) / math.log(MAX_DISTANCE / max_exact)
                         * (N_BUCKETS - max_exact)).astype(jnp.int32)
    large = jnp.minimum(large, N_BUCKETS - 1)
    return jnp.where(dist < max_exact, dist, large)


def _mixer_inputs(x, norm_w, w_in, q_norm_w, k_norm_w):
    h = _rmsnorm(x, norm_w)
    proj = h @ w_in
    cuts = [D_ATTN, 2 * D_ATTN, 3 * D_ATTN, 4 * D_ATTN, 4 * D_ATTN + D_SSD,
            4 * D_ATTN + D_SSD + CONV_DIM]
    q, k, v, g, z, xbc, dt_raw = jnp.split(proj, cuts, axis=-1)
    heads = x.shape[:-1] + (N_ATTN_HEADS, HEAD_DIM)
    q = _rmsnorm(q.reshape(heads), q_norm_w)
    k = _rmsnorm(k.reshape(heads), k_norm_w)
    return q, k, v.reshape(heads), g, z, xbc, dt_raw


def _mixer_output(x, o_attn, g, y_ssd, z, ssd_norm_w, w_out):
    lead = x.shape[:-1]
    a = o_attn.reshape(lead + (D_ATTN,)).astype(jnp.float32) * jax.nn.silu(g.astype(jnp.float32))
    s = _rmsnorm(y_ssd.reshape(lead + (D_SSD,)).astype(jnp.float32)
                 * jax.nn.silu(z.astype(jnp.float32)), ssd_norm_w)
    mixed = jnp.concatenate([a, s], axis=-1).astype(x.dtype)
    return x + mixed @ w_out


def _merge(outs, lses):
    wts = jax.nn.softmax(jnp.stack(lses, axis=0), axis=0)
    o = jnp.sum(wts[..., None] * jnp.stack(outs, axis=0).astype(jnp.float32), axis=0)
    return o.astype(outs[0].dtype)


def _to_residue(a, d):
    b, s = a.shape[:2]
    L = s // d
    a = a.reshape((b, L, d) + a.shape[2:])
    a = jnp.moveaxis(a, 2, 1).reshape((b * d, L) + a.shape[3:])
    lp = -(-L // BLK) * BLK
    return jnp.pad(a, [(0, 0), (0, lp - L)] + [(0, 0)] * (a.ndim - 2))


def _from_residue(a, b, d, L):
    a = a[:, :L].reshape((b, d, L) + a.shape[2:])
    a = jnp.moveaxis(a, 1, 2)
    return a.reshape((b, L * d) + a.shape[3:])


def _band_attn(q, k, v, win_sub, dil, rel_bias):
    n, lp, h, dh = q.shape
    nb = lp // BLK
    qb = q.reshape(n, nb, BLK, h, dh)

    def ext(a):
        a = a.reshape(n, nb, BLK, h, dh)
        prev = jnp.concatenate([jnp.zeros_like(a[:, :1]), a[:, :-1]], axis=1)
        return jnp.concatenate([prev, a], axis=2)

    kx, vx = ext(k), ext(v)
    rel = jnp.arange(BLK)[:, None] + BLK - jnp.arange(2 * BLK)[None, :]
    band = (rel >= 0) & (rel <= win_sub)
    bias = jnp.moveaxis(rel_bias[_t5_bucket(jnp.maximum(rel, 0) * dil)], -1, 0).astype(jnp.float32)
    not_first = jnp.arange(2 * BLK) >= BLK
    scale = HEAD_DIM ** -0.5

    def one_block(args):
        qi, ki, vi, bi = args
        s = jnp.einsum('nqhd,nkhd->nhqk', qi, ki).astype(jnp.float32) * scale + bias
        valid = band & ((bi > 0) | not_first)[None, :]
        s = jnp.where(valid, s, -jnp.inf)
        lse = jax.nn.logsumexp(s, axis=-1)
        p = jnp.exp(s - lse[..., None]).astype(vi.dtype)
        o = jnp.einsum('nhqk,nkhd->nqhd', p, vi)
        return o, jnp.moveaxis(lse, 1, 2)

    o, lse = lax.map(one_block, (jnp.moveaxis(qb, 1, 0), jnp.moveaxis(kx, 1, 0),
                                 jnp.moveaxis(vx, 1, 0), jnp.arange(nb)))
    o = jnp.moveaxis(o, 0, 1).reshape(n, lp, h, dh)
    lse = jnp.moveaxis(lse, 0, 1).reshape(n, lp, h)
    return o, lse


def _dilated_attn_prompt(q, k, v, rel_bias):
    b, s = q.shape[:2]
    outs, lses = [], []
    for w, d in PATTERNS:
        L = s // d
        o, lse = _band_attn(_to_residue(q, d), _to_residue(k, d), _to_residue(v, d),
                            w // d, d, rel_bias)
        outs.append(_from_residue(o, b, d, L))
        lses.append(_from_residue(lse, b, d, L))
    return _merge(outs, lses)


def _dilated_attn_sample(q, k_all, v_all, n_past, rel_bias):
    t = q.shape[1]
    j = jnp.arange(t)
    scale = HEAD_DIM ** -0.5
    outs, lses = [], []
    for w, d in PATTERNS:
        kk = jnp.arange(w // d + 1)
        idx = n_past + j[:, None] - kk[None, :] * d
        valid = idx >= 0
        idx = jnp.maximum(idx, 0)
        kg, vg = k_all[:, idx], v_all[:, idx]
        bias = rel_bias[_t5_bucket(kk * d)].T.astype(jnp.float32)
        s = jnp.einsum('bthd,btkhd->bhtk', q, kg).astype(jnp.float32) * scale + bias[None, :, None, :]
        s = jnp.where(valid, s, -jnp.inf)
        lse = jax.nn.logsumexp(s, axis=-1)
        p = jnp.exp(s - lse[..., None]).astype(vg.dtype)
        outs.append(jnp.einsum('bhtk,btkhd->bthd', p, vg))
        lses.append(jnp.moveaxis(lse, 1, 2))
    return _merge(outs, lses)


def _conv_silu(xpad, conv_w, conv_b, t):
    y = conv_b + sum(xpad[:, i:i + t] * conv_w[i] for i in range(CONV_W))
    return jax.nn.silu(y)


def _ssd_split(xc, dt_raw, dt_bias, a_log):
    lead = xc.shape[:-1]
    gn = N_GROUPS * D_STATE
    xs = xc[..., :D_SSD].reshape(lead + (N_GROUPS, HEADS_PER_GROUP, SSD_HEAD_DIM))
    bm = xc[..., D_SSD:D_SSD + gn].reshape(lead + (N_GROUPS, D_STATE)).astype(jnp.float32)
    cm = xc[..., D_SSD + gn:].reshape(lead + (N_GROUPS, D_STATE)).astype(jnp.float32)
    dt = jax.nn.softplus(dt_raw.astype(jnp.float32) + dt_bias.astype(jnp.float32))
    dt = dt.reshape(lead + (N_GROUPS, HEADS_PER_GROUP))
    a = -jnp.exp(a_log.astype(jnp.float32)).reshape(N_GROUPS, HEADS_PER_GROUP)
    return xs, bm, cm, dt, a


def _ssd_chunked(xs, dt, a, bm, cm):
    b, s = xs.shape[:2]
    nc = s // CHUNK
    xdt = (xs.astype(jnp.float32) * dt[..., None]).reshape(b, nc, CHUNK, N_GROUPS, HEADS_PER_GROUP, SSD_HEAD_DIM)
    la = (dt * a).reshape(b, nc, CHUNK, N_GROUPS, HEADS_PER_GROUP)
    bc = bm.reshape(b, nc, CHUNK, N_GROUPS, D_STATE)
    cc = cm.reshape(b, nc, CHUNK, N_GROUPS, D_STATE)
    a_cs = jnp.cumsum(la, axis=2)
    causal = jnp.tril(jnp.ones((CHUNK, CHUNK), bool))[:, :, None, None]
    seg = a_cs[:, :, :, None] - a_cs[:, :, None, :]
    lmat = jnp.exp(jnp.where(causal, seg, -jnp.inf))
    cb = jnp.einsum('bclgn,bcsgn->bclsg', cc, bc)
    y_diag = jnp.einsum('bclsg,bclsgr,bcsgrp->bclgrp', cb, lmat, xdt)
    decay_to_end = jnp.exp(a_cs[:, :, -1:] - a_cs)
    states = jnp.einsum('bclgn,bclgr,bclgrp->bcgrpn', bc, decay_to_end, xdt)
    chunk_decay = jnp.exp(a_cs[:, :, -1])

    def step(h, inp):
        st, dec = inp
        return h * dec[..., None, None] + st, h

    h0 = jnp.zeros((b, N_GROUPS, HEADS_PER_GROUP, SSD_HEAD_DIM, D_STATE), jnp.float32)
    h_fin, h_prev = lax.scan(step, h0, (jnp.moveaxis(states, 1, 0), jnp.moveaxis(chunk_decay, 1, 0)))
    h_prev = jnp.moveaxis(h_prev, 0, 1)
    y_off = jnp.einsum('bclgn,bcgrpn,bclgr->bclgrp', cc, h_prev, jnp.exp(a_cs))
    y = (y_diag + y_off).reshape(b, s, N_GROUPS, HEADS_PER_GROUP, SSD_HEAD_DIM)
    return y, h_fin


def _ssd_recurrent(xs, dt, a, bm, cm, h0):
    def step(h, inp):
        xt, dtt, bt, ct = inp
        h = h * jnp.exp(dtt * a)[..., None, None] + jnp.einsum(
            'bgrp,bgn->bgrpn', xt.astype(jnp.float32) * dtt[..., None], bt)
        return h, jnp.einsum('bgrpn,bgn->bgrp', h, ct)

    h, ys = lax.scan(step, h0, (jnp.moveaxis(xs, 1, 0), jnp.moveaxis(dt, 1, 0),
                                jnp.moveaxis(bm, 1, 0), jnp.moveaxis(cm, 1, 0)))
    return jnp.moveaxis(ys, 0, 1), h


def _prompt_layer(x, norm_w, w_in, q_norm_w, k_norm_w, rel_bias, conv_w, conv_b,
                  dt_bias, a_log, d_skip, ssd_norm_w, w_out):
    b, s = x.shape[:2]
    q, k, v, g, z, xbc, dt_raw = _mixer_inputs(x, norm_w, w_in, q_norm_w, k_norm_w)
    o = _dilated_attn_prompt(q, k, v, rel_bias)
    xc = _conv_silu(jnp.pad(xbc, ((0, 0), (CONV_W - 1, 0), (0, 0))), conv_w, conv_b, s)
    xs, bm, cm, dt, a = _ssd_split(xc, dt_raw, dt_bias, a_log)
    y, h = _ssd_chunked(xs, dt, a, bm, cm)
    y = y + d_skip.astype(jnp.float32).reshape(N_GROUPS, HEADS_PER_GROUP)[..., None] * xs.astype(jnp.float32)
    out = _mixer_output(x, o, g, y, z, ssd_norm_w, w_out)
    nw = min(WINDOW_MAX, s)
    new = (k[:, s - nw:], v[:, s - nw:], xbc[:, s - (CONV_W - 1):],
           h.reshape(b, N_SSD_HEADS, SSD_HEAD_DIM, D_STATE).astype(x.dtype))
    return out, new


def _sample_layer(x, win_k, win_v, conv_state, ssm_state, norm_w, w_in, q_norm_w, k_norm_w,
                  rel_bias, conv_w, conv_b, dt_bias, a_log, d_skip, ssd_norm_w, w_out):
    b, t = x.shape[:2]
    q, k, v, g, z, xbc, dt_raw = _mixer_inputs(x, norm_w, w_in, q_norm_w, k_norm_w)
    k_all = jnp.concatenate([win_k.astype(k.dtype), k], axis=1)
    v_all = jnp.concatenate([win_v.astype(v.dtype), v], axis=1)
    o = _dilated_attn_sample(q, k_all, v_all, win_k.shape[1], rel_bias)
    xcat = jnp.concatenate([conv_state.astype(xbc.dtype), xbc], axis=1)
    xc = _conv_silu(xcat, conv_w, conv_b, t)
    xs, bm, cm, dt, a = _ssd_split(xc, dt_raw, dt_bias, a_log)
    h0 = ssm_state.reshape(b, N_GROUPS, HEADS_PER_GROUP, SSD_HEAD_DIM, D_STATE).astype(jnp.float32)
    y, h = _ssd_recurrent(xs, dt, a, bm, cm, h0)
    y = y + d_skip.astype(jnp.float32).reshape(N_GROUPS, HEADS_PER_GROUP)[..., None] * xs.astype(jnp.float32)
    out = _mixer_output(x, o, g, y, z, ssd_norm_w, w_out)
    new = (k, v, xcat[:, -(CONV_W - 1):],
           h.reshape(b, N_SSD_HEADS, SSD_HEAD_DIM, D_STATE).astype(x.dtype))
    return out, new


def setup_inputs(seed: int = 0) -> dict:
    key = jax.random.key(seed)
    ks = jax.random.split(key, 18)
    f32 = jnp.float32
    win_buf = min(WINDOW_MAX, PAST_LEN)

    def gain(k, n):
        return 1.0 + 0.02 * jax.random.normal(k, (DEPTH, n), f32)

    u = jax.random.uniform(ks[10], (DEPTH, N_SSD_HEADS), f32)
    dt0 = jnp.exp(u * (math.log(DT_MAX) - math.log(DT_MIN)) + math.log(DT_MIN))
    dt_bias = dt0 + jnp.log(-jnp.expm1(-dt0))
    return {
        "x_prompt": jax.random.normal(ks[0], (BATCH, SEQ, D_MODEL), f32),
        "x_sample": jax.random.normal(ks[1], (DEC_BATCH, DEC_SEQ, D_MODEL), f32),
        "cache_win_k": jax.random.normal(ks[2], (DEPTH, DEC_BATCH, win_buf, N_ATTN_HEADS, HEAD_DIM), f32),
        "cache_win_v": jax.random.normal(ks[3], (DEPTH, DEC_BATCH, win_buf, N_ATTN_HEADS, HEAD_DIM), f32),
        "state_conv": jax.random.normal(ks[4], (DEPTH, DEC_BATCH, CONV_W - 1, CONV_DIM), f32),
        "state_ssm": 0.5 * jax.random.normal(ks[5], (DEPTH, DEC_BATCH, N_SSD_HEADS, SSD_HEAD_DIM, D_STATE), f32),
        "norm_w": gain(ks[6], D_MODEL),
        "w_in": jax.random.normal(ks[7], (DEPTH, D_MODEL, D_IN_PROJ), f32) * D_MODEL ** -0.5,
        "q_norm_w": gain(ks[8], HEAD_DIM),
        "k_norm_w": gain(ks[9], HEAD_DIM),
        "rel_bias": 0.5 * jax.random.normal(ks[11], (N_BUCKETS, N_ATTN_HEADS), f32),
        "conv_w": jax.random.normal(ks[12], (DEPTH, CONV_W, CONV_DIM), f32) * CONV_W ** -0.5,
        "conv_b": 0.01 * jax.random.normal(ks[13], (DEPTH, CONV_DIM), f32),
        "dt_bias": dt_bias,
        "a_log": jnp.log(jax.random.uniform(ks[14], (DEPTH, N_SSD_HEADS), f32, 1.0, 16.0)),
        "d_skip": gain(ks[15], N_SSD_HEADS),
        "ssd_norm_w": gain(ks[16], D_SSD),
        "w_out": jax.random.normal(ks[17], (DEPTH, D_MIX, D_MODEL), f32) * D_MIX ** -0.5,
    }


def reference(x_prompt, x_sample, cache_win_k, cache_win_v, state_conv, state_ssm,
              norm_w, w_in, q_norm_w, k_norm_w, rel_bias, conv_w, conv_b, dt_bias,
              a_log, d_skip, ssd_norm_w, w_out):
    y_p, y_s = x_prompt, x_sample
    new_p, new_s = [], []
    for l in range(DEPTH):
        lw = (norm_w[l], w_in[l], q_norm_w[l], k_norm_w[l], rel_bias, conv_w[l], conv_b[l],
              dt_bias[l], a_log[l], d_skip[l], ssd_norm_w[l], w_out[l])
        y_p, st_p = _prompt_layer(y_p, *lw)
        y_s, st_s = _sample_layer(y_s, cache_win_k[l], cache_win_v[l], state_conv[l], state_ssm[l], *lw)
        new_p.append(st_p)
        new_s.append(st_s)
    kp, vp, cp, hp = (jnp.stack(a) for a in zip(*new_p))
    k_s, v_s, c_s, h_s = (jnp.stack(a) for a in zip(*new_s))
    return (y_p, y_s, kp, vp, cp, hp, k_s, v_s, c_s, h_s)
```

```python
import functools
import math

import jax
import jax.numpy as jnp
from jax import lax
from jax.experimental import pallas as pl
from jax.experimental.pallas import tpu as pltpu

F32 = jnp.float32
BF16 = jnp.bfloat16

D_MODEL = 1024
D_ATTN = 1024
D_SSD = 1024
HEAD_DIM = 64
N_HEADS = 16
PATTERNS = ((128, 1), (512, 4), (2048, 16))
WINDOW_MAX = 2048
BLK = 128
N_BUCKETS = 32
D_STATE = 128
N_GROUPS = 2
CONV_W = 4
CONV_DIM = D_SSD + 2 * N_GROUPS * D_STATE
CHUNK = 128
EPS = 1e-6
D_IN_PROJ = 4 * D_ATTN + D_SSD + CONV_DIM + N_HEADS
LANES = 128
D_IN_PAD = D_IN_PROJ - N_HEADS + LANES
SPAN = BLK * 16
NEG = -1e30
VMEM_LIMIT = 56 * 1024 * 1024


def _sigmoid(x):
    return 1.0 / (1.0 + jnp.exp(-x))


def _softplus(x):
    return jnp.maximum(x, 0.0) + jnp.log1p(jnp.exp(-jnp.abs(x)))


def _t5_bucket(dist):
    max_exact = N_BUCKETS // 2
    d_f = jnp.maximum(dist, 1).astype(F32)
    large = max_exact + (jnp.log(d_f / max_exact) / math.log(WINDOW_MAX / max_exact)
                         * (N_BUCKETS - max_exact)).astype(jnp.int32)
    large = jnp.minimum(large, N_BUCKETS - 1)
    return jnp.where(dist < max_exact, dist, large)


def _inproj_kernel(x_ref, nw_ref, w_ref, qnw_ref, knw_ref, gsum_ref,
                   q_ref, k_ref, v_ref, g_ref, z_ref, xbc_ref, dt_ref):
    x = x_ref[...]
    ms = jnp.mean(x * x, axis=-1, keepdims=True)
    h = (x * lax.rsqrt(ms + EPS) * nw_ref[...]).astype(BF16)

    def seg(c0, width):
        return jnp.dot(h, w_ref[:, c0:c0 + width], preferred_element_type=F32)

    for out_ref, base, hw_ref in ((q_ref, 0, qnw_ref), (k_ref, D_ATTN, knw_ref)):
        for c in range(D_ATTN // 256):
            p = seg(base + 256 * c, 256)
            p2 = p * p
            p2_hi = p2.astype(BF16)
            p2_lo = (p2 - p2_hi.astype(F32)).astype(BF16)
            ss = (jnp.dot(p2_hi, gsum_ref[...], preferred_element_type=F32)
                  + jnp.dot(p2_lo, gsum_ref[...], preferred_element_type=F32))
            out_ref[:, 256 * c:256 * (c + 1)] = p * lax.rsqrt(ss * (1.0 / HEAD_DIM) + EPS) * hw_ref[...]
    for c in range(2):
        v_ref[:, 512 * c:512 * (c + 1)] = seg(2 * D_ATTN + 512 * c, 512)
        g_ref[:, 512 * c:512 * (c + 1)] = seg(3 * D_ATTN + 512 * c, 512).astype(g_ref.dtype)
        z_ref[:, 512 * c:512 * (c + 1)] = seg(4 * D_ATTN + 512 * c, 512).astype(z_ref.dtype)
    for c in range(3):
        xbc_ref[:, 512 * c:512 * (c + 1)] = seg(5 * D_ATTN + 512 * c, 512)
    dt_ref[...] = seg(5 * D_ATTN + CONV_DIM, LANES)


def _inproj(x2d, nw, w_pad, qnw, knw, gsum, tm):
    t = x2d.shape[0]
    row = lambda i: (i, 0)
    const = lambda i: (0, 0)
    outs = (
        jax.ShapeDtypeStruct((t, D_ATTN), F32),
        jax.ShapeDtypeStruct((t, D_ATTN), F32),
        jax.ShapeDtypeStruct((t, D_ATTN), F32),
        jax.ShapeDtypeStruct((t, D_ATTN), BF16),
        jax.ShapeDtypeStruct((t, D_SSD), BF16),
        jax.ShapeDtypeStruct((t, CONV_DIM), F32),
        jax.ShapeDtypeStruct((t, LANES), F32),
    )
    return pl.pallas_call(
        _inproj_kernel,
        grid=(t // tm,),
        in_specs=[
            pl.BlockSpec((tm, D_MODEL), row),
            pl.BlockSpec((1, D_MODEL), const),
            pl.BlockSpec((D_MODEL, D_IN_PAD), const, pipeline_mode=pl.Buffered(1)),
            pl.BlockSpec((1, 256), const),
            pl.BlockSpec((1, 256), const),
            pl.BlockSpec((256, 256), const),
        ],
        out_specs=[
            pl.BlockSpec((tm, D_ATTN), row),
            pl.BlockSpec((tm, D_ATTN), row),
            pl.BlockSpec((tm, D_ATTN), row),
            pl.BlockSpec((tm, D_ATTN), row),
            pl.BlockSpec((tm, D_SSD), row),
            pl.BlockSpec((tm, CONV_DIM), row),
            pl.BlockSpec((tm, LANES), row),
        ],
        out_shape=outs,
        compiler_params=pltpu.CompilerParams(
            dimension_semantics=("arbitrary",), vmem_limit_bytes=VMEM_LIMIT),
    )(x2d, nw, w_pad, qnw, knw, gsum)


def _outproj_kernel(x_ref, a_ref, s_ref, w_ref, y_ref):
    y_ref[...] = (x_ref[...]
                  + jnp.dot(a_ref[...], w_ref[0:D_ATTN, :], preferred_element_type=F32)
                  + jnp.dot(s_ref[...], w_ref[D_ATTN:, :], preferred_element_type=F32))


def _outproj(x2d, a, s, w_out_b, tm):
    t = x2d.shape[0]
    row = lambda i: (i, 0)
    return pl.pallas_call(
        _outproj_kernel,
        grid=(t // tm,),
        in_specs=[
            pl.BlockSpec((tm, D_MODEL), row),
            pl.BlockSpec((tm, D_ATTN), row),
            pl.BlockSpec((tm, D_SSD), row),
            pl.BlockSpec((D_ATTN + D_SSD, D_MODEL), lambda i: (0, 0), pipeline_mode=pl.Buffered(1)),
        ],
        out_specs=pl.BlockSpec((tm, D_MODEL), row),
        out_shape=jax.ShapeDtypeStruct((t, D_MODEL), F32),
        compiler_params=pltpu.CompilerParams(
            dimension_semantics=("arbitrary",), vmem_limit_bytes=VMEM_LIMIT),
    )(x2d, a, s, w_out_b)


def _attn_kernel(q_ref, kp_ref, kc_ref, vp_ref, vc_ref, g_ref, bias_ref, o_ref, m_s, l_s, acc_s):
    first = (pl.program_id(2) == 0).astype(jnp.int32)
    lane = lax.broadcasted_iota(jnp.int32, (BLK, LANES), 1)
    is_a = lane < HEAD_DIM
    scale = HEAD_DIM ** -0.5

    def block(qs, ka, kb, va, vb, bias):
        qsc = qs * scale
        qq = jnp.concatenate([jnp.where(is_a, qsc, 0.0), jnp.where(is_a, 0.0, qsc)], axis=0).astype(BF16)
        kk = jnp.concatenate([ka, kb], axis=0).astype(BF16)
        s = lax.dot_general(qq, kk, (((1,), (1,)), ((), ())), preferred_element_type=F32) + bias
        m = jnp.max(s, axis=1, keepdims=True)
        p = jnp.exp(s - m)
        l = jnp.sum(p, axis=1, keepdims=True)
        vv = jnp.concatenate([va, vb], axis=0).astype(BF16)
        o = jnp.dot(p.astype(BF16), vv, preferred_element_type=F32)
        o_c = jnp.where(is_a, o[:BLK], o[BLK:])
        m_c = jnp.where(is_a, m[:BLK], m[BLK:])
        l_c = jnp.where(is_a, l[:BLK], l[BLK:])
        return o_c, m_c, l_c

    def rows(start, d):
        if d == 1:
            return pl.ds(start, BLK)
        return pl.ds(start, BLK, stride=d)

    def merge(idx, o_c, m_c, l_c):
        m_old = m_s[idx, :]
        m_new = jnp.maximum(m_old, m_c)
        e_old = jnp.exp(m_old - m_new)
        e_new = jnp.exp(m_c - m_new)
        l_s[idx, :] = l_s[idx, :] * e_old + l_c * e_new
        acc_s[idx, :] = acc_s[idx, :] * e_old + o_c * e_new
        m_s[idx, :] = m_new

    def run(pi, d, q_start, prev_ref, prev_start, bias):
        qi = rows(q_start, d)
        o_c, m_c, l_c = block(q_ref[qi, :], prev_ref[rows(prev_start, d), :], kc_ref[qi, :],
                              (vp_ref if prev_ref is kp_ref else vc_ref)[rows(prev_start, d), :],
                              vc_ref[qi, :], bias)
        if pi == 0:
            m_s[qi, :] = m_c
            l_s[qi, :] = l_c
            acc_s[qi, :] = o_c
        else:
            merge(qi, o_c, m_c, l_c)

    for pi, (_, d) in enumerate(PATTERNS):
        sub = BLK * d
        n_sub = SPAN // sub

        def body_first(r, carry, pi=pi, d=d, sub=sub):
            run(pi, d, r, kp_ref, SPAN - sub + r, bias_ref[first, pi])
            return carry
        lax.fori_loop(0, d, body_first, 0)

        if n_sub > 1:
            def body_rest(i, carry, pi=pi, d=d, sub=sub):
                u = i // d + 1
                r = i % d
                start = u * sub + r
                if d == 1:
                    start = pl.multiple_of(start, BLK)
                run(pi, d, start, kc_ref, start - sub, bias_ref[0, pi])
                return carry
            lax.fori_loop(0, (n_sub - 1) * d, body_rest, 0)

    g = g_ref[...].astype(F32)
    o_ref[...] = (acc_s[...] / l_s[...] * (g * _sigmoid(g))).astype(o_ref.dtype)


def _attn_prompt(q, k, v, g, bias_tbl, batch, seq):
    n_span = seq // SPAN
    n_hp = N_HEADS // 2
    cur = lambda hp, b, s: (b * n_span + s, hp)
    prev = lambda hp, b, s: (b * n_span + jnp.maximum(s - 1, 0), hp)
    blk = (SPAN, LANES)
    return pl.pallas_call(
        _attn_kernel,
        grid=(n_hp, batch, n_span),
        in_specs=[
            pl.BlockSpec(blk, cur),
            pl.BlockSpec(blk, prev),
            pl.BlockSpec(blk, cur),
            pl.BlockSpec(blk, prev),
            pl.BlockSpec(blk, cur),
            pl.BlockSpec(blk, cur),
            pl.BlockSpec((None, 2, len(PATTERNS), 2 * BLK, 2 * BLK), lambda hp, b, s: (hp, 0, 0, 0, 0)),
        ],
        out_specs=pl.BlockSpec(blk, cur),
        out_shape=jax.ShapeDtypeStruct((batch * seq, D_ATTN), BF16),
        scratch_shapes=[pltpu.VMEM(blk, F32), pltpu.VMEM(blk, F32), pltpu.VMEM(blk, F32)],
        compiler_params=pltpu.CompilerParams(
            dimension_semantics=("arbitrary", "arbitrary", "arbitrary"), vmem_limit_bytes=VMEM_LIMIT),
    )(q, k, k, v, v, g, bias_tbl)


def _prompt_bias_table(rel_bias):
    i = jnp.arange(BLK)[:, None]
    j = jnp.arange(2 * BLK)[None, :]
    rel = i + BLK - j
    tbls = []
    for w, d in PATTERNS:
        band = (rel >= 0) & (rel <= w // d)
        bias = rel_bias[_t5_bucket(jnp.maximum(rel, 0) * d)].astype(F32)
        bias = jnp.moveaxis(bias, -1, 0)
        normal = jnp.where(band[None], bias, NEG)
        first = jnp.where((band & (j >= BLK))[None], bias, NEG)
        tbls.append(jnp.stack([normal, first], axis=0))
    t = jnp.stack(tbls, axis=1)
    t = t.reshape(2, len(PATTERNS), N_HEADS // 2, 2 * BLK, 2 * BLK)
    return jnp.moveaxis(t, 2, 0)


def _ssd_kernel(xbc_ref, z_ref, dt_ref, cw_ref, cb_ref, dtb_ref, alog_ref, dsk_ref, nw_ref,
                e_ref, tril_ref, s_ref, h_ref, cbuf):
    c = pl.program_id(1)

    @pl.when(c == 0)
    def _():
        cbuf[0:8, :] = jnp.zeros((8, CONV_DIM), F32)
        h_ref[...] = jnp.zeros_like(h_ref)

    cbuf[8:8 + CHUNK, :] = xbc_ref[...]
    acc = cb_ref[...] + cbuf[5:5 + CHUNK, :] * cw_ref[0:1, :]
    for i in range(1, CONV_W):
        acc = acc + cbuf[5 + i:5 + i + CHUNK, :] * cw_ref[i:i + 1, :]
    xc = acc * _sigmoid(acc)
    cbuf[0:8, :] = cbuf[CHUNK:CHUNK + 8, :]

    xs = xc[:, :D_SSD]
    lane = lax.broadcasted_iota(jnp.int32, (CHUNK, LANES), 1)
    sub = lax.broadcasted_iota(jnp.int32, (CHUNK, LANES), 0)
    head_lane = lane < N_HEADS
    dt = jnp.where(head_lane, _softplus(dt_ref[...] + dtb_ref[...]), 0.0)
    la = dt * (-jnp.exp(alog_ref[...]))
    la_hi = la.astype(BF16)
    la_lo = (la - la_hi.astype(F32)).astype(BF16)
    tril = tril_ref[...]
    a_cs = (jnp.dot(tril, la_hi, preferred_element_type=F32)
            + jnp.dot(tril, la_lo, preferred_element_type=F32))
    ea = jnp.where(head_lane, jnp.exp(a_cs), 0.0)
    dte = jnp.where(head_lane, jnp.exp(a_cs[CHUNK - 1:CHUNK, :] - a_cs), 0.0)

    def expand(val):
        hi = val.astype(BF16)
        lo = (val - hi.astype(F32)).astype(BF16)
        return (jnp.dot(hi, e_ref[...], preferred_element_type=F32)
                + jnp.dot(lo, e_ref[...], preferred_element_type=F32))

    dtx = expand(dt)
    eax = expand(ea)
    dtex = expand(dte)
    xdt_f = xs * dtx
    xdt = xdt_f.astype(BF16)
    xdte = (xdt_f * dtex).astype(BF16)

    a_cs_t = a_cs.T
    causal = sub >= lane
    is_a = lane < HEAD_DIM
    ys = []
    for g in range(N_GROUPS):
        b_g = xc[:, D_SSD + g * D_STATE:D_SSD + (g + 1) * D_STATE]
        c_g = xc[:, D_SSD + (N_GROUPS + g) * D_STATE:D_SSD + (N_GROUPS + g + 1) * D_STATE]
        b_bf = b_g.astype(BF16)
        c_bf = c_g.astype(BF16)
        cb = lax.dot_general(c_bf, b_bf, (((1,), (1,)), ((), ())), preferred_element_type=F32)
        gcols = slice(g * 512, (g + 1) * 512)
        h_prev = h_ref[:, gcols]
        y_off = jnp.dot(c_bf, h_prev.astype(BF16), preferred_element_type=F32) * eax[:, gcols]
        st = jnp.dot(b_g.T.astype(BF16), xdte[:, gcols], preferred_element_type=F32)
        h_ref[:, gcols] = h_prev * eax[CHUNK - 1:CHUNK, gcols] + st
        for hp in range(4):
            pair = []
            for which in range(2):
                h = g * 8 + hp * 2 + which
                col = jnp.sum(jnp.where(lane == h, a_cs, 0.0), axis=1, keepdims=True)
                seg = col - a_cs_t[h:h + 1, :]
                lmat = jnp.exp(jnp.where(causal, seg, NEG))
                pair.append((cb * lmat).astype(BF16))
            x_pair = xdt[:, g * 512 + hp * LANES:g * 512 + (hp + 1) * LANES]
            y_a = jnp.dot(pair[0], x_pair, preferred_element_type=F32)
            y_b = jnp.dot(pair[1], x_pair, preferred_element_type=F32)
            ys.append(jnp.where(is_a, y_a, y_b) + y_off[:, hp * LANES:(hp + 1) * LANES])
    y = jnp.concatenate(ys, axis=1) + dsk_ref[...] * xs
    zf = z_ref[...].astype(F32)
    yz = y * (zf * _sigmoid(zf))
    var = jnp.mean(yz * yz, axis=-1, keepdims=True)
    s_ref[...] = (yz * lax.rsqrt(var + EPS) * nw_ref[...]).astype(s_ref.dtype)


def _ssd_prompt(xbc, z, dt, cw, cb, dtb, alog, dsk, nw, emat, tril, batch, seq):
    nc = seq // CHUNK
    row = lambda b, c: (b * nc + c, 0)
    const = lambda b, c: (0, 0)
    return pl.pallas_call(
        _ssd_kernel,
        grid=(batch, nc),
        in_specs=[
            pl.BlockSpec((CHUNK, CONV_DIM), row),
            pl.BlockSpec((CHUNK, D_SSD), row),
            pl.BlockSpec((CHUNK, LANES), row),
            pl.BlockSpec((CONV_W, CONV_DIM), const),
            pl.BlockSpec((1, CONV_DIM), const),
            pl.BlockSpec((1, LANES), const),
            pl.BlockSpec((1, LANES), const),
            pl.BlockSpec((1, D_SSD), const),
            pl.BlockSpec((1, D_SSD), const),
            pl.BlockSpec((LANES, D_SSD), const),
            pl.BlockSpec((CHUNK, CHUNK), const),
        ],
        out_specs=[
            pl.BlockSpec((CHUNK, D_SSD), row),
            pl.BlockSpec((None, D_STATE, D_SSD), lambda b, c: (b, 0, 0)),
        ],
        out_shape=(jax.ShapeDtypeStruct((batch * seq, D_SSD), BF16),
                   jax.ShapeDtypeStruct((batch, D_STATE, D_SSD), F32)),
        scratch_shapes=[pltpu.VMEM((CHUNK + 8, CONV_DIM), F32)],
        compiler_params=pltpu.CompilerParams(
            dimension_semantics=("arbitrary", "arbitrary"), vmem_limit_bytes=VMEM_LIMIT),
    )(xbc, z, dt, cw, cb, dtb, alog, dsk, nw, emat, tril)


def _attn_sample_kernel(q_ref, kn_ref, vn_ref, g_ref, k1_ref, k4_ref, k16_ref, v1_ref, v4_ref, v16_ref,
                        bias_ref, bias0_ref, o_ref):
    q = q_ref[...] * (HEAD_DIM ** -0.5)
    shape3 = (BLK, N_HEADS, HEAD_DIM)
    s0 = jnp.sum(q * kn_ref[...], axis=-1, keepdims=True) + bias0_ref[...]
    scores = []
    m = s0
    for pi, k_ref in enumerate((k1_ref, k4_ref, k16_ref)):
        s = jnp.sum(q[None] * k_ref[...], axis=-1, keepdims=True)
        s = jnp.broadcast_to(s, shape3) + bias_ref[pi]
        scores.append(s)
        m = jnp.maximum(m, jnp.max(s, axis=0))
    p0 = jnp.exp(s0 - m)
    n_pat = float(len(PATTERNS))
    l = n_pat * p0
    acc = n_pat * p0 * vn_ref[...]
    for s, v_ref in zip(scores, (v1_ref, v4_ref, v16_ref)):
        p = jnp.exp(s - m[None])
        l = l + jnp.sum(p, axis=0)
        acc = acc + jnp.sum(p * v_ref[...], axis=0)
    g = g_ref[...]
    o_ref[...] = acc / l * (g * _sigmoid(g))


def _attn_sample(q3, kn3, vn3, g3, cache_k, cache_v, bias_s, bias0):
    b, nw = cache_k.shape[0], cache_k.shape[1]
    tok = pl.BlockSpec((None, N_HEADS, HEAD_DIM), lambda i: (i, 0, 0))
    views, specs = [], []
    for cache in (cache_k, cache_v):
        for _, d in PATTERNS:
            n_blk = nw // (BLK * d)
            if d == 1:
                views.append(cache)
                specs.append(pl.BlockSpec((None, BLK, N_HEADS, HEAD_DIM),
                                          lambda i, n_blk=n_blk: (i, n_blk - 1, 0, 0)))
            else:
                views.append(cache.reshape(b, nw // d, d, N_HEADS, HEAD_DIM))
                specs.append(pl.BlockSpec((None, BLK, None, N_HEADS, HEAD_DIM),
                                          lambda i, n_blk=n_blk: (i, n_blk - 1, 0, 0, 0)))
    return pl.pallas_call(
        _attn_sample_kernel,
        grid=(b,),
        in_specs=[tok, tok, tok, tok] + specs + [
            pl.BlockSpec((len(PATTERNS), BLK, N_HEADS, HEAD_DIM), lambda i: (0, 0, 0, 0)),
            pl.BlockSpec((N_HEADS, HEAD_DIM), lambda i: (0, 0)),
        ],
        out_specs=tok,
        out_shape=jax.ShapeDtypeStruct((b, N_HEADS, HEAD_DIM), F32),
        compiler_params=pltpu.CompilerParams(
            dimension_semantics=("arbitrary",), vmem_limit_bytes=VMEM_LIMIT),
    )(q3, kn3, vn3, g3, *views, bias_s, bias0)


def _sample_bias_tables(rel_bias):
    kk = BLK - jnp.arange(BLK)
    tbls = []
    for _, d in PATTERNS:
        bias = rel_bias[_t5_bucket(kk * d)].astype(F32)
        tbls.append(jnp.broadcast_to(bias[:, :, None], (BLK, N_HEADS, HEAD_DIM)))
    bias0 = rel_bias[_t5_bucket(jnp.zeros((), jnp.int32))].astype(F32)
    return jnp.stack(tbls, axis=0), jnp.broadcast_to(bias0[:, None], (N_HEADS, HEAD_DIM))


def _ssd_sample_kernel(xbc_ref, z_ref, dt_ref, sc_ref, h_ref, cw_ref, cb_ref, dtb_ref, alog_ref, dsk_ref,
                       nw_ref, e_ref, s_ref, conv_out_ref, h_out_ref):
    xnew = xbc_ref[...]
    sc = sc_ref[...]
    acc = cb_ref[...] + xnew * cw_ref[CONV_W - 1:CONV_W, :]
    for i in range(CONV_W - 1):
        acc = acc + sc[i:i + 1, :] * cw_ref[i:i + 1, :]
    xc = acc * _sigmoid(acc)
    conv_out_ref[0:CONV_W - 2, :] = sc[1:CONV_W - 1, :]
    conv_out_ref[CONV_W - 2:CONV_W - 1, :] = xnew

    xs = xc[:, :D_SSD]
    lane1 = lax.broadcasted_iota(jnp.int32, (1, LANES), 1)
    dt = jnp.where(lane1 < N_HEADS, _softplus(dt_ref[...] + dtb_ref[...]), 0.0)
    da = jnp.where(lane1 < N_HEADS, jnp.exp(dt * (-jnp.exp(alog_ref[...]))), 0.0)

    def expand(val):
        v8 = jnp.broadcast_to(val, (8, LANES))
        out = jnp.zeros((8, D_SSD), F32)
        for _ in range(3):
            part = v8.astype(BF16)
            out = out + jnp.dot(part, e_ref[...], preferred_element_type=F32)
            v8 = v8 - part.astype(F32)
        return out[0:1, :]

    xdt = xs * expand(dt)
    dax = expand(da)

    lane = lax.broadcasted_iota(jnp.int32, (HEAD_DIM, LANES), 1)
    sub = lax.broadcasted_iota(jnp.int32, (HEAD_DIM, LANES), 0)
    eye2 = (lane % HEAD_DIM) == sub
    is_a = lane < HEAD_DIM

    def to_cols(row):
        mat = jnp.where(eye2, jnp.broadcast_to(row, (HEAD_DIM, LANES)), 0.0)
        col_a = jnp.sum(jnp.where(is_a, mat, 0.0), axis=1, keepdims=True)
        col_b = jnp.sum(jnp.where(is_a, 0.0, mat), axis=1, keepdims=True)
        return col_a, col_b

    y_rows = []
    for hp in range(N_HEADS // 2):
        g = hp // 4
        b_row = xc[:, D_SSD + g * D_STATE:D_SSD + (g + 1) * D_STATE]
        c_row = xc[:, D_SSD + (N_GROUPS + g) * D_STATE:D_SSD + (N_GROUPS + g + 1) * D_STATE]
        cols = slice(hp * LANES, (hp + 1) * LANES)
        x_cols = to_cols(xdt[:, cols])
        d_cols = to_cols(dax[:, cols])
        y_cols = []
        for which in range(2):
            h = hp * 2 + which
            h_new = h_ref[h] * d_cols[which] + x_cols[which] * b_row
            h_out_ref[h] = h_new
            y_cols.append(jnp.sum(h_new * c_row, axis=1, keepdims=True))
        y_mat = jnp.where(eye2, jnp.where(is_a, y_cols[0], y_cols[1]), 0.0)
        y_rows.append(jnp.sum(y_mat, axis=0, keepdims=True))
    y = jnp.concatenate(y_rows, axis=1) + dsk_ref[...] * xs
    zf = z_ref[...].astype(F32)
    yz = y * (zf * _sigmoid(zf))
    var = jnp.mean(yz * yz, axis=-1, keepdims=True)
    s_ref[...] = (yz * lax.rsqrt(var + EPS) * nw_ref[...]).astype(s_ref.dtype)


def _ssd_sample(xbc3, z3, dt3, state_conv, state_ssm, cw, cb, dtb, alog, dsk, nw, emat):
    b = xbc3.shape[0]
    const = lambda i: (0, 0)
    tok = lambda width: pl.BlockSpec((None, 1, width), lambda i: (i, 0, 0))
    conv_spec = pl.BlockSpec((None, CONV_W - 1, CONV_DIM), lambda i: (i, 0, 0))
    ssm_spec = pl.BlockSpec((None, N_HEADS, HEAD_DIM, D_STATE), lambda i: (i, 0, 0, 0))
    return pl.pallas_call(
        _ssd_sample_kernel,
        grid=(b,),
        in_specs=[
            tok(CONV_DIM), tok(D_SSD), tok(LANES), conv_spec, ssm_spec,
            pl.BlockSpec((CONV_W, CONV_DIM), const),
            pl.BlockSpec((1, CONV_DIM), const),
            pl.BlockSpec((1, LANES), const),
            pl.BlockSpec((1, LANES), const),
            pl.BlockSpec((1, D_SSD), const),
            pl.BlockSpec((1, D_SSD), const),
            pl.BlockSpec((LANES, D_SSD), const),
        ],
        out_specs=[tok(D_SSD), conv_spec, ssm_spec],
        out_shape=(jax.ShapeDtypeStruct((b, 1, D_SSD), BF16),
                   jax.ShapeDtypeStruct((b, CONV_W - 1, CONV_DIM), F32),
                   jax.ShapeDtypeStruct((b, N_HEADS, HEAD_DIM, D_STATE), F32)),
        compiler_params=pltpu.CompilerParams(
            dimension_semantics=("arbitrary",), vmem_limit_bytes=VMEM_LIMIT),
    )(xbc3, z3, dt3, state_conv, state_ssm, cw, cb, dtb, alog, dsk, nw, emat)


def kernel(x_prompt, x_sample, cache_win_k, cache_win_v, state_conv, state_ssm, norm_w, w_in, q_norm_w,
           k_norm_w, rel_bias, conv_w, conv_b, dt_bias, a_log, d_skip, ssd_norm_w, w_out):
    assert x_prompt.shape[-1] == D_MODEL and w_in.shape[0] == 1, "single-layer model of width 1024 only"
    batch, seq, _ = x_prompt.shape
    dec_batch, dec_seq, _ = x_sample.shape
    assert dec_seq == 1 and seq % SPAN == 0 and cache_win_k.shape[2] == WINDOW_MAX

    w_pad = jnp.pad(w_in[0], ((0, 0), (0, D_IN_PAD - D_IN_PROJ))).astype(BF16)
    w_out_b = w_out[0].astype(BF16)
    nw = norm_w[0].reshape(1, D_MODEL)
    qnw = jnp.tile(q_norm_w[0], 256 // HEAD_DIM).reshape(1, 256)
    knw = jnp.tile(k_norm_w[0], 256 // HEAD_DIM).reshape(1, 256)
    idx256 = jnp.arange(256) // HEAD_DIM
    gsum = (idx256[:, None] == idx256[None, :]).astype(BF16)
    cw, cb = conv_w[0], conv_b[0].reshape(1, CONV_DIM)
    pad_heads = lambda a: jnp.pad(a.reshape(1, N_HEADS), ((0, 0), (0, LANES - N_HEADS)))
    dtb, alog = pad_heads(dt_bias[0]), pad_heads(a_log[0])
    dsk = jnp.repeat(d_skip[0], HEAD_DIM).reshape(1, D_SSD)
    snw = ssd_norm_w[0].reshape(1, D_SSD)
    emat = (jnp.arange(LANES)[:, None] == (jnp.arange(D_SSD) // HEAD_DIM)[None, :]).astype(BF16)
    tril = (jnp.arange(CHUNK)[:, None] >= jnp.arange(CHUNK)[None, :]).astype(BF16)

    xp = x_prompt.reshape(batch * seq, D_MODEL)
    q, k, v, g, z, xbc, dt = _inproj(xp, nw, w_pad, qnw, knw, gsum, tm=256)
    a = _attn_prompt(q, k, v, g, _prompt_bias_table(rel_bias), batch, seq)
    s, h_fin = _ssd_prompt(xbc, z, dt, cw, cb, dtb, alog, dsk, snw, emat, tril, batch, seq)
    y_p = _outproj(xp, a, s, w_out_b, tm=512).reshape(batch, seq, D_MODEL)
    nwin = min(WINDOW_MAX, seq)
    heads = lambda t: t.reshape(batch, seq, N_HEADS, HEAD_DIM)[None, :, seq - nwin:]
    kp, vp = heads(k), heads(v)
    cp = xbc.reshape(batch, seq, CONV_DIM)[None, :, seq - (CONV_W - 1):]
    hp = jnp.swapaxes(h_fin, 1, 2).reshape(batch, N_HEADS, HEAD_DIM, D_STATE)[None]

    xs2 = x_sample.reshape(dec_batch, D_MODEL)
    qs, ks, vs, gs, zs, xbcs, dts = _inproj(xs2, nw, w_pad, qnw, knw, gsum, tm=dec_batch)
    tok3 = lambda t: t.reshape(dec_batch, N_HEADS, HEAD_DIM)
    bias_s, bias0 = _sample_bias_tables(rel_bias)
    a_s = _attn_sample(tok3(qs), tok3(ks), tok3(vs), tok3(gs.astype(F32)), cache_win_k[0], cache_win_v[0],
                       bias_s, bias0)
    s_s, conv_s, h_s = _ssd_sample(xbcs.reshape(dec_batch, 1, CONV_DIM), zs.reshape(dec_batch, 1, D_SSD),
                                   dts.reshape(dec_batch, 1, LANES), state_conv[0], state_ssm[0],
                                   cw, cb, dtb, alog, dsk, snw, emat)
    y_s = _outproj(xs2, a_s.reshape(dec_batch, D_ATTN).astype(BF16), s_s.reshape(dec_batch, D_SSD),
                   w_out_b, tm=dec_batch).reshape(dec_batch, 1, D_MODEL)
    k_s = ks.reshape(1, dec_batch, 1, N_HEADS, HEAD_DIM)
    v_s = vs.reshape(1, dec_batch, 1, N_HEADS, HEAD_DIM)
    return (y_p, y_s, kp, vp, cp, hp, k_s, v_s, conv_s[None], h_s[None])
```

```python
import functools
import math

import jax
import jax.numpy as jnp
from jax import lax
from jax.experimental import pallas as pl
from jax.experimental.pallas import tpu as pltpu

F32 = jnp.float32
BF16 = jnp.bfloat16

D_MODEL = 1024
D_ATTN = 1024
D_SSD = 1024
HEAD_DIM = 64
N_HEADS = 16
PATTERNS = ((128, 1), (512, 4), (2048, 16))
WINDOW_MAX = 2048
BLK = 128
N_BUCKETS = 32
D_STATE = 128
N_GROUPS = 2
CONV_W = 4
CONV_DIM = D_SSD + 2 * N_GROUPS * D_STATE
CHUNK = 128
EPS = 1e-6
D_IN_PROJ = 4 * D_ATTN + D_SSD + CONV_DIM + N_HEADS
LANES = 128
D_IN_PAD = D_IN_PROJ - N_HEADS + LANES
SPAN = BLK * 16
NEG = -1e30
LOG2E = math.log2(math.e)
ATTN_GROUP = 4
SAMPLE_HEADS_PER_STEP = 8
VMEM_LIMIT = 56 * 1024 * 1024


def _sigmoid(x):
    return 1.0 / (1.0 + jnp.exp(-x))


def _softplus(x):
    return jnp.maximum(x, 0.0) + jnp.log1p(jnp.exp(-jnp.abs(x)))


def _t5_bucket(dist):
    max_exact = N_BUCKETS // 2
    d_f = jnp.maximum(dist, 1).astype(F32)
    large = max_exact + (jnp.log(d_f / max_exact) / math.log(WINDOW_MAX / max_exact)
                         * (N_BUCKETS - max_exact)).astype(jnp.int32)
    large = jnp.minimum(large, N_BUCKETS - 1)
    return jnp.where(dist < max_exact, dist, large)


def _inproj_kernel(x_ref, nw_ref, w_ref, qnw_ref, knw_ref, gsum_ref,
                   q_ref, k_ref, v_ref, g_ref, z_ref, xbc_ref, dt_ref):
    x = x_ref[...]
    ms = jnp.mean(x * x, axis=-1, keepdims=True)
    h = (x * lax.rsqrt(ms + EPS) * nw_ref[...]).astype(BF16)

    def seg(c0, width):
        return jnp.dot(h, w_ref[:, c0:c0 + width], preferred_element_type=F32)

    for out_ref, base, hw_ref in ((q_ref, 0, qnw_ref), (k_ref, D_ATTN, knw_ref)):
        for c in range(D_ATTN // 256):
            p = seg(base + 256 * c, 256)
            p2 = p * p
            p2_hi = p2.astype(BF16)
            p2_lo = (p2 - p2_hi.astype(F32)).astype(BF16)
            ss = (jnp.dot(p2_hi, gsum_ref[...], preferred_element_type=F32)
                  + jnp.dot(p2_lo, gsum_ref[...], preferred_element_type=F32))
            out_ref[:, 256 * c:256 * (c + 1)] = p * lax.rsqrt(ss * (1.0 / HEAD_DIM) + EPS) * hw_ref[...]
    for c in range(2):
        v_ref[:, 512 * c:512 * (c + 1)] = seg(2 * D_ATTN + 512 * c, 512)
        g_ref[:, 512 * c:512 * (c + 1)] = seg(3 * D_ATTN + 512 * c, 512).astype(g_ref.dtype)
        z_ref[:, 512 * c:512 * (c + 1)] = seg(4 * D_ATTN + 512 * c, 512).astype(z_ref.dtype)
    for c in range(3):
        xbc_ref[:, 512 * c:512 * (c + 1)] = seg(5 * D_ATTN + 512 * c, 512)
    dt_ref[...] = seg(5 * D_ATTN + CONV_DIM, LANES)


def _inproj(x2d, nw, w_pad, qnw, knw, gsum, tm):
    t = x2d.shape[0]
    row = lambda i: (i, 0)
    const = lambda i: (0, 0)
    outs = (
        jax.ShapeDtypeStruct((t, D_ATTN), F32),
        jax.ShapeDtypeStruct((t, D_ATTN), F32),
        jax.ShapeDtypeStruct((t, D_ATTN), F32),
        jax.ShapeDtypeStruct((t, D_ATTN), BF16),
        jax.ShapeDtypeStruct((t, D_SSD), BF16),
        jax.ShapeDtypeStruct((t, CONV_DIM), F32),
        jax.ShapeDtypeStruct((t, LANES), F32),
    )
    return pl.pallas_call(
        _inproj_kernel,
        grid=(t // tm,),
        in_specs=[
            pl.BlockSpec((tm, D_MODEL), row),
            pl.BlockSpec((1, D_MODEL), const),
            pl.BlockSpec((D_MODEL, D_IN_PAD), const, pipeline_mode=pl.Buffered(1)),
            pl.BlockSpec((1, 256), const),
            pl.BlockSpec((1, 256), const),
            pl.BlockSpec((256, 256), const),
        ],
        out_specs=[
            pl.BlockSpec((tm, D_ATTN), row),
            pl.BlockSpec((tm, D_ATTN), row),
            pl.BlockSpec((tm, D_ATTN), row),
            pl.BlockSpec((tm, D_ATTN), row),
            pl.BlockSpec((tm, D_SSD), row),
            pl.BlockSpec((tm, CONV_DIM), row),
            pl.BlockSpec((tm, LANES), row),
        ],
        out_shape=outs,
        compiler_params=pltpu.CompilerParams(
            dimension_semantics=("arbitrary",), vmem_limit_bytes=VMEM_LIMIT),
    )(x2d, nw, w_pad, qnw, knw, gsum)


def _outproj_kernel(x_ref, a_ref, s_ref, w_ref, y_ref):
    y_ref[...] = (x_ref[...]
                  + jnp.dot(a_ref[...], w_ref[0:D_ATTN, :], preferred_element_type=F32)
                  + jnp.dot(s_ref[...], w_ref[D_ATTN:, :], preferred_element_type=F32))


def _outproj(x2d, a, s, w_out_b, tm):
    t = x2d.shape[0]
    row = lambda i: (i, 0)
    return pl.pallas_call(
        _outproj_kernel,
        grid=(t // tm,),
        in_specs=[
            pl.BlockSpec((tm, D_MODEL), row),
            pl.BlockSpec((tm, D_ATTN), row),
            pl.BlockSpec((tm, D_SSD), row),
            pl.BlockSpec((D_ATTN + D_SSD, D_MODEL), lambda i: (0, 0), pipeline_mode=pl.Buffered(1)),
        ],
        out_specs=pl.BlockSpec((tm, D_MODEL), row),
        out_shape=jax.ShapeDtypeStruct((t, D_MODEL), F32),
        compiler_params=pltpu.CompilerParams(
            dimension_semantics=("arbitrary",), vmem_limit_bytes=VMEM_LIMIT),
    )(x2d, a, s, w_out_b)


def _attn_kernel(q_ref, kp_ref, kc_ref, vp_ref, vc_ref, g_ref, bias_ref, o_ref, m_s, l_s, acc_s):
    first = (pl.program_id(2) == 0).astype(jnp.int32)
    lane = lax.broadcasted_iota(jnp.int32, (BLK, LANES), 1)
    is_a = lane < HEAD_DIM
    qscale = HEAD_DIM ** -0.5 * LOG2E
    ones = jnp.ones((2 * BLK, LANES), BF16)

    def block(qs, ka, kb, va, vb, bias):
        qsc = qs * qscale
        qq = jnp.concatenate([jnp.where(is_a, qsc, 0.0), jnp.where(is_a, 0.0, qsc)], axis=0).astype(BF16)
        kk = jnp.concatenate([ka, kb], axis=0).astype(BF16)
        s = lax.dot_general(qq, kk, (((1,), (1,)), ((), ())), preferred_element_type=F32) + bias
        m = jnp.max(s, axis=1, keepdims=True)
        p = jnp.exp2(s - m).astype(BF16)
        vv = jnp.concatenate([jnp.concatenate([va, vb], axis=0).astype(BF16), ones], axis=1)
        ol = jnp.dot(p, vv, preferred_element_type=F32)
        o_c = jnp.where(is_a, ol[:BLK, :LANES], ol[BLK:, :LANES])
        l_c = jnp.where(is_a, ol[:BLK, LANES:], ol[BLK:, LANES:])
        m_c = jnp.where(is_a, m[:BLK], m[BLK:])
        return o_c, m_c, l_c

    def rows(start, d):
        if d == 1:
            return pl.ds(start, BLK)
        return pl.ds(start, BLK, stride=d)

    def run(pi, d, q_start, prev_k, prev_v, prev_start, bias):
        qi = rows(q_start, d)
        pi_rows = rows(prev_start, d)
        o_c, m_c, l_c = block(q_ref[qi, :], prev_k[pi_rows, :], kc_ref[qi, :],
                              prev_v[pi_rows, :], vc_ref[qi, :], bias)
        m_s[pi, qi, :] = m_c
        l_s[pi, qi, :] = l_c
        acc_s[pi, qi, :] = o_c

    for pi, (_, d) in enumerate(PATTERNS):
        sub = BLK * d
        n_sub = SPAN // sub

        g_first = min(d, ATTN_GROUP)

        def body_first(i, carry, pi=pi, d=d, sub=sub, g_first=g_first):
            for j in range(g_first):
                r = i * g_first + j
                run(pi, d, r, kp_ref, vp_ref, SPAN - sub + r, bias_ref[first, pi])
            return carry
        lax.fori_loop(0, d // g_first, body_first, 0)

        n_rest = (n_sub - 1) * d
        if n_rest:
            g_rest = max(g for g in range(1, ATTN_GROUP + 1) if n_rest % g == 0)

            def body_rest(i, carry, pi=pi, d=d, sub=sub, g_rest=g_rest):
                for j in range(g_rest):
                    idx = i * g_rest + j
                    start = (idx // d + 1) * sub + idx % d
                    if d == 1:
                        start = pl.multiple_of(start, BLK)
                    run(pi, d, start, kc_ref, vc_ref, start - sub, bias_ref[0, pi])
                return carry
            lax.fori_loop(0, n_rest // g_rest, body_rest, 0)

    m = jnp.maximum(jnp.maximum(m_s[0], m_s[1]), m_s[2])
    l = jnp.zeros((SPAN, LANES), F32)
    acc = jnp.zeros((SPAN, LANES), F32)
    for pi in range(len(PATTERNS)):
        e = jnp.exp2(m_s[pi] - m)
        l = l + l_s[pi] * e
        acc = acc + acc_s[pi] * e
    g = g_ref[...].astype(F32)
    o_ref[...] = (acc / l * (g * _sigmoid(g))).astype(o_ref.dtype)


def _attn_prompt(q, k, v, g, bias_tbl, batch, seq):
    n_span = seq // SPAN
    n_hp = N_HEADS // 2
    cur = lambda hp, b, s: (b * n_span + s, hp)
    prev = lambda hp, b, s: (b * n_span + jnp.maximum(s - 1, 0), hp)
    blk = (SPAN, LANES)
    return pl.pallas_call(
        _attn_kernel,
        grid=(n_hp, batch, n_span),
        in_specs=[
            pl.BlockSpec(blk, cur),
            pl.BlockSpec(blk, prev),
            pl.BlockSpec(blk, cur),
            pl.BlockSpec(blk, prev),
            pl.BlockSpec(blk, cur),
            pl.BlockSpec(blk, cur),
            pl.BlockSpec((None, 2, len(PATTERNS), 2 * BLK, 2 * BLK), lambda hp, b, s: (hp, 0, 0, 0, 0)),
        ],
        out_specs=pl.BlockSpec(blk, cur),
        out_shape=jax.ShapeDtypeStruct((batch * seq, D_ATTN), BF16),
        scratch_shapes=[pltpu.VMEM((len(PATTERNS),) + blk, F32) for _ in range(3)],
        compiler_params=pltpu.CompilerParams(
            dimension_semantics=("arbitrary", "arbitrary", "arbitrary"), vmem_limit_bytes=VMEM_LIMIT),
    )(q, k, k, v, v, g, bias_tbl)


def _bias_lookup(rel_bias, dist):
    onehot = (_t5_bucket(dist)[..., None] == jnp.arange(N_BUCKETS)).astype(F32)
    return jnp.einsum('...b,bh->h...', onehot, rel_bias.astype(F32), precision=lax.Precision.HIGHEST)


def _prompt_bias_table(rel_bias):
    i = jnp.arange(BLK)[:, None]
    j = jnp.arange(2 * BLK)[None, :]
    rel = i + BLK - j
    tbls = []
    for w, d in PATTERNS:
        band = (rel >= 0) & (rel <= w // d)
        bias = _bias_lookup(rel_bias, jnp.maximum(rel, 0) * d) * LOG2E
        normal = jnp.where(band[None], bias, NEG)
        first = jnp.where((band & (j >= BLK))[None], bias, NEG)
        tbls.append(jnp.stack([normal, first], axis=0))
    t = jnp.stack(tbls, axis=1)
    t = t.reshape(2, len(PATTERNS), N_HEADS // 2, 2 * BLK, 2 * BLK)
    return jnp.moveaxis(t, 2, 0)


def _ssd_kernel(xbc_ref, z_ref, dt_ref, cw_ref, cb_ref, dtb_ref, alog_ref, dsk_ref, nw_ref,
                e_ref, tril_ref, s_ref, h_ref, cbuf):
    c = pl.program_id(1)

    @pl.when(c == 0)
    def _():
        cbuf[0:8, :] = jnp.zeros((8, CONV_DIM), F32)
        h_ref[...] = jnp.zeros_like(h_ref)

    cbuf[8:8 + CHUNK, :] = xbc_ref[...]
    acc = cb_ref[...] + cbuf[5:5 + CHUNK, :] * cw_ref[0:1, :]
    for i in range(1, CONV_W):
        acc = acc + cbuf[5 + i:5 + i + CHUNK, :] * cw_ref[i:i + 1, :]
    xc = acc * _sigmoid(acc)
    cbuf[0:8, :] = cbuf[CHUNK:CHUNK + 8, :]

    xs = xc[:, :D_SSD]
    lane = lax.broadcasted_iota(jnp.int32, (CHUNK, LANES), 1)
    sub = lax.broadcasted_iota(jnp.int32, (CHUNK, LANES), 0)
    head_lane = lane < N_HEADS
    dt = jnp.where(head_lane, _softplus(dt_ref[...] + dtb_ref[...]), 0.0)
    la = dt * (-jnp.exp(alog_ref[...]))
    la_hi = la.astype(BF16)
    la_lo = (la - la_hi.astype(F32)).astype(BF16)
    tril = tril_ref[...]
    a_cs = (jnp.dot(tril, la_hi, preferred_element_type=F32)
            + jnp.dot(tril, la_lo, preferred_element_type=F32))
    ea = jnp.where(head_lane, jnp.exp(a_cs), 0.0)
    dte = jnp.where(head_lane, jnp.exp(a_cs[CHUNK - 1:CHUNK, :] - a_cs), 0.0)

    def expand(val):
        hi = val.astype(BF16)
        lo = (val - hi.astype(F32)).astype(BF16)
        return (jnp.dot(hi, e_ref[...], preferred_element_type=F32)
                + jnp.dot(lo, e_ref[...], preferred_element_type=F32))

    dtx = expand(dt)
    eax = expand(ea)
    dtex = expand(dte)
    xdt_f = xs * dtx
    xdt = xdt_f.astype(BF16)
    xdte = (xdt_f * dtex).astype(BF16)

    a_cs_t = a_cs.T
    causal = sub >= lane
    is_a = lane < HEAD_DIM
    ys = []
    for g in range(N_GROUPS):
        b_g = xc[:, D_SSD + g * D_STATE:D_SSD + (g + 1) * D_STATE]
        c_g = xc[:, D_SSD + (N_GROUPS + g) * D_STATE:D_SSD + (N_GROUPS + g + 1) * D_STATE]
        b_bf = b_g.astype(BF16)
        c_bf = c_g.astype(BF16)
        cb = lax.dot_general(c_bf, b_bf, (((1,), (1,)), ((), ())), preferred_element_type=F32)
        gcols = slice(g * 512, (g + 1) * 512)
        h_prev = h_ref[:, gcols]
        y_off = jnp.dot(c_bf, h_prev.astype(BF16), preferred_element_type=F32) * eax[:, gcols]
        st = jnp.dot(b_g.T.astype(BF16), xdte[:, gcols], preferred_element_type=F32)
        h_ref[:, gcols] = h_prev * eax[CHUNK - 1:CHUNK, gcols] + st
        for hp in range(4):
            pair = []
            for which in range(2):
                h = g * 8 + hp * 2 + which
                col = jnp.sum(jnp.where(lane == h, a_cs, 0.0), axis=1, keepdims=True)
                seg = col - a_cs_t[h:h + 1, :]
                lmat = jnp.exp(jnp.where(causal, seg, NEG))
                pair.append((cb * lmat).astype(BF16))
            x_pair = xdt[:, g * 512 + hp * LANES:g * 512 + (hp + 1) * LANES]
            y_a = jnp.dot(pair[0], x_pair, preferred_element_type=F32)
            y_b = jnp.dot(pair[1], x_pair, preferred_element_type=F32)
            ys.append(jnp.where(is_a, y_a, y_b) + y_off[:, hp * LANES:(hp + 1) * LANES])
    y = jnp.concatenate(ys, axis=1) + dsk_ref[...] * xs
    zf = z_ref[...].astype(F32)
    yz = y * (zf * _sigmoid(zf))
    var = jnp.mean(yz * yz, axis=-1, keepdims=True)
    s_ref[...] = (yz * lax.rsqrt(var + EPS) * nw_ref[...]).astype(s_ref.dtype)


def _ssd_prompt(xbc, z, dt, cw, cb, dtb, alog, dsk, nw, emat, tril, batch, seq):
    nc = seq // CHUNK
    row = lambda b, c: (b * nc + c, 0)
    const = lambda b, c: (0, 0)
    return pl.pallas_call(
        _ssd_kernel,
        grid=(batch, nc),
        in_specs=[
            pl.BlockSpec((CHUNK, CONV_DIM), row),
            pl.BlockSpec((CHUNK, D_SSD), row),
            pl.BlockSpec((CHUNK, LANES), row),
            pl.BlockSpec((CONV_W, CONV_DIM), const),
            pl.BlockSpec((1, CONV_DIM), const),
            pl.BlockSpec((1, LANES), const),
            pl.BlockSpec((1, LANES), const),
            pl.BlockSpec((1, D_SSD), const),
            pl.BlockSpec((1, D_SSD), const),
            pl.BlockSpec((LANES, D_SSD), const),
            pl.BlockSpec((CHUNK, CHUNK), const),
        ],
        out_specs=[
            pl.BlockSpec((CHUNK, D_SSD), row),
            pl.BlockSpec((None, D_STATE, D_SSD), lambda b, c: (b, 0, 0)),
        ],
        out_shape=(jax.ShapeDtypeStruct((batch * seq, D_SSD), BF16),
                   jax.ShapeDtypeStruct((batch, D_STATE, D_SSD), F32)),
        scratch_shapes=[pltpu.VMEM((CHUNK + 8, CONV_DIM), F32)],
        compiler_params=pltpu.CompilerParams(
            dimension_semantics=("arbitrary", "arbitrary"), vmem_limit_bytes=VMEM_LIMIT),
    )(xbc, z, dt, cw, cb, dtb, alog, dsk, nw, emat, tril)


def _attn_sample_kernel(qt_ref, knt_ref, vnt_ref, gt_ref, k_ref, v_ref, btbl_ref, bias0_ref, o_ref):
    hh = pl.program_id(1)

    @pl.when(hh == 0)
    def _():
        o_ref[...] = jnp.zeros_like(o_ref)

    lane = lax.broadcasted_iota(jnp.int32, (HEAD_DIM, LANES), 1)
    lane1 = lax.broadcasted_iota(jnp.int32, (1, LANES), 1)
    qt = qt_ref[...] * (HEAD_DIM ** -0.5)
    n_pat = float(len(PATTERNS))
    for j in range(SAMPLE_HEADS_PER_STEP):
        h = hh * SAMPLE_HEADS_PER_STEP + j
        pick = lane == h

        def col(val, pick=pick):
            return jnp.sum(jnp.where(pick, val, 0.0), axis=1, keepdims=True)

        qc, knc, vnc, gc = col(qt), col(knt_ref[...]), col(vnt_ref[...]), col(gt_ref[...])
        b0 = jnp.sum(jnp.where(lane1 == h, bias0_ref[...], 0.0), axis=1, keepdims=True)
        s0 = jnp.sum(qc * knc, axis=0, keepdims=True) + b0
        s = jnp.sum(k_ref[j] * qc, axis=0, keepdims=True)
        sp = [s + btbl_ref[pi, pl.ds(h, 1), :] for pi in range(len(PATTERNS))]
        m = s0
        for x in sp:
            m = jnp.maximum(m, jnp.max(x, axis=1, keepdims=True))
        p0 = n_pat * jnp.exp(s0 - m)
        pw = jnp.exp(sp[0] - m)
        for x in sp[1:]:
            pw = pw + jnp.exp(x - m)
        l = jnp.sum(pw, axis=1, keepdims=True) + p0
        oc = (jnp.sum(v_ref[j] * pw, axis=1, keepdims=True) + p0 * vnc) / l
        o_ref[...] = jnp.where(pick, oc * (gc * _sigmoid(gc)), o_ref[...])


def _attn_sample(qt, knt, vnt, gt, cache_k_t, cache_v_t, btbl, bias0):
    b, n_past = cache_k_t.shape[0], cache_k_t.shape[3]
    tok = pl.BlockSpec((None, HEAD_DIM, LANES), lambda i, hh: (i, 0, 0))
    cache = pl.BlockSpec((None, SAMPLE_HEADS_PER_STEP, HEAD_DIM, n_past), lambda i, hh: (i, hh, 0, 0))
    return pl.pallas_call(
        _attn_sample_kernel,
        grid=(b, N_HEADS // SAMPLE_HEADS_PER_STEP),
        in_specs=[tok, tok, tok, tok, cache, cache,
                  pl.BlockSpec((len(PATTERNS), N_HEADS, n_past), lambda i, hh: (0, 0, 0)),
                  pl.BlockSpec((1, LANES), lambda i, hh: (0, 0))],
        out_specs=tok,
        out_shape=jax.ShapeDtypeStruct((b, HEAD_DIM, LANES), F32),
        compiler_params=pltpu.CompilerParams(
            dimension_semantics=("arbitrary", "arbitrary"), vmem_limit_bytes=VMEM_LIMIT),
    )(qt, knt, vnt, gt, cache_k_t, cache_v_t, btbl, bias0)


def _sample_bias_tables(rel_bias, n_past):
    dist = n_past - jnp.arange(n_past)
    bias = _bias_lookup(rel_bias, dist)
    tbls = [jnp.where(((dist % d == 0) & (dist <= w))[None], bias, NEG) for w, d in PATTERNS]
    bias0 = _bias_lookup(rel_bias, jnp.zeros((1,), jnp.int32))
    return jnp.stack(tbls, axis=0), jnp.pad(bias0.reshape(1, N_HEADS), ((0, 0), (0, LANES - N_HEADS)))


def _ssd_sample_kernel(xbc_ref, z_ref, dt_ref, sc_ref, h_ref, cw_ref, cb_ref, dtb_ref, alog_ref, dsk_ref,
                       nw_ref, e_ref, s_ref, conv_out_ref, h_out_ref):
    xnew = xbc_ref[...]
    sc = sc_ref[...]
    acc = cb_ref[...] + xnew * cw_ref[CONV_W - 1:CONV_W, :]
    for i in range(CONV_W - 1):
        acc = acc + sc[i:i + 1, :] * cw_ref[i:i + 1, :]
    xc = acc * _sigmoid(acc)
    conv_out_ref[0:CONV_W - 2, :] = sc[1:CONV_W - 1, :]
    conv_out_ref[CONV_W - 2:CONV_W - 1, :] = xnew

    xs = xc[:, :D_SSD]
    lane1 = lax.broadcasted_iota(jnp.int32, (1, LANES), 1)
    dt = jnp.where(lane1 < N_HEADS, _softplus(dt_ref[...] + dtb_ref[...]), 0.0)
    da = jnp.where(lane1 < N_HEADS, jnp.exp(dt * (-jnp.exp(alog_ref[...]))), 0.0)

    def expand(val):
        v8 = jnp.broadcast_to(val, (8, LANES))
        out = jnp.zeros((8, D_SSD), F32)
        for _ in range(3):
            part = v8.astype(BF16)
            out = out + jnp.dot(part, e_ref[...], preferred_element_type=F32)
            v8 = v8 - part.astype(F32)
        return out[0:1, :]

    xdt = xs * expand(dt)
    dax = expand(da)

    lane = lax.broadcasted_iota(jnp.int32, (HEAD_DIM, LANES), 1)
    sub = lax.broadcasted_iota(jnp.int32, (HEAD_DIM, LANES), 0)
    eye2 = (lane % HEAD_DIM) == sub
    is_a = lane < HEAD_DIM

    def to_cols(row):
        mat = jnp.where(eye2, jnp.broadcast_to(row, (HEAD_DIM, LANES)), 0.0)
        col_a = jnp.sum(jnp.where(is_a, mat, 0.0), axis=1, keepdims=True)
        col_b = jnp.sum(jnp.where(is_a, 0.0, mat), axis=1, keepdims=True)
        return col_a, col_b

    y_rows = []
    for hp in range(N_HEADS // 2):
        g = hp // 4
        b_row = xc[:, D_SSD + g * D_STATE:D_SSD + (g + 1) * D_STATE]
        c_row = xc[:, D_SSD + (N_GROUPS + g) * D_STATE:D_SSD + (N_GROUPS + g + 1) * D_STATE]
        cols = slice(hp * LANES, (hp + 1) * LANES)
        x_cols = to_cols(xdt[:, cols])
        d_cols = to_cols(dax[:, cols])
        y_cols = []
        for which in range(2):
            h = hp * 2 + which
            h_new = h_ref[h] * d_cols[which] + x_cols[which] * b_row
            h_out_ref[h] = h_new
            y_cols.append(jnp.sum(h_new * c_row, axis=1, keepdims=True))
        y_mat = jnp.where(eye2, jnp.where(is_a, y_cols[0], y_cols[1]), 0.0)
        y_rows.append(jnp.sum(y_mat, axis=0, keepdims=True))
    y = jnp.concatenate(y_rows, axis=1) + dsk_ref[...] * xs
    zf = z_ref[...].astype(F32)
    yz = y * (zf * _sigmoid(zf))
    var = jnp.mean(yz * yz, axis=-1, keepdims=True)
    s_ref[...] = (yz * lax.rsqrt(var + EPS) * nw_ref[...]).astype(s_ref.dtype)


def _ssd_sample(xbc3, z3, dt3, state_conv, state_ssm, cw, cb, dtb, alog, dsk, nw, emat):
    b = xbc3.shape[0]
    const = lambda i: (0, 0)
    tok = lambda width: pl.BlockSpec((None, 1, width), lambda i: (i, 0, 0))
    conv_spec = pl.BlockSpec((None, CONV_W - 1, CONV_DIM), lambda i: (i, 0, 0))
    ssm_spec = pl.BlockSpec((None, N_HEADS, HEAD_DIM, D_STATE), lambda i: (i, 0, 0, 0))
    return pl.pallas_call(
        _ssd_sample_kernel,
        grid=(b,),
        in_specs=[
            tok(CONV_DIM), tok(D_SSD), tok(LANES), conv_spec, ssm_spec,
            pl.BlockSpec((CONV_W, CONV_DIM), const),
            pl.BlockSpec((1, CONV_DIM), const),
            pl.BlockSpec((1, LANES), const),
            pl.BlockSpec((1, LANES), const),
            pl.BlockSpec((1, D_SSD), const),
            pl.BlockSpec((1, D_SSD), const),
            pl.BlockSpec((LANES, D_SSD), const),
        ],
        out_specs=[tok(D_SSD), conv_spec, ssm_spec],
        out_shape=(jax.ShapeDtypeStruct((b, 1, D_SSD), BF16),
                   jax.ShapeDtypeStruct((b, CONV_W - 1, CONV_DIM), F32),
                   jax.ShapeDtypeStruct((b, N_HEADS, HEAD_DIM, D_STATE), F32)),
        compiler_params=pltpu.CompilerParams(
            dimension_semantics=("arbitrary",), vmem_limit_bytes=VMEM_LIMIT),
    )(xbc3, z3, dt3, state_conv, state_ssm, cw, cb, dtb, alog, dsk, nw, emat)


def kernel(x_prompt, x_sample, cache_win_k, cache_win_v, state_conv, state_ssm, norm_w, w_in, q_norm_w,
           k_norm_w, rel_bias, conv_w, conv_b, dt_bias, a_log, d_skip, ssd_norm_w, w_out):
    assert x_prompt.shape[-1] == D_MODEL and w_in.shape[0] == 1, "single-layer model of width 1024 only"
    batch, seq, _ = x_prompt.shape
    dec_batch, dec_seq, _ = x_sample.shape
    assert dec_seq == 1 and seq % SPAN == 0 and cache_win_k.shape[2] == WINDOW_MAX

    w_pad = jnp.pad(w_in[0], ((0, 0), (0, D_IN_PAD - D_IN_PROJ))).astype(BF16)
    w_out_b = w_out[0].astype(BF16)
    nw = norm_w[0].reshape(1, D_MODEL)
    qnw = jnp.tile(q_norm_w[0], 256 // HEAD_DIM).reshape(1, 256)
    knw = jnp.tile(k_norm_w[0], 256 // HEAD_DIM).reshape(1, 256)
    idx256 = jnp.arange(256) // HEAD_DIM
    gsum = (idx256[:, None] == idx256[None, :]).astype(BF16)
    cw, cb = conv_w[0], conv_b[0].reshape(1, CONV_DIM)
    pad_heads = lambda a: jnp.pad(a.reshape(1, N_HEADS), ((0, 0), (0, LANES - N_HEADS)))
    dtb, alog = pad_heads(dt_bias[0]), pad_heads(a_log[0])
    dsk = jnp.repeat(d_skip[0], HEAD_DIM).reshape(1, D_SSD)
    snw = ssd_norm_w[0].reshape(1, D_SSD)
    emat = (jnp.arange(LANES)[:, None] == (jnp.arange(D_SSD) // HEAD_DIM)[None, :]).astype(BF16)
    tril = (jnp.arange(CHUNK)[:, None] >= jnp.arange(CHUNK)[None, :]).astype(BF16)

    xp = x_prompt.reshape(batch * seq, D_MODEL)
    q, k, v, g, z, xbc, dt = _inproj(xp, nw, w_pad, qnw, knw, gsum, tm=256)
    a = _attn_prompt(q, k, v, g, _prompt_bias_table(rel_bias), batch, seq)
    s, h_fin = _ssd_prompt(xbc, z, dt, cw, cb, dtb, alog, dsk, snw, emat, tril, batch, seq)
    y_p = _outproj(xp, a, s, w_out_b, tm=512).reshape(batch, seq, D_MODEL)
    nwin = min(WINDOW_MAX, seq)
    heads = lambda t: t.reshape(batch, seq, D_ATTN)[:, seq - nwin:].reshape(1, batch, nwin, N_HEADS, HEAD_DIM)
    kp, vp = heads(k), heads(v)
    cp = xbc.reshape(batch, seq, CONV_DIM)[None, :, seq - (CONV_W - 1):]
    hp = jnp.swapaxes(h_fin, 1, 2).reshape(batch, N_HEADS, HEAD_DIM, D_STATE)[None]

    xs2 = x_sample.reshape(dec_batch, D_MODEL)
    qs, ks, vs, gs, zs, xbcs, dts = _inproj(xs2, nw, w_pad, qnw, knw, gsum, tm=dec_batch)
    tok_t = lambda t: jnp.pad(jnp.swapaxes(t.astype(F32).reshape(dec_batch, N_HEADS, HEAD_DIM), 1, 2),
                              ((0, 0), (0, 0), (0, LANES - N_HEADS)))
    cache_t = lambda c: jnp.transpose(c[0], (0, 2, 3, 1))
    btbl, bias0 = _sample_bias_tables(rel_bias, cache_win_k.shape[2])
    a_t = _attn_sample(tok_t(qs), tok_t(ks), tok_t(vs), tok_t(gs), cache_t(cache_win_k), cache_t(cache_win_v),
                       btbl, bias0)
    a_s = jnp.swapaxes(a_t[:, :, :N_HEADS], 1, 2)
    s_s, conv_s, h_s = _ssd_sample(xbcs.reshape(dec_batch, 1, CONV_DIM), zs.reshape(dec_batch, 1, D_SSD),
                                   dts.reshape(dec_batch, 1, LANES), state_conv[0], state_ssm[0],
                                   cw, cb, dtb, alog, dsk, snw, emat)
    y_s = _outproj(xs2, a_s.reshape(dec_batch, D_ATTN).astype(BF16), s_s.reshape(dec_batch, D_SSD),
                   w_out_b, tm=dec_batch).reshape(dec_batch, 1, D_MODEL)
    k_s = ks.reshape(1, dec_batch, 1, N_HEADS, HEAD_DIM)
    v_s = vs.reshape(1, dec_batch, 1, N_HEADS, HEAD_DIM)
    return (y_p, y_s, kp, vp, cp, hp, k_s, v_s, conv_s[None], h_s[None])
```

```python
import functools
import math

import jax
import jax.numpy as jnp
from jax import lax
from jax.experimental import pallas as pl
from jax.experimental.pallas import tpu as pltpu

F32 = jnp.float32
BF16 = jnp.bfloat16

D_MODEL = 1024
D_ATTN = 1024
D_SSD = 1024
HEAD_DIM = 64
N_HEADS = 16
PATTERNS = ((128, 1), (512, 4), (2048, 16))
WINDOW_MAX = 2048
BLK = 128
N_BUCKETS = 32
D_STATE = 128
N_GROUPS = 2
CONV_W = 4
CONV_DIM = D_SSD + 2 * N_GROUPS * D_STATE
CHUNK = 128
EPS = 1e-6
D_IN_PROJ = 4 * D_ATTN + D_SSD + CONV_DIM + N_HEADS
LANES = 128
D_IN_PAD = D_IN_PROJ - N_HEADS + LANES
SPAN = BLK * 16
NEG = -1e30
LOG2E = math.log2(math.e)
ATTN_GROUP = 8
SAMPLE_HEADS_PER_STEP = 8
VMEM_LIMIT = 56 * 1024 * 1024


def _sigmoid(x):
    return 1.0 / (1.0 + jnp.exp(-x))


def _softplus(x):
    return jnp.maximum(x, 0.0) + jnp.log(1.0 + jnp.exp(-jnp.abs(x)))


def _t5_bucket(dist):
    max_exact = N_BUCKETS // 2
    d_f = jnp.maximum(dist, 1).astype(F32)
    large = max_exact + (jnp.log(d_f / max_exact) / math.log(WINDOW_MAX / max_exact)
                         * (N_BUCKETS - max_exact)).astype(jnp.int32)
    large = jnp.minimum(large, N_BUCKETS - 1)
    return jnp.where(dist < max_exact, dist, large)


def _inproj_kernel(x_ref, nw_ref, w_ref, qnw_ref, knw_ref, gsum_ref,
                   q_ref, k_ref, v_ref, g_ref, z_ref, xbc_ref, dt_ref):
    x = x_ref[...]
    h = (x * nw_ref[...]).astype(BF16)
    r = lax.rsqrt(jnp.mean(x * x, axis=-1, keepdims=True) + EPS)

    def seg(c0, width):
        return jnp.dot(h, w_ref[:, c0:c0 + width], preferred_element_type=F32) * r

    for out_ref, base, hw_ref in ((q_ref, 0, qnw_ref), (k_ref, D_ATTN, knw_ref)):
        for c in range(2):
            p = seg(base + 512 * c, 512)
            p2 = (p * p).astype(BF16)
            ss = jnp.concatenate([jnp.dot(p2[:, :256], gsum_ref[...], preferred_element_type=F32),
                                  jnp.dot(p2[:, 256:], gsum_ref[...], preferred_element_type=F32)], axis=1)
            out_ref[:, 512 * c:512 * (c + 1)] = p * lax.rsqrt(ss * (1.0 / HEAD_DIM) + EPS) * hw_ref[...]
    for c in range(2):
        v_ref[:, 512 * c:512 * (c + 1)] = seg(2 * D_ATTN + 512 * c, 512)
        g_ref[:, 512 * c:512 * (c + 1)] = seg(3 * D_ATTN + 512 * c, 512).astype(g_ref.dtype)
        z_ref[:, 512 * c:512 * (c + 1)] = seg(4 * D_ATTN + 512 * c, 512).astype(z_ref.dtype)
    for c in range(3):
        xbc_ref[:, 512 * c:512 * (c + 1)] = seg(5 * D_ATTN + 512 * c, 512)
    dt_ref[...] = seg(5 * D_ATTN + CONV_DIM, LANES)


def _inproj(x2d, nw, w_pad, qnw, knw, gsum, tm):
    t = x2d.shape[0]
    row = lambda i: (i, 0)
    const = lambda i: (0, 0)
    outs = (
        jax.ShapeDtypeStruct((t, D_ATTN), F32),
        jax.ShapeDtypeStruct((t, D_ATTN), F32),
        jax.ShapeDtypeStruct((t, D_ATTN), F32),
        jax.ShapeDtypeStruct((t, D_ATTN), BF16),
        jax.ShapeDtypeStruct((t, D_SSD), BF16),
        jax.ShapeDtypeStruct((t, CONV_DIM), F32),
        jax.ShapeDtypeStruct((t, LANES), F32),
    )
    return pl.pallas_call(
        _inproj_kernel,
        grid=(t // tm,),
        in_specs=[
            pl.BlockSpec((tm, D_MODEL), row),
            pl.BlockSpec((1, D_MODEL), const),
            pl.BlockSpec((D_MODEL, D_IN_PAD), const, pipeline_mode=pl.Buffered(1)),
            pl.BlockSpec((1, 512), const),
            pl.BlockSpec((1, 512), const),
            pl.BlockSpec((256, 256), const),
        ],
        out_specs=[
            pl.BlockSpec((tm, D_ATTN), row),
            pl.BlockSpec((tm, D_ATTN), row),
            pl.BlockSpec((tm, D_ATTN), row),
            pl.BlockSpec((tm, D_ATTN), row),
            pl.BlockSpec((tm, D_SSD), row),
            pl.BlockSpec((tm, CONV_DIM), row),
            pl.BlockSpec((tm, LANES), row),
        ],
        out_shape=outs,
        compiler_params=pltpu.CompilerParams(
            dimension_semantics=("arbitrary",), vmem_limit_bytes=VMEM_LIMIT),
    )(x2d, nw, w_pad, qnw, knw, gsum)


def _outproj_kernel(x_ref, a_ref, s_ref, w_ref, y_ref):
    y_ref[...] = (x_ref[...]
                  + jnp.dot(a_ref[...], w_ref[0:D_ATTN, :], preferred_element_type=F32)
                  + jnp.dot(s_ref[...], w_ref[D_ATTN:, :], preferred_element_type=F32))


def _outproj(x2d, a, s, w_out_b, tm):
    t = x2d.shape[0]
    row = lambda i: (i, 0)
    return pl.pallas_call(
        _outproj_kernel,
        grid=(t // tm,),
        in_specs=[
            pl.BlockSpec((tm, D_MODEL), row),
            pl.BlockSpec((tm, D_ATTN), row),
            pl.BlockSpec((tm, D_SSD), row),
            pl.BlockSpec((D_ATTN + D_SSD, D_MODEL), lambda i: (0, 0), pipeline_mode=pl.Buffered(1)),
        ],
        out_specs=pl.BlockSpec((tm, D_MODEL), row),
        out_shape=jax.ShapeDtypeStruct((t, D_MODEL), F32),
        compiler_params=pltpu.CompilerParams(
            dimension_semantics=("arbitrary",), vmem_limit_bytes=VMEM_LIMIT),
    )(x2d, a, s, w_out_b)


def _attn_kernel(q_ref, kp_ref, kc_ref, vp_ref, vc_ref, g_ref, bias_ref, o_ref, m_s, l_s, acc_s):
    first = (pl.program_id(2) == 0).astype(jnp.int32)
    lane = lax.broadcasted_iota(jnp.int32, (BLK, LANES), 1)
    is_a = lane < HEAD_DIM
    qscale = HEAD_DIM ** -0.5 * LOG2E
    ones = jnp.ones((2 * BLK, LANES), BF16)

    def block(qs, ka, kb, va, vb, bias):
        qsc = qs * qscale
        qq = jnp.concatenate([jnp.where(is_a, qsc, 0.0), jnp.where(is_a, 0.0, qsc)], axis=0).astype(BF16)
        kk = jnp.concatenate([ka, kb], axis=0).astype(BF16)
        s = lax.dot_general(qq, kk, (((1,), (1,)), ((), ())), preferred_element_type=F32) + bias
        m = jnp.max(s, axis=1, keepdims=True)
        p = jnp.exp2(s - m).astype(BF16)
        vv = jnp.concatenate([jnp.concatenate([va, vb], axis=0).astype(BF16), ones], axis=1)
        ol = jnp.dot(p, vv, preferred_element_type=F32)
        o_c = jnp.where(is_a, ol[:BLK, :LANES], ol[BLK:, :LANES])
        l_c = jnp.where(is_a, ol[:BLK, LANES:], ol[BLK:, LANES:])
        m_c = jnp.where(is_a, m[:BLK], m[BLK:])
        return o_c, m_c, l_c

    def rows(start, d):
        if d == 1:
            return pl.ds(start, BLK)
        return pl.ds(start, BLK, stride=d)

    def run(pi, d, q_start, prev_k, prev_v, prev_start, bias):
        qi = rows(q_start, d)
        pi_rows = rows(prev_start, d)
        o_c, m_c, l_c = block(q_ref[qi, :], prev_k[pi_rows, :], kc_ref[qi, :],
                              prev_v[pi_rows, :], vc_ref[qi, :], bias)
        m_s[pi, qi, :] = m_c
        l_s[pi, qi, :] = l_c
        acc_s[pi, qi, :] = o_c

    for pi, (_, d) in enumerate(PATTERNS):
        sub = BLK * d
        n_sub = SPAN // sub

        g_first = min(d, ATTN_GROUP)

        def body_first(i, carry, pi=pi, d=d, sub=sub, g_first=g_first):
            for j in range(g_first):
                r = i * g_first + j
                run(pi, d, r, kp_ref, vp_ref, SPAN - sub + r, bias_ref[first, pi])
            return carry
        lax.fori_loop(0, d // g_first, body_first, 0)

        n_rest = (n_sub - 1) * d
        if n_rest:
            g_rest = max(g for g in range(1, ATTN_GROUP + 1) if n_rest % g == 0)

            def body_rest(i, carry, pi=pi, d=d, sub=sub, g_rest=g_rest):
                for j in range(g_rest):
                    idx = i * g_rest + j
                    start = (idx // d + 1) * sub + idx % d
                    if d == 1:
                        start = pl.multiple_of(start, BLK)
                    run(pi, d, start, kc_ref, vc_ref, start - sub, bias_ref[0, pi])
                return carry
            lax.fori_loop(0, n_rest // g_rest, body_rest, 0)

    m = jnp.maximum(jnp.maximum(m_s[0], m_s[1]), m_s[2])
    l = jnp.zeros((SPAN, LANES), F32)
    acc = jnp.zeros((SPAN, LANES), F32)
    for pi in range(len(PATTERNS)):
        e = jnp.exp2(m_s[pi] - m)
        l = l + l_s[pi] * e
        acc = acc + acc_s[pi] * e
    g = g_ref[...].astype(F32)
    o_ref[...] = (acc / l * (g * _sigmoid(g))).astype(o_ref.dtype)


def _attn_prompt(q, k, v, g, bias_tbl, batch, seq):
    n_span = seq // SPAN
    n_hp = N_HEADS // 2
    cur = lambda hp, b, s: (b * n_span + s, hp)
    prev = lambda hp, b, s: (b * n_span + jnp.maximum(s - 1, 0), hp)
    blk = (SPAN, LANES)
    return pl.pallas_call(
        _attn_kernel,
        grid=(n_hp, batch, n_span),
        in_specs=[
            pl.BlockSpec(blk, cur),
            pl.BlockSpec(blk, prev),
            pl.BlockSpec(blk, cur),
            pl.BlockSpec(blk, prev),
            pl.BlockSpec(blk, cur),
            pl.BlockSpec(blk, cur),
            pl.BlockSpec((None, 2, len(PATTERNS), 2 * BLK, 2 * BLK), lambda hp, b, s: (hp, 0, 0, 0, 0)),
        ],
        out_specs=pl.BlockSpec(blk, cur),
        out_shape=jax.ShapeDtypeStruct((batch * seq, D_ATTN), BF16),
        scratch_shapes=[pltpu.VMEM((len(PATTERNS),) + blk, F32) for _ in range(3)],
        compiler_params=pltpu.CompilerParams(
            dimension_semantics=("arbitrary", "arbitrary", "arbitrary"), vmem_limit_bytes=VMEM_LIMIT),
    )(q, k, k, v, v, g, bias_tbl)


def _bias_lookup(rel_bias, dist):
    onehot = (_t5_bucket(dist)[..., None] == jnp.arange(N_BUCKETS)).astype(F32)
    return jnp.einsum('...b,bh->h...', onehot, rel_bias.astype(F32), precision=lax.Precision.HIGHEST)


def _prompt_bias_table(rel_bias):
    i = jnp.arange(BLK)[:, None]
    j = jnp.arange(2 * BLK)[None, :]
    rel = i + BLK - j
    tbls = []
    for w, d in PATTERNS:
        band = (rel >= 0) & (rel <= w // d)
        bias = _bias_lookup(rel_bias, jnp.maximum(rel, 0) * d) * LOG2E
        normal = jnp.where(band[None], bias, NEG)
        first = jnp.where((band & (j >= BLK))[None], bias, NEG)
        tbls.append(jnp.stack([normal, first], axis=0))
    t = jnp.stack(tbls, axis=1)
    t = t.reshape(2, len(PATTERNS), N_HEADS // 2, 2 * BLK, 2 * BLK)
    return jnp.moveaxis(t, 2, 0)


def _ssd_kernel(xbc_ref, z_ref, dt_ref, cw_ref, cb_ref, dtb_ref, alog_ref, dsk_ref, nw_ref,
                e_ref, tril_ref, s_ref, h_ref, cbuf):
    c = pl.program_id(1)

    n_slab = CONV_DIM // LANES

    @pl.when(c == 0)
    def _():
        cbuf[:, 0:8, :] = jnp.zeros((n_slab, 8, LANES), F32)
        h_ref[...] = jnp.zeros_like(h_ref)

    xc_slabs = []
    for j in range(n_slab):
        cols = slice(j * LANES, (j + 1) * LANES)
        xj = xbc_ref[:, cols]
        cbuf[j, 8:8 + CHUNK, :] = xj
        acc = cb_ref[:, cols] + xj * cw_ref[CONV_W - 1:CONV_W, cols]
        for i in range(CONV_W - 1):
            tap = cbuf[pl.ds(j, 1, stride=2), pl.ds(8 - (CONV_W - 1) + i, CHUNK), :][0]
            acc = acc + tap * cw_ref[i:i + 1, cols]
        xc_slabs.append(acc * _sigmoid(acc))
        cbuf[j, 0:8, :] = xj[CHUNK - 8:, :]

    n_x = D_SSD // LANES
    xs = jnp.concatenate(xc_slabs[:n_x], axis=1)
    lane = lax.broadcasted_iota(jnp.int32, (CHUNK, LANES), 1)
    sub = lax.broadcasted_iota(jnp.int32, (CHUNK, LANES), 0)
    head_lane = lane < N_HEADS
    dt = jnp.where(head_lane, _softplus(dt_ref[...] + dtb_ref[...]), 0.0)
    la = dt * (-jnp.exp(alog_ref[...]))

    def hi_lo(val):
        hi = val.astype(BF16).astype(F32)
        return (hi + pltpu.roll(val - hi, N_HEADS, axis=1)).astype(BF16)

    cs2 = jnp.dot(tril_ref[...], hi_lo(la), preferred_element_type=F32)
    a_cs = jnp.where(head_lane, cs2 + pltpu.roll(cs2, LANES - N_HEADS, axis=1), 0.0)
    ea = jnp.where(head_lane, jnp.exp(a_cs), 0.0)
    dte = jnp.where(head_lane, jnp.exp(a_cs[CHUNK - 1:CHUNK, :] - a_cs), 0.0)

    expanded = jnp.dot(jnp.concatenate([hi_lo(dt), hi_lo(ea), hi_lo(dte)], axis=0), e_ref[...],
                       preferred_element_type=F32)
    dtx, eax, dtex = expanded[:CHUNK], expanded[CHUNK:2 * CHUNK], expanded[2 * CHUNK:]
    xdt_f = xs * dtx
    xdt = xdt_f.astype(BF16)
    xdte = (xdt_f * dtex).astype(BF16)

    a_cs_t = a_cs.T
    causal = sub >= lane
    is_a = lane < HEAD_DIM
    ys = []
    for g in range(N_GROUPS):
        b_g = xc_slabs[n_x + g]
        c_g = xc_slabs[n_x + N_GROUPS + g]
        b_bf = b_g.astype(BF16)
        c_bf = c_g.astype(BF16)
        cb = lax.dot_general(c_bf, b_bf, (((1,), (1,)), ((), ())), preferred_element_type=F32)
        gcols = slice(g * 512, (g + 1) * 512)
        h_prev = h_ref[:, gcols]
        y_off = jnp.dot(c_bf, h_prev.astype(BF16), preferred_element_type=F32) * eax[:, gcols]
        st = jnp.dot(b_g.T.astype(BF16), xdte[:, gcols], preferred_element_type=F32)
        h_ref[:, gcols] = h_prev * eax[CHUNK - 1:CHUNK, gcols] + st
        for hp in range(4):
            pair = []
            for which in range(2):
                h = g * 8 + hp * 2 + which
                col = jnp.sum(jnp.where(lane == h, a_cs, 0.0), axis=1, keepdims=True)
                seg = col - a_cs_t[h:h + 1, :]
                lmat = jnp.exp(jnp.where(causal, seg, NEG))
                pair.append((cb * lmat).astype(BF16))
            x_pair = xdt[:, g * 512 + hp * LANES:g * 512 + (hp + 1) * LANES]
            y_a = jnp.dot(pair[0], x_pair, preferred_element_type=F32)
            y_b = jnp.dot(pair[1], x_pair, preferred_element_type=F32)
            ys.append(jnp.where(is_a, y_a, y_b) + y_off[:, hp * LANES:(hp + 1) * LANES])
    y = jnp.concatenate(ys, axis=1) + dsk_ref[...] * xs
    zf = z_ref[...].astype(F32)
    yz = y * (zf * _sigmoid(zf))
    var = jnp.mean(yz * yz, axis=-1, keepdims=True)
    s_ref[...] = (yz * lax.rsqrt(var + EPS) * nw_ref[...]).astype(s_ref.dtype)


def _ssd_prompt(xbc, z, dt, cw, cb, dtb, alog, dsk, nw, emat, tril, batch, seq):
    nc = seq // CHUNK
    row = lambda b, c: (b * nc + c, 0)
    const = lambda b, c: (0, 0)
    return pl.pallas_call(
        _ssd_kernel,
        grid=(batch, nc),
        in_specs=[
            pl.BlockSpec((CHUNK, CONV_DIM), row),
            pl.BlockSpec((CHUNK, D_SSD), row),
            pl.BlockSpec((CHUNK, LANES), row),
            pl.BlockSpec((CONV_W, CONV_DIM), const),
            pl.BlockSpec((1, CONV_DIM), const),
            pl.BlockSpec((1, LANES), const),
            pl.BlockSpec((1, LANES), const),
            pl.BlockSpec((1, D_SSD), const),
            pl.BlockSpec((1, D_SSD), const),
            pl.BlockSpec((LANES, D_SSD), const),
            pl.BlockSpec((CHUNK, CHUNK), const),
        ],
        out_specs=[
            pl.BlockSpec((CHUNK, D_SSD), row),
            pl.BlockSpec((None, D_STATE, D_SSD), lambda b, c: (b, 0, 0)),
        ],
        out_shape=(jax.ShapeDtypeStruct((batch * seq, D_SSD), BF16),
                   jax.ShapeDtypeStruct((batch, D_STATE, D_SSD), F32)),
        scratch_shapes=[pltpu.VMEM((CONV_DIM // LANES, CHUNK + 8, LANES), F32)],
        compiler_params=pltpu.CompilerParams(
            dimension_semantics=("arbitrary", "arbitrary"), vmem_limit_bytes=VMEM_LIMIT),
    )(xbc, z, dt, cw, cb, dtb, alog, dsk, nw, emat, tril)


def _attn_sample_kernel(qt_ref, knt_ref, vnt_ref, gt_ref, k_ref, v_ref, btbl_ref, bias0_ref, o_ref):
    hh = pl.program_id(1)

    @pl.when(hh == 0)
    def _():
        o_ref[...] = jnp.zeros_like(o_ref)

    lane = lax.broadcasted_iota(jnp.int32, (HEAD_DIM, LANES), 1)
    lane1 = lax.broadcasted_iota(jnp.int32, (1, LANES), 1)
    qt = qt_ref[...] * (HEAD_DIM ** -0.5)
    n_pat = float(len(PATTERNS))
    for j in range(SAMPLE_HEADS_PER_STEP):
        h = hh * SAMPLE_HEADS_PER_STEP + j
        pick = lane == h

        def col(val, pick=pick):
            return jnp.sum(jnp.where(pick, val, 0.0), axis=1, keepdims=True)

        qc, knc, vnc, gc = col(qt), col(knt_ref[...]), col(vnt_ref[...]), col(gt_ref[...])
        b0 = jnp.sum(jnp.where(lane1 == h, bias0_ref[...], 0.0), axis=1, keepdims=True)
        s0 = jnp.sum(qc * knc, axis=0, keepdims=True) + b0
        s = jnp.sum(k_ref[j] * qc, axis=0, keepdims=True)
        sp = [s + btbl_ref[pi, pl.ds(h, 1), :] for pi in range(len(PATTERNS))]
        m = s0
        for x in sp:
            m = jnp.maximum(m, jnp.max(x, axis=1, keepdims=True))
        p0 = n_pat * jnp.exp(s0 - m)
        pw = jnp.exp(sp[0] - m)
        for x in sp[1:]:
            pw = pw + jnp.exp(x - m)
        l = jnp.sum(pw, axis=1, keepdims=True) + p0
        oc = (jnp.sum(v_ref[j] * pw, axis=1, keepdims=True) + p0 * vnc) / l
        o_ref[...] = jnp.where(pick, oc * (gc * _sigmoid(gc)), o_ref[...])


def _attn_sample(qt, knt, vnt, gt, cache_k_t, cache_v_t, btbl, bias0):
    b, n_past = cache_k_t.shape[0], cache_k_t.shape[3]
    tok = pl.BlockSpec((None, HEAD_DIM, LANES), lambda i, hh: (i, 0, 0))
    cache = pl.BlockSpec((None, SAMPLE_HEADS_PER_STEP, HEAD_DIM, n_past), lambda i, hh: (i, hh, 0, 0))
    return pl.pallas_call(
        _attn_sample_kernel,
        grid=(b, N_HEADS // SAMPLE_HEADS_PER_STEP),
        in_specs=[tok, tok, tok, tok, cache, cache,
                  pl.BlockSpec((len(PATTERNS), N_HEADS, n_past), lambda i, hh: (0, 0, 0)),
                  pl.BlockSpec((1, LANES), lambda i, hh: (0, 0))],
        out_specs=tok,
        out_shape=jax.ShapeDtypeStruct((b, HEAD_DIM, LANES), F32),
        compiler_params=pltpu.CompilerParams(
            dimension_semantics=("arbitrary", "arbitrary"), vmem_limit_bytes=VMEM_LIMIT),
    )(qt, knt, vnt, gt, cache_k_t, cache_v_t, btbl, bias0)


def _sample_bias_tables(rel_bias, n_past):
    dist = n_past - jnp.arange(n_past)
    bias = _bias_lookup(rel_bias, dist)
    tbls = [jnp.where(((dist % d == 0) & (dist <= w))[None], bias, NEG) for w, d in PATTERNS]
    bias0 = _bias_lookup(rel_bias, jnp.zeros((1,), jnp.int32))
    return jnp.stack(tbls, axis=0), jnp.pad(bias0.reshape(1, N_HEADS), ((0, 0), (0, LANES - N_HEADS)))


def _ssd_sample_kernel(xbc_ref, z_ref, dt_ref, sc_ref, h_ref, cw_ref, cb_ref, dtb_ref, alog_ref, dsk_ref,
                       nw_ref, e_ref, s_ref, conv_out_ref, h_out_ref):
    xnew = xbc_ref[...]
    sc = sc_ref[...]
    acc = cb_ref[...] + xnew * cw_ref[CONV_W - 1:CONV_W, :]
    for i in range(CONV_W - 1):
        acc = acc + sc[i:i + 1, :] * cw_ref[i:i + 1, :]
    xc = acc * _sigmoid(acc)
    conv_out_ref[0:CONV_W - 2, :] = sc[1:CONV_W - 1, :]
    conv_out_ref[CONV_W - 2:CONV_W - 1, :] = xnew

    xs = xc[:, :D_SSD]
    lane1 = lax.broadcasted_iota(jnp.int32, (1, LANES), 1)
    dt = jnp.where(lane1 < N_HEADS, _softplus(dt_ref[...] + dtb_ref[...]), 0.0)
    da = jnp.where(lane1 < N_HEADS, jnp.exp(dt * (-jnp.exp(alog_ref[...]))), 0.0)

    def expand(val):
        v8 = jnp.broadcast_to(val, (8, LANES))
        out = jnp.zeros((8, D_SSD), F32)
        for _ in range(3):
            part = v8.astype(BF16)
            out = out + jnp.dot(part, e_ref[...], preferred_element_type=F32)
            v8 = v8 - part.astype(F32)
        return out[0:1, :]

    xdt = xs * expand(dt)
    dax = expand(da)

    lane = lax.broadcasted_iota(jnp.int32, (HEAD_DIM, LANES), 1)
    sub = lax.broadcasted_iota(jnp.int32, (HEAD_DIM, LANES), 0)
    eye2 = (lane % HEAD_DIM) == sub
    is_a = lane < HEAD_DIM

    def to_cols(row):
        mat = jnp.where(eye2, jnp.broadcast_to(row, (HEAD_DIM, LANES)), 0.0)
        col_a = jnp.sum(jnp.where(is_a, mat, 0.0), axis=1, keepdims=True)
        col_b = jnp.sum(jnp.where(is_a, 0.0, mat), axis=1, keepdims=True)
        return col_a, col_b

    y_rows = []
    for hp in range(N_HEADS // 2):
        g = hp // 4
        b_row = xc[:, D_SSD + g * D_STATE:D_SSD + (g + 1) * D_STATE]
        c_row = xc[:, D_SSD + (N_GROUPS + g) * D_STATE:D_SSD + (N_GROUPS + g + 1) * D_STATE]
        cols = slice(hp * LANES, (hp + 1) * LANES)
        x_cols = to_cols(xdt[:, cols])
        d_cols = to_cols(dax[:, cols])
        y_cols = []
        for which in range(2):
            h = hp * 2 + which
            h_new = h_ref[h] * d_cols[which] + x_cols[which] * b_row
            h_out_ref[h] = h_new
            y_cols.append(jnp.sum(h_new * c_row, axis=1, keepdims=True))
        y_mat = jnp.where(eye2, jnp.where(is_a, y_cols[0], y_cols[1]), 0.0)
        y_rows.append(jnp.sum(y_mat, axis=0, keepdims=True))
    y = jnp.concatenate(y_rows, axis=1) + dsk_ref[...] * xs
    zf = z_ref[...].astype(F32)
    yz = y * (zf * _sigmoid(zf))
    var = jnp.mean(yz * yz, axis=-1, keepdims=True)
    s_ref[...] = (yz * lax.rsqrt(var + EPS) * nw_ref[...]).astype(s_ref.dtype)


def _ssd_sample(xbc3, z3, dt3, state_conv, state_ssm, cw, cb, dtb, alog, dsk, nw, emat):
    b = xbc3.shape[0]
    const = lambda i: (0, 0)
    tok = lambda width: pl.BlockSpec((None, 1, width), lambda i: (i, 0, 0))
    conv_spec = pl.BlockSpec((None, CONV_W - 1, CONV_DIM), lambda i: (i, 0, 0))
    ssm_spec = pl.BlockSpec((None, N_HEADS, HEAD_DIM, D_STATE), lambda i: (i, 0, 0, 0))
    return pl.pallas_call(
        _ssd_sample_kernel,
        grid=(b,),
        in_specs=[
            tok(CONV_DIM), tok(D_SSD), tok(LANES), conv_spec, ssm_spec,
            pl.BlockSpec((CONV_W, CONV_DIM), const),
            pl.BlockSpec((1, CONV_DIM), const),
            pl.BlockSpec((1, LANES), const),
            pl.BlockSpec((1, LANES), const),
            pl.BlockSpec((1, D_SSD), const),
            pl.BlockSpec((1, D_SSD), const),
            pl.BlockSpec((LANES, D_SSD), const),
        ],
        out_specs=[tok(D_SSD), conv_spec, ssm_spec],
        out_shape=(jax.ShapeDtypeStruct((b, 1, D_SSD), BF16),
                   jax.ShapeDtypeStruct((b, CONV_W - 1, CONV_DIM), F32),
                   jax.ShapeDtypeStruct((b, N_HEADS, HEAD_DIM, D_STATE), F32)),
        compiler_params=pltpu.CompilerParams(
            dimension_semantics=("arbitrary",), vmem_limit_bytes=VMEM_LIMIT),
    )(xbc3, z3, dt3, state_conv, state_ssm, cw, cb, dtb, alog, dsk, nw, emat)


def kernel(x_prompt, x_sample, cache_win_k, cache_win_v, state_conv, state_ssm, norm_w, w_in, q_norm_w,
           k_norm_w, rel_bias, conv_w, conv_b, dt_bias, a_log, d_skip, ssd_norm_w, w_out):
    assert x_prompt.shape[-1] == D_MODEL and w_in.shape[0] == 1, "single-layer model of width 1024 only"
    batch, seq, _ = x_prompt.shape
    dec_batch, dec_seq, _ = x_sample.shape
    assert dec_seq == 1 and seq % SPAN == 0 and cache_win_k.shape[2] == WINDOW_MAX

    w_pad = jnp.pad(w_in[0], ((0, 0), (0, D_IN_PAD - D_IN_PROJ))).astype(BF16)
    w_out_b = w_out[0].astype(BF16)
    nw = norm_w[0].reshape(1, D_MODEL)
    qnw = jnp.tile(q_norm_w[0], 512 // HEAD_DIM).reshape(1, 512)
    knw = jnp.tile(k_norm_w[0], 512 // HEAD_DIM).reshape(1, 512)
    idx256 = jnp.arange(256) // HEAD_DIM
    gsum = (idx256[:, None] == idx256[None, :]).astype(BF16)
    cw, cb = conv_w[0], conv_b[0].reshape(1, CONV_DIM)
    pad_heads = lambda a: jnp.pad(a.reshape(1, N_HEADS), ((0, 0), (0, LANES - N_HEADS)))
    dtb, alog = pad_heads(dt_bias[0]), pad_heads(a_log[0])
    dsk = jnp.repeat(d_skip[0], HEAD_DIM).reshape(1, D_SSD)
    snw = ssd_norm_w[0].reshape(1, D_SSD)
    erow = jnp.arange(LANES)[:, None]
    emat = ((erow % N_HEADS == (jnp.arange(D_SSD) // HEAD_DIM)[None, :]) & (erow < 2 * N_HEADS)).astype(BF16)
    tril = (jnp.arange(CHUNK)[:, None] >= jnp.arange(CHUNK)[None, :]).astype(BF16)

    xp = x_prompt.reshape(batch * seq, D_MODEL)
    q, k, v, g, z, xbc, dt = _inproj(xp, nw, w_pad, qnw, knw, gsum, tm=512)
    a = _attn_prompt(q, k, v, g, _prompt_bias_table(rel_bias), batch, seq)
    s, h_fin = _ssd_prompt(xbc, z, dt, cw, cb, dtb, alog, dsk, snw, emat, tril, batch, seq)
    y_p = _outproj(xp, a, s, w_out_b, tm=512).reshape(batch, seq, D_MODEL)
    nwin = min(WINDOW_MAX, seq)
    heads = lambda t: t.reshape(batch, seq, D_ATTN)[:, seq - nwin:].reshape(1, batch, nwin, N_HEADS, HEAD_DIM)
    kp, vp = heads(k), heads(v)
    cp = xbc.reshape(batch, seq, CONV_DIM)[None, :, seq - (CONV_W - 1):]
    hp = jnp.swapaxes(h_fin, 1, 2).reshape(batch, N_HEADS, HEAD_DIM, D_STATE)[None]

    xs2 = x_sample.reshape(dec_batch, D_MODEL)
    qs, ks, vs, gs, zs, xbcs, dts = _inproj(xs2, nw, w_pad, qnw, knw, gsum, tm=dec_batch)
    tok_t = lambda t: jnp.pad(jnp.swapaxes(t.astype(F32).reshape(dec_batch, N_HEADS, HEAD_DIM), 1, 2),
                              ((0, 0), (0, 0), (0, LANES - N_HEADS)))
    cache_t = lambda c: jnp.transpose(c[0], (0, 2, 3, 1))
    btbl, bias0 = _sample_bias_tables(rel_bias, cache_win_k.shape[2])
    a_t = _attn_sample(tok_t(qs), tok_t(ks), tok_t(vs), tok_t(gs), cache_t(cache_win_k), cache_t(cache_win_v),
                       btbl, bias0)
    a_s = jnp.swapaxes(a_t[:, :, :N_HEADS], 1, 2)
    s_s, conv_s, h_s = _ssd_sample(xbcs.reshape(dec_batch, 1, CONV_DIM), zs.reshape(dec_batch, 1, D_SSD),
                                   dts.reshape(dec_batch, 1, LANES), state_conv[0], state_ssm[0],
                                   cw, cb, dtb, alog, dsk, snw, emat)
    y_s = _outproj(xs2, a_s.reshape(dec_batch, D_ATTN).astype(BF16), s_s.reshape(dec_batch, D_SSD),
                   w_out_b, tm=dec_batch).reshape(dec_batch, 1, D_MODEL)
    k_s = ks.reshape(1, dec_batch, 1, N_HEADS, HEAD_DIM)
    v_s = vs.reshape(1, dec_batch, 1, N_HEADS, HEAD_DIM)
    return (y_p, y_s, kp, vp, cp, hp, k_s, v_s, conv_s[None], h_s[None])
```

```python
import functools
import math

import jax
import jax.numpy as jnp
from jax import lax
from jax.experimental import pallas as pl
from jax.experimental.pallas import tpu as pltpu

F32 = jnp.float32
BF16 = jnp.bfloat16

D_MODEL = 1024
D_ATTN = 1024
D_SSD = 1024
HEAD_DIM = 64
N_HEADS = 16
PATTERNS = ((128, 1), (512, 4), (2048, 16))
WINDOW_MAX = 2048
BLK = 128
N_BUCKETS = 32
D_STATE = 128
N_GROUPS = 2
CONV_W = 4
CONV_DIM = D_SSD + 2 * N_GROUPS * D_STATE
CHUNK = 128
EPS = 1e-6
D_IN_PROJ = 4 * D_ATTN + D_SSD + CONV_DIM + N_HEADS
LANES = 128
D_IN_PAD = D_IN_PROJ - N_HEADS + LANES
SPAN = BLK * 16
NEG = -1e30
LOG2E = math.log2(math.e)
ATTN_GROUP = 8
SAMPLE_HEADS_PER_STEP = 8
VMEM_LIMIT = 56 * 1024 * 1024


def _sigmoid(x):
    return 1.0 / (1.0 + jnp.exp(-x))


def _softplus(x):
    return jnp.maximum(x, 0.0) + jnp.log(1.0 + jnp.exp(-jnp.abs(x)))


def _t5_bucket(dist):
    max_exact = N_BUCKETS // 2
    d_f = jnp.maximum(dist, 1).astype(F32)
    large = max_exact + (jnp.log(d_f / max_exact) / math.log(WINDOW_MAX / max_exact)
                         * (N_BUCKETS - max_exact)).astype(jnp.int32)
    large = jnp.minimum(large, N_BUCKETS - 1)
    return jnp.where(dist < max_exact, dist, large)


def _inproj_kernel(x_ref, nw_ref, w_ref, qnw_ref, knw_ref, gsum_ref,
                   q_ref, k_ref, v_ref, g_ref, z_ref, xbc_ref, dt_ref):
    x = x_ref[...]
    h = (x * nw_ref[...]).astype(BF16)
    r = lax.rsqrt(jnp.mean(x * x, axis=-1, keepdims=True) + EPS)

    def seg(c0, width):
        return jnp.dot(h, w_ref[:, c0:c0 + width], preferred_element_type=F32) * r

    for out_ref, base, hw_ref in ((q_ref, 0, qnw_ref), (k_ref, D_ATTN, knw_ref)):
        for c in range(2):
            p = seg(base + 512 * c, 512)
            p2 = (p * p).astype(BF16)
            ss = jnp.concatenate([jnp.dot(p2[:, :256], gsum_ref[...], preferred_element_type=F32),
                                  jnp.dot(p2[:, 256:], gsum_ref[...], preferred_element_type=F32)], axis=1)
            out_ref[:, 512 * c:512 * (c + 1)] = p * lax.rsqrt(ss * (1.0 / HEAD_DIM) + EPS) * hw_ref[...]
    for c in range(2):
        v_ref[:, 512 * c:512 * (c + 1)] = seg(2 * D_ATTN + 512 * c, 512)
        g_ref[:, 512 * c:512 * (c + 1)] = seg(3 * D_ATTN + 512 * c, 512).astype(g_ref.dtype)
        z_ref[:, 512 * c:512 * (c + 1)] = seg(4 * D_ATTN + 512 * c, 512).astype(z_ref.dtype)
    for c in range(3):
        xbc_ref[:, 512 * c:512 * (c + 1)] = seg(5 * D_ATTN + 512 * c, 512)
    dt_ref[...] = seg(5 * D_ATTN + CONV_DIM, LANES)


def _inproj(x2d, nw, w_pad, qnw, knw, gsum, tm):
    t = x2d.shape[0]
    row = lambda i: (i, 0)
    const = lambda i: (0, 0)
    outs = (
        jax.ShapeDtypeStruct((t, D_ATTN), F32),
        jax.ShapeDtypeStruct((t, D_ATTN), F32),
        jax.ShapeDtypeStruct((t, D_ATTN), F32),
        jax.ShapeDtypeStruct((t, D_ATTN), BF16),
        jax.ShapeDtypeStruct((t, D_SSD), BF16),
        jax.ShapeDtypeStruct((t, CONV_DIM), F32),
        jax.ShapeDtypeStruct((t, LANES), F32),
    )
    return pl.pallas_call(
        _inproj_kernel,
        grid=(t // tm,),
        in_specs=[
            pl.BlockSpec((tm, D_MODEL), row),
            pl.BlockSpec((1, D_MODEL), const),
            pl.BlockSpec((D_MODEL, D_IN_PAD), const, pipeline_mode=pl.Buffered(1)),
            pl.BlockSpec((1, 512), const),
            pl.BlockSpec((1, 512), const),
            pl.BlockSpec((256, 256), const),
        ],
        out_specs=[
            pl.BlockSpec((tm, D_ATTN), row),
            pl.BlockSpec((tm, D_ATTN), row),
            pl.BlockSpec((tm, D_ATTN), row),
            pl.BlockSpec((tm, D_ATTN), row),
            pl.BlockSpec((tm, D_SSD), row),
            pl.BlockSpec((tm, CONV_DIM), row),
            pl.BlockSpec((tm, LANES), row),
        ],
        out_shape=outs,
        compiler_params=pltpu.CompilerParams(
            dimension_semantics=("arbitrary",), vmem_limit_bytes=VMEM_LIMIT),
    )(x2d, nw, w_pad, qnw, knw, gsum)


def _outproj_kernel(x_ref, a_ref, s_ref, w_ref, y_ref):
    y_ref[...] = (x_ref[...]
                  + jnp.dot(a_ref[...], w_ref[0:D_ATTN, :], preferred_element_type=F32)
                  + jnp.dot(s_ref[...], w_ref[D_ATTN:, :], preferred_element_type=F32))


def _outproj(x2d, a, s, w_out_b, tm):
    t = x2d.shape[0]
    row = lambda i: (i, 0)
    return pl.pallas_call(
        _outproj_kernel,
        grid=(t // tm,),
        in_specs=[
            pl.BlockSpec((tm, D_MODEL), row),
            pl.BlockSpec((tm, D_ATTN), row),
            pl.BlockSpec((tm, D_SSD), row),
            pl.BlockSpec((D_ATTN + D_SSD, D_MODEL), lambda i: (0, 0), pipeline_mode=pl.Buffered(1)),
        ],
        out_specs=pl.BlockSpec((tm, D_MODEL), row),
        out_shape=jax.ShapeDtypeStruct((t, D_MODEL), F32),
        compiler_params=pltpu.CompilerParams(
            dimension_semantics=("arbitrary",), vmem_limit_bytes=VMEM_LIMIT),
    )(x2d, a, s, w_out_b)


def _attn_kernel(q_ref, k_ref, v_ref, g_ref, bias_ref, o_ref, qp, kp, vp, mid, m_s, l_s, acc_s, s_buf, m_buf):
    s_idx = pl.program_id(2)
    first = (s_idx == 0).astype(jnp.int32)
    slot = s_idx % 2
    pslot = 1 - slot
    lane = lax.broadcasted_iota(jnp.int32, (BLK, LANES), 1)
    is_a = lane < HEAD_DIM
    qscale = HEAD_DIM ** -0.5 * LOG2E
    ones = jnp.ones((2 * BLK, LANES), BF16)
    quarter = SPAN // 4

    @pl.when(s_idx == 0)
    def _():
        kp[pslot] = jnp.zeros((SPAN, LANES), F32)
        vp[pslot] = jnp.zeros((SPAN, LANES), F32)

    def regroup(src_ref, store, scale=None):
        for lo in range(4):
            mid[lo] = src_ref[pl.ds(lo, quarter, stride=4), :]
        for lo in range(4):
            for hi in range(4):
                val = mid[lo, pl.ds(hi, BLK, stride=4), :]
                store(4 * hi + lo, val if scale is None else val * scale)

    def store_q(r, val):
        qp[pl.ds(r * BLK, BLK), :] = val

    def store_k(r, val):
        kp[slot, pl.ds(r * BLK, BLK), :] = val

    def store_v(r, val):
        vp[slot, pl.ds(r * BLK, BLK), :] = val

    regroup(q_ref, store_q, qscale)
    regroup(k_ref, store_k)
    regroup(v_ref, store_v)

    def chunks(d, blk):
        n_chunk = 16 // d
        length = BLK // n_chunk
        u, r = blk // d, blk % d
        is_u0 = u == 0
        cur = [pl.multiple_of((d * c + r) * BLK + u * length, 8) for c in range(n_chunk)]
        back = jnp.where(is_u0, BLK - length, (u - 1) * length)
        prev = [pl.multiple_of((d * c + r) * BLK + back, 8) for c in range(n_chunk)]
        return is_u0, length, cur, jnp.where(is_u0, pslot, slot), prev

    def gather(read, starts, length):
        return jnp.concatenate([read(pl.ds(st, length)) for st in starts], axis=0)

    def scatter(write, starts, length, val):
        for c, st in enumerate(starts):
            write(pl.ds(st, length), val[c * length:(c + 1) * length])

    def qk_stage(pi, d, grp, sl):
        for j in range(ATTN_GROUP):
            is_u0, length, cur, prev_slot, prev = chunks(d, grp * ATTN_GROUP + j)
            qsc = gather(lambda rows: qp[rows, :], cur, length)
            qq = jnp.concatenate([jnp.where(is_a, qsc, 0.0), jnp.where(is_a, 0.0, qsc)], axis=0).astype(BF16)
            kk = jnp.concatenate([gather(lambda rows: kp[prev_slot, rows, :], prev, length),
                                  gather(lambda rows: kp[slot, rows, :], cur, length)], axis=0).astype(BF16)
            bias = bias_ref[jnp.where(is_u0, first, 0), pi]
            s = lax.dot_general(qq, kk, (((1,), (1,)), ((), ())), preferred_element_type=F32) + bias
            s_buf[sl, j] = s
            m_buf[sl, j] = jnp.broadcast_to(jnp.max(s, axis=1, keepdims=True), (2 * BLK, LANES))

    def pv_stage(pi, d, grp, sl):
        for j in range(ATTN_GROUP):
            _, length, cur, prev_slot, prev = chunks(d, grp * ATTN_GROUP + j)
            m = m_buf[sl, j]
            p = jnp.exp2(s_buf[sl, j] - jnp.concatenate([m, m], axis=1)).astype(BF16)
            vv = jnp.concatenate([gather(lambda rows: vp[prev_slot, rows, :], prev, length),
                                  gather(lambda rows: vp[slot, rows, :], cur, length)], axis=0).astype(BF16)
            ol = jnp.dot(p, jnp.concatenate([vv, ones], axis=1), preferred_element_type=F32)
            for ref, val in ((acc_s, jnp.where(is_a, ol[:BLK, :LANES], ol[BLK:, :LANES])),
                             (l_s, jnp.where(is_a, ol[:BLK, LANES:], ol[BLK:, LANES:])),
                             (m_s, jnp.where(is_a, m[:BLK], m[BLK:]))):
                scatter(lambda rows, v, ref=ref: ref.__setitem__((pi, rows, slice(None)), v), cur, length, val)

    n_grp = SPAN // BLK // ATTN_GROUP
    zero = jnp.int32(0)
    qk_stage(0, PATTERNS[0][1], zero, 0)
    for pi, (_, d) in enumerate(PATTERNS):
        def body(g, carry, pi=pi, d=d):
            pv_stage(pi, d, g, g % 2)
            qk_stage(pi, d, g + 1, (g + 1) % 2)
            return carry
        lax.fori_loop(0, (n_grp - 1) + jnp.minimum(pl.program_id(0), 0), body, 0)
        pv_stage(pi, d, zero + (n_grp - 1), (n_grp - 1) % 2)
        if pi + 1 < len(PATTERNS):
            qk_stage(pi + 1, PATTERNS[pi + 1][1], zero, n_grp % 2)

    m = jnp.maximum(jnp.maximum(m_s[0], m_s[1]), m_s[2])
    l = jnp.zeros((SPAN, LANES), F32)
    acc = jnp.zeros((SPAN, LANES), F32)
    for pi in range(len(PATTERNS)):
        e = jnp.exp2(m_s[pi] - m)
        l = l + l_s[pi] * e
        acc = acc + acc_s[pi] * e
    o = acc / l
    for lo in range(4):
        for hi in range(4):
            r = 4 * hi + lo
            mid[lo, pl.ds(hi, BLK, stride=4), :] = o[r * BLK:(r + 1) * BLK]
    for lo in range(4):
        qp[pl.ds(lo, quarter, stride=4), :] = mid[lo]
    g = g_ref[...].astype(F32)
    o_ref[...] = (qp[...] * (g * _sigmoid(g))).astype(o_ref.dtype)


def _attn_prompt(q, k, v, g, bias_tbl, batch, seq):
    n_span = seq // SPAN
    n_hp = N_HEADS // 2
    cur = lambda hp, b, s: (b * n_span + s, hp)
    blk = (SPAN, LANES)
    assert (SPAN // BLK // ATTN_GROUP) % 2 == 0, "score slots alternate by group parity across patterns"
    return pl.pallas_call(
        _attn_kernel,
        grid=(n_hp, batch, n_span),
        in_specs=[
            pl.BlockSpec(blk, cur),
            pl.BlockSpec(blk, cur),
            pl.BlockSpec(blk, cur),
            pl.BlockSpec(blk, cur),
            pl.BlockSpec((None, 2, len(PATTERNS), 2 * BLK, 2 * BLK), lambda hp, b, s: (hp, 0, 0, 0, 0)),
        ],
        out_specs=pl.BlockSpec(blk, cur),
        out_shape=jax.ShapeDtypeStruct((batch * seq, D_ATTN), BF16),
        scratch_shapes=[
            pltpu.VMEM(blk, F32),
            pltpu.VMEM((2,) + blk, F32),
            pltpu.VMEM((2,) + blk, F32),
            pltpu.VMEM((4, SPAN // 4, LANES), F32),
            pltpu.VMEM((len(PATTERNS),) + blk, F32),
            pltpu.VMEM((len(PATTERNS),) + blk, F32),
            pltpu.VMEM((len(PATTERNS),) + blk, F32),
            pltpu.VMEM((2, ATTN_GROUP, 2 * BLK, 2 * BLK), F32),
            pltpu.VMEM((2, ATTN_GROUP, 2 * BLK, LANES), F32),
        ],
        compiler_params=pltpu.CompilerParams(
            dimension_semantics=("arbitrary", "arbitrary", "arbitrary"), vmem_limit_bytes=VMEM_LIMIT),
    )(q, k, v, g, bias_tbl)


def _bias_lookup(rel_bias, dist):
    onehot = (_t5_bucket(dist)[..., None] == jnp.arange(N_BUCKETS)).astype(F32)
    return jnp.einsum('...b,bh->h...', onehot, rel_bias.astype(F32), precision=lax.Precision.HIGHEST)


def _prompt_bias_table(rel_bias):
    tbls = []
    for w, d in PATTERNS:
        n_chunk = 16 // d
        n = jnp.arange(BLK)
        idx = (n % (BLK // n_chunk)) * n_chunk + n // (BLK // n_chunk)
        i = idx[:, None]
        j = jnp.concatenate([idx, BLK + idx])[None, :]
        rel = i + BLK - j
        band = (rel >= 0) & (rel <= w // d)
        bias = _bias_lookup(rel_bias, jnp.maximum(rel, 0) * d) * LOG2E
        normal = jnp.where(band[None], bias, NEG)
        first = jnp.where((band & (j >= BLK))[None], bias, NEG)
        tbls.append(jnp.stack([normal, first], axis=0))
    t = jnp.stack(tbls, axis=1)
    t = t.reshape(2, len(PATTERNS), N_HEADS // 2, 2 * BLK, 2 * BLK)
    return jnp.moveaxis(t, 2, 0)


def _ssd_kernel(xbc_ref, z_ref, dt_ref, cw_ref, cb_ref, dtb_ref, alog_ref, dsk_ref, nw_ref,
                e_ref, tril_ref, s_ref, h_ref, cbuf):
    c = pl.program_id(1)

    n_slab = CONV_DIM // LANES

    @pl.when(c == 0)
    def _():
        cbuf[:, 0:8, :] = jnp.zeros((n_slab, 8, LANES), F32)
        h_ref[...] = jnp.zeros_like(h_ref)

    xc_slabs = []
    for j in range(n_slab):
        cols = slice(j * LANES, (j + 1) * LANES)
        xj = xbc_ref[:, cols]
        cbuf[j, 8:8 + CHUNK, :] = xj
        acc = cb_ref[:, cols] + xj * cw_ref[CONV_W - 1:CONV_W, cols]
        for i in range(CONV_W - 1):
            tap = cbuf[pl.ds(j, 1, stride=2), pl.ds(8 - (CONV_W - 1) + i, CHUNK), :][0]
            acc = acc + tap * cw_ref[i:i + 1, cols]
        xc_slabs.append(acc * _sigmoid(acc))
        cbuf[j, 0:8, :] = xj[CHUNK - 8:, :]

    n_x = D_SSD // LANES
    xs = jnp.concatenate(xc_slabs[:n_x], axis=1)
    lane = lax.broadcasted_iota(jnp.int32, (CHUNK, LANES), 1)
    sub = lax.broadcasted_iota(jnp.int32, (CHUNK, LANES), 0)
    head_lane = lane < N_HEADS
    dt = jnp.where(head_lane, _softplus(dt_ref[...] + dtb_ref[...]), 0.0)
    la = dt * (-jnp.exp(alog_ref[...]))

    def hi_lo(val):
        hi = val.astype(BF16).astype(F32)
        return (hi + pltpu.roll(val - hi, N_HEADS, axis=1)).astype(BF16)

    cs2 = jnp.dot(tril_ref[...], hi_lo(la), preferred_element_type=F32)
    a_cs = jnp.where(head_lane, cs2 + pltpu.roll(cs2, LANES - N_HEADS, axis=1), 0.0)
    ea = jnp.where(head_lane, jnp.exp(a_cs), 0.0)
    dte = jnp.where(head_lane, jnp.exp(a_cs[CHUNK - 1:CHUNK, :] - a_cs), 0.0)

    expanded = jnp.dot(jnp.concatenate([hi_lo(dt), hi_lo(ea), hi_lo(dte)], axis=0), e_ref[...],
                       preferred_element_type=F32)
    dtx, eax, dtex = expanded[:CHUNK], expanded[CHUNK:2 * CHUNK], expanded[2 * CHUNK:]
    xdt_f = xs * dtx
    xdt = xdt_f.astype(BF16)
    xdte = (xdt_f * dtex).astype(BF16)

    a_cs_t = a_cs.T
    causal = sub >= lane
    is_a = lane < HEAD_DIM
    ys = []
    for g in range(N_GROUPS):
        b_g = xc_slabs[n_x + g]
        c_g = xc_slabs[n_x + N_GROUPS + g]
        b_bf = b_g.astype(BF16)
        c_bf = c_g.astype(BF16)
        cb = lax.dot_general(c_bf, b_bf, (((1,), (1,)), ((), ())), preferred_element_type=F32)
        gcols = slice(g * 512, (g + 1) * 512)
        h_prev = h_ref[:, gcols]
        y_off = jnp.dot(c_bf, h_prev.astype(BF16), preferred_element_type=F32) * eax[:, gcols]
        st = jnp.dot(b_g.T.astype(BF16), xdte[:, gcols], preferred_element_type=F32)
        h_ref[:, gcols] = h_prev * eax[CHUNK - 1:CHUNK, gcols] + st
        for hp in range(4):
            pair = []
            for which in range(2):
                h = g * 8 + hp * 2 + which
                col = jnp.sum(jnp.where(lane == h, a_cs, 0.0), axis=1, keepdims=True)
                seg = col - a_cs_t[h:h + 1, :]
                lmat = jnp.exp(jnp.where(causal, seg, NEG))
                pair.append((cb * lmat).astype(BF16))
            x_pair = xdt[:, g * 512 + hp * LANES:g * 512 + (hp + 1) * LANES]
            y_a = jnp.dot(pair[0], x_pair, preferred_element_type=F32)
            y_b = jnp.dot(pair[1], x_pair, preferred_element_type=F32)
            ys.append(jnp.where(is_a, y_a, y_b) + y_off[:, hp * LANES:(hp + 1) * LANES])
    y = jnp.concatenate(ys, axis=1) + dsk_ref[...] * xs
    zf = z_ref[...].astype(F32)
    yz = y * (zf * _sigmoid(zf))
    var = jnp.mean(yz * yz, axis=-1, keepdims=True)
    s_ref[...] = (yz * lax.rsqrt(var + EPS) * nw_ref[...]).astype(s_ref.dtype)


def _ssd_prompt(xbc, z, dt, cw, cb, dtb, alog, dsk, nw, emat, tril, batch, seq):
    nc = seq // CHUNK
    row = lambda b, c: (b * nc + c, 0)
    const = lambda b, c: (0, 0)
    return pl.pallas_call(
        _ssd_kernel,
        grid=(batch, nc),
        in_specs=[
            pl.BlockSpec((CHUNK, CONV_DIM), row),
            pl.BlockSpec((CHUNK, D_SSD), row),
            pl.BlockSpec((CHUNK, LANES), row),
            pl.BlockSpec((CONV_W, CONV_DIM), const),
            pl.BlockSpec((1, CONV_DIM), const),
            pl.BlockSpec((1, LANES), const),
            pl.BlockSpec((1, LANES), const),
            pl.BlockSpec((1, D_SSD), const),
            pl.BlockSpec((1, D_SSD), const),
            pl.BlockSpec((LANES, D_SSD), const),
            pl.BlockSpec((CHUNK, CHUNK), const),
        ],
        out_specs=[
            pl.BlockSpec((CHUNK, D_SSD), row),
            pl.BlockSpec((None, D_STATE, D_SSD), lambda b, c: (b, 0, 0)),
        ],
        out_shape=(jax.ShapeDtypeStruct((batch * seq, D_SSD), BF16),
                   jax.ShapeDtypeStruct((batch, D_STATE, D_SSD), F32)),
        scratch_shapes=[pltpu.VMEM((CONV_DIM // LANES, CHUNK + 8, LANES), F32)],
        compiler_params=pltpu.CompilerParams(
            dimension_semantics=("arbitrary", "arbitrary"), vmem_limit_bytes=VMEM_LIMIT),
    )(xbc, z, dt, cw, cb, dtb, alog, dsk, nw, emat, tril)


def _attn_sample_kernel(qt_ref, knt_ref, vnt_ref, gt_ref, k_ref, v_ref, btbl_ref, bias0_ref, o_ref):
    hh = pl.program_id(1)

    @pl.when(hh == 0)
    def _():
        o_ref[...] = jnp.zeros_like(o_ref)

    lane = lax.broadcasted_iota(jnp.int32, (HEAD_DIM, LANES), 1)
    lane1 = lax.broadcasted_iota(jnp.int32, (1, LANES), 1)
    qt = qt_ref[...] * (HEAD_DIM ** -0.5)
    n_pat = float(len(PATTERNS))
    for j in range(SAMPLE_HEADS_PER_STEP):
        h = hh * SAMPLE_HEADS_PER_STEP + j
        pick = lane == h

        def col(val, pick=pick):
            return jnp.sum(jnp.where(pick, val, 0.0), axis=1, keepdims=True)

        qc, knc, vnc, gc = col(qt), col(knt_ref[...]), col(vnt_ref[...]), col(gt_ref[...])
        b0 = jnp.sum(jnp.where(lane1 == h, bias0_ref[...], 0.0), axis=1, keepdims=True)
        s0 = jnp.sum(qc * knc, axis=0, keepdims=True) + b0
        s = jnp.sum(k_ref[j] * qc, axis=0, keepdims=True)
        sp = [s + btbl_ref[pi, pl.ds(h, 1), :] for pi in range(len(PATTERNS))]
        m = s0
        for x in sp:
            m = jnp.maximum(m, jnp.max(x, axis=1, keepdims=True))
        p0 = n_pat * jnp.exp(s0 - m)
        pw = jnp.exp(sp[0] - m)
        for x in sp[1:]:
            pw = pw + jnp.exp(x - m)
        l = jnp.sum(pw, axis=1, keepdims=True) + p0
        oc = (jnp.sum(v_ref[j] * pw, axis=1, keepdims=True) + p0 * vnc) / l
        o_ref[...] = jnp.where(pick, oc * (gc * _sigmoid(gc)), o_ref[...])


def _attn_sample(qt, knt, vnt, gt, cache_k_t, cache_v_t, btbl, bias0):
    b, n_past = cache_k_t.shape[0], cache_k_t.shape[3]
    tok = pl.BlockSpec((None, HEAD_DIM, LANES), lambda i, hh: (i, 0, 0))
    cache = pl.BlockSpec((None, SAMPLE_HEADS_PER_STEP, HEAD_DIM, n_past), lambda i, hh: (i, hh, 0, 0))
    return pl.pallas_call(
        _attn_sample_kernel,
        grid=(b, N_HEADS // SAMPLE_HEADS_PER_STEP),
        in_specs=[tok, tok, tok, tok, cache, cache,
                  pl.BlockSpec((len(PATTERNS), N_HEADS, n_past), lambda i, hh: (0, 0, 0)),
                  pl.BlockSpec((1, LANES), lambda i, hh: (0, 0))],
        out_specs=tok,
        out_shape=jax.ShapeDtypeStruct((b, HEAD_DIM, LANES), F32),
        compiler_params=pltpu.CompilerParams(
            dimension_semantics=("arbitrary", "arbitrary"), vmem_limit_bytes=VMEM_LIMIT),
    )(qt, knt, vnt, gt, cache_k_t, cache_v_t, btbl, bias0)


def _sample_bias_tables(rel_bias, n_past):
    dist = n_past - jnp.arange(n_past)
    bias = _bias_lookup(rel_bias, dist)
    tbls = [jnp.where(((dist % d == 0) & (dist <= w))[None], bias, NEG) for w, d in PATTERNS]
    bias0 = _bias_lookup(rel_bias, jnp.zeros((1,), jnp.int32))
    return jnp.stack(tbls, axis=0), jnp.pad(bias0.reshape(1, N_HEADS), ((0, 0), (0, LANES - N_HEADS)))


def _ssd_sample_kernel(xbc_ref, z_ref, dt_ref, sc_ref, h_ref, cw_ref, cb_ref, dtb_ref, alog_ref, dsk_ref,
                       nw_ref, e_ref, s_ref, conv_out_ref, h_out_ref):
    xnew = xbc_ref[...]
    sc = sc_ref[...]
    acc = cb_ref[...] + xnew * cw_ref[CONV_W - 1:CONV_W, :]
    for i in range(CONV_W - 1):
        acc = acc + sc[i:i + 1, :] * cw_ref[i:i + 1, :]
    xc = acc * _sigmoid(acc)
    conv_out_ref[0:CONV_W - 2, :] = sc[1:CONV_W - 1, :]
    conv_out_ref[CONV_W - 2:CONV_W - 1, :] = xnew

    xs = xc[:, :D_SSD]
    lane1 = lax.broadcasted_iota(jnp.int32, (1, LANES), 1)
    dt = jnp.where(lane1 < N_HEADS, _softplus(dt_ref[...] + dtb_ref[...]), 0.0)
    da = jnp.where(lane1 < N_HEADS, jnp.exp(dt * (-jnp.exp(alog_ref[...]))), 0.0)

    def expand(val):
        v8 = jnp.broadcast_to(val, (8, LANES))
        out = jnp.zeros((8, D_SSD), F32)
        for _ in range(3):
            part = v8.astype(BF16)
            out = out + jnp.dot(part, e_ref[...], preferred_element_type=F32)
            v8 = v8 - part.astype(F32)
        return out[0:1, :]

    xdt = xs * expand(dt)
    dax = expand(da)

    lane = lax.broadcasted_iota(jnp.int32, (HEAD_DIM, LANES), 1)
    sub = lax.broadcasted_iota(jnp.int32, (HEAD_DIM, LANES), 0)
    eye2 = (lane % HEAD_DIM) == sub
    is_a = lane < HEAD_DIM

    def to_cols(row):
        mat = jnp.where(eye2, jnp.broadcast_to(row, (HEAD_DIM, LANES)), 0.0)
        col_a = jnp.sum(jnp.where(is_a, mat, 0.0), axis=1, keepdims=True)
        col_b = jnp.sum(jnp.where(is_a, 0.0, mat), axis=1, keepdims=True)
        return col_a, col_b

    y_rows = []
    for hp in range(N_HEADS // 2):
        g = hp // 4
        b_row = xc[:, D_SSD + g * D_STATE:D_SSD + (g + 1) * D_STATE]
        c_row = xc[:, D_SSD + (N_GROUPS + g) * D_STATE:D_SSD + (N_GROUPS + g + 1) * D_STATE]
        cols = slice(hp * LANES, (hp + 1) * LANES)
        x_cols = to_cols(xdt[:, cols])
        d_cols = to_cols(dax[:, cols])
        y_cols = []
        for which in range(2):
            h = hp * 2 + which
            h_new = h_ref[h] * d_cols[which] + x_cols[which] * b_row
            h_out_ref[h] = h_new
            y_cols.append(jnp.sum(h_new * c_row, axis=1, keepdims=True))
        y_mat = jnp.where(eye2, jnp.where(is_a, y_cols[0], y_cols[1]), 0.0)
        y_rows.append(jnp.sum(y_mat, axis=0, keepdims=True))
    y = jnp.concatenate(y_rows, axis=1) + dsk_ref[...] * xs
    zf = z_ref[...].astype(F32)
    yz = y * (zf * _sigmoid(zf))
    var = jnp.mean(yz * yz, axis=-1, keepdims=True)
    s_ref[...] = (yz * lax.rsqrt(var + EPS) * nw_ref[...]).astype(s_ref.dtype)


def _ssd_sample(xbc3, z3, dt3, state_conv, state_ssm, cw, cb, dtb, alog, dsk, nw, emat):
    b = xbc3.shape[0]
    const = lambda i: (0, 0)
    tok = lambda width: pl.BlockSpec((None, 1, width), lambda i: (i, 0, 0))
    conv_spec = pl.BlockSpec((None, CONV_W - 1, CONV_DIM), lambda i: (i, 0, 0))
    ssm_spec = pl.BlockSpec((None, N_HEADS, HEAD_DIM, D_STATE), lambda i: (i, 0, 0, 0))
    return pl.pallas_call(
        _ssd_sample_kernel,
        grid=(b,),
        in_specs=[
            tok(CONV_DIM), tok(D_SSD), tok(LANES), conv_spec, ssm_spec,
            pl.BlockSpec((CONV_W, CONV_DIM), const),
            pl.BlockSpec((1, CONV_DIM), const),
            pl.BlockSpec((1, LANES), const),
            pl.BlockSpec((1, LANES), const),
            pl.BlockSpec((1, D_SSD), const),
            pl.BlockSpec((1, D_SSD), const),
            pl.BlockSpec((LANES, D_SSD), const),
        ],
        out_specs=[tok(D_SSD), conv_spec, ssm_spec],
        out_shape=(jax.ShapeDtypeStruct((b, 1, D_SSD), BF16),
                   jax.ShapeDtypeStruct((b, CONV_W - 1, CONV_DIM), F32),
                   jax.ShapeDtypeStruct((b, N_HEADS, HEAD_DIM, D_STATE), F32)),
        compiler_params=pltpu.CompilerParams(
            dimension_semantics=("arbitrary",), vmem_limit_bytes=VMEM_LIMIT),
    )(xbc3, z3, dt3, state_conv, state_ssm, cw, cb, dtb, alog, dsk, nw, emat)


def kernel(x_prompt, x_sample, cache_win_k, cache_win_v, state_conv, state_ssm, norm_w, w_in, q_norm_w,
           k_norm_w, rel_bias, conv_w, conv_b, dt_bias, a_log, d_skip, ssd_norm_w, w_out):
    assert x_prompt.shape[-1] == D_MODEL and w_in.shape[0] == 1, "single-layer model of width 1024 only"
    batch, seq, _ = x_prompt.shape
    dec_batch, dec_seq, _ = x_sample.shape
    assert dec_seq == 1 and seq % SPAN == 0 and cache_win_k.shape[2] == WINDOW_MAX

    w_pad = jnp.pad(w_in[0], ((0, 0), (0, D_IN_PAD - D_IN_PROJ))).astype(BF16)
    w_out_b = w_out[0].astype(BF16)
    nw = norm_w[0].reshape(1, D_MODEL)
    qnw = jnp.tile(q_norm_w[0], 512 // HEAD_DIM).reshape(1, 512)
    knw = jnp.tile(k_norm_w[0], 512 // HEAD_DIM).reshape(1, 512)
    idx256 = jnp.arange(256) // HEAD_DIM
    gsum = (idx256[:, None] == idx256[None, :]).astype(BF16)
    cw, cb = conv_w[0], conv_b[0].reshape(1, CONV_DIM)
    pad_heads = lambda a: jnp.pad(a.reshape(1, N_HEADS), ((0, 0), (0, LANES - N_HEADS)))
    dtb, alog = pad_heads(dt_bias[0]), pad_heads(a_log[0])
    dsk = jnp.repeat(d_skip[0], HEAD_DIM).reshape(1, D_SSD)
    snw = ssd_norm_w[0].reshape(1, D_SSD)
    erow = jnp.arange(LANES)[:, None]
    emat = ((erow % N_HEADS == (jnp.arange(D_SSD) // HEAD_DIM)[None, :]) & (erow < 2 * N_HEADS)).astype(BF16)
    tril = (jnp.arange(CHUNK)[:, None] >= jnp.arange(CHUNK)[None, :]).astype(BF16)

    xp = x_prompt.reshape(batch * seq, D_MODEL)
    q, k, v, g, z, xbc, dt = _inproj(xp, nw, w_pad, qnw, knw, gsum, tm=512)
    a = _attn_prompt(q, k, v, g, _prompt_bias_table(rel_bias), batch, seq)
    s, h_fin = _ssd_prompt(xbc, z, dt, cw, cb, dtb, alog, dsk, snw, emat, tril, batch, seq)
    y_p = _outproj(xp, a, s, w_out_b, tm=512).reshape(batch, seq, D_MODEL)
    nwin = min(WINDOW_MAX, seq)
    heads = lambda t: t.reshape(batch, seq, D_ATTN)[:, seq - nwin:].reshape(1, batch, nwin, N_HEADS, HEAD_DIM)
    kp, vp = heads(k), heads(v)
    cp = xbc.reshape(batch, seq, CONV_DIM)[None, :, seq - (CONV_W - 1):]
    hp = jnp.swapaxes(h_fin, 1, 2).reshape(batch, N_HEADS, HEAD_DIM, D_STATE)[None]

    xs2 = x_sample.reshape(dec_batch, D_MODEL)
    qs, ks, vs, gs, zs, xbcs, dts = _inproj(xs2, nw, w_pad, qnw, knw, gsum, tm=dec_batch)
    tok_t = lambda t: jnp.pad(jnp.swapaxes(t.astype(F32).reshape(dec_batch, N_HEADS, HEAD_DIM), 1, 2),
                              ((0, 0), (0, 0), (0, LANES - N_HEADS)))
    cache_t = lambda c: jnp.transpose(c[0], (0, 2, 3, 1))
    btbl, bias0 = _sample_bias_tables(rel_bias, cache_win_k.shape[2])
    a_t = _attn_sample(tok_t(qs), tok_t(ks), tok_t(vs), tok_t(gs), cache_t(cache_win_k), cache_t(cache_win_v),
                       btbl, bias0)
    a_s = jnp.swapaxes(a_t[:, :, :N_HEADS], 1, 2)
    s_s, conv_s, h_s = _ssd_sample(xbcs.reshape(dec_batch, 1, CONV_DIM), zs.reshape(dec_batch, 1, D_SSD),
                                   dts.reshape(dec_batch, 1, LANES), state_conv[0], state_ssm[0],
                                   cw, cb, dtb, alog, dsk, snw, emat)
    y_s = _outproj(xs2, a_s.reshape(dec_batch, D_ATTN).astype(BF16), s_s.reshape(dec_batch, D_SSD),
                   w_out_b, tm=dec_batch).reshape(dec_batch, 1, D_MODEL)
    k_s = ks.reshape(1, dec_batch, 1, N_HEADS, HEAD_DIM)
    v_s = vs.reshape(1, dec_batch, 1, N_HEADS, HEAD_DIM)
    return (y_p, y_s, kp, vp, cp, hp, k_s, v_s, conv_s[None], h_s[None])
```

```python
import functools
import math

import jax
import jax.numpy as jnp
from jax import lax
from jax.experimental import pallas as pl
from jax.experimental.pallas import tpu as pltpu

F32 = jnp.float32
BF16 = jnp.bfloat16

D_MODEL = 1024
D_ATTN = 1024
D_SSD = 1024
HEAD_DIM = 64
N_HEADS = 16
PATTERNS = ((128, 1), (512, 4), (2048, 16))
WINDOW_MAX = 2048
BLK = 128
N_BUCKETS = 32
D_STATE = 128
N_GROUPS = 2
CONV_W = 4
CONV_DIM = D_SSD + 2 * N_GROUPS * D_STATE
CHUNK = 128
EPS = 1e-6
D_IN_PROJ = 4 * D_ATTN + D_SSD + CONV_DIM + N_HEADS
LANES = 128
D_IN_PAD = D_IN_PROJ - N_HEADS + LANES
SPAN = BLK * 16
NEG = -1e30
LOG2E = math.log2(math.e)
ATTN_GROUP = 16
SAMPLE_HEADS_PER_STEP = 8
VMEM_LIMIT = 56 * 1024 * 1024


def _sigmoid(x):
    return 1.0 / (1.0 + jnp.exp(-x))


def _softplus(x):
    return jnp.maximum(x, 0.0) + jnp.log(1.0 + jnp.exp(-jnp.abs(x)))


def _t5_bucket(dist):
    max_exact = N_BUCKETS // 2
    d_f = jnp.maximum(dist, 1).astype(F32)
    large = max_exact + (jnp.log(d_f / max_exact) / math.log(WINDOW_MAX / max_exact)
                         * (N_BUCKETS - max_exact)).astype(jnp.int32)
    large = jnp.minimum(large, N_BUCKETS - 1)
    return jnp.where(dist < max_exact, dist, large)


def _inproj_kernel(x_ref, nw_ref, w_ref, qnw_ref, knw_ref, gsum_ref,
                   q_ref, k_ref, v_ref, g_ref, z_ref, xbc_ref, dt_ref):
    x = x_ref[...]
    h = (x * nw_ref[...]).astype(BF16)
    r = lax.rsqrt(jnp.mean(x * x, axis=-1, keepdims=True) + EPS)

    def seg(c0, width):
        return jnp.dot(h, w_ref[:, c0:c0 + width], preferred_element_type=F32) * r

    for out_ref, base, hw_ref in ((q_ref, 0, qnw_ref), (k_ref, D_ATTN, knw_ref)):
        for c in range(2):
            p = seg(base + 512 * c, 512)
            p2 = (p * p).astype(BF16)
            ss = jnp.concatenate([jnp.dot(p2[:, :256], gsum_ref[...], preferred_element_type=F32),
                                  jnp.dot(p2[:, 256:], gsum_ref[...], preferred_element_type=F32)], axis=1)
            out_ref[:, 512 * c:512 * (c + 1)] = p * lax.rsqrt(ss * (1.0 / HEAD_DIM) + EPS) * hw_ref[...]
    for c in range(2):
        v_ref[:, 512 * c:512 * (c + 1)] = seg(2 * D_ATTN + 512 * c, 512)
        g_ref[:, 512 * c:512 * (c + 1)] = seg(3 * D_ATTN + 512 * c, 512).astype(g_ref.dtype)
        z_ref[:, 512 * c:512 * (c + 1)] = seg(4 * D_ATTN + 512 * c, 512).astype(z_ref.dtype)
    for c in range(3):
        xbc_ref[:, 512 * c:512 * (c + 1)] = seg(5 * D_ATTN + 512 * c, 512)
    dt_ref[...] = seg(5 * D_ATTN + CONV_DIM, LANES)


def _inproj(x2d, nw, w_pad, qnw, knw, gsum, tm):
    t = x2d.shape[0]
    row = lambda i: (i, 0)
    const = lambda i: (0, 0)
    outs = (
        jax.ShapeDtypeStruct((t, D_ATTN), F32),
        jax.ShapeDtypeStruct((t, D_ATTN), F32),
        jax.ShapeDtypeStruct((t, D_ATTN), F32),
        jax.ShapeDtypeStruct((t, D_ATTN), BF16),
        jax.ShapeDtypeStruct((t, D_SSD), BF16),
        jax.ShapeDtypeStruct((t, CONV_DIM), F32),
        jax.ShapeDtypeStruct((t, LANES), F32),
    )
    return pl.pallas_call(
        _inproj_kernel,
        grid=(t // tm,),
        in_specs=[
            pl.BlockSpec((tm, D_MODEL), row),
            pl.BlockSpec((1, D_MODEL), const),
            pl.BlockSpec((D_MODEL, D_IN_PAD), const, pipeline_mode=pl.Buffered(1)),
            pl.BlockSpec((1, 512), const),
            pl.BlockSpec((1, 512), const),
            pl.BlockSpec((256, 256), const),
        ],
        out_specs=[
            pl.BlockSpec((tm, D_ATTN), row),
            pl.BlockSpec((tm, D_ATTN), row),
            pl.BlockSpec((tm, D_ATTN), row),
            pl.BlockSpec((tm, D_ATTN), row),
            pl.BlockSpec((tm, D_SSD), row),
            pl.BlockSpec((tm, CONV_DIM), row),
            pl.BlockSpec((tm, LANES), row),
        ],
        out_shape=outs,
        compiler_params=pltpu.CompilerParams(
            dimension_semantics=("arbitrary",), vmem_limit_bytes=VMEM_LIMIT),
    )(x2d, nw, w_pad, qnw, knw, gsum)


def _outproj_kernel(x_ref, a_ref, s_ref, w_ref, y_ref):
    y_ref[...] = (x_ref[...]
                  + jnp.dot(a_ref[...], w_ref[0:D_ATTN, :], preferred_element_type=F32)
                  + jnp.dot(s_ref[...], w_ref[D_ATTN:, :], preferred_element_type=F32))


def _outproj(x2d, a, s, w_out_b, tm):
    t = x2d.shape[0]
    row = lambda i: (i, 0)
    return pl.pallas_call(
        _outproj_kernel,
        grid=(t // tm,),
        in_specs=[
            pl.BlockSpec((tm, D_MODEL), row),
            pl.BlockSpec((tm, D_ATTN), row),
            pl.BlockSpec((tm, D_SSD), row),
            pl.BlockSpec((D_ATTN + D_SSD, D_MODEL), lambda i: (0, 0), pipeline_mode=pl.Buffered(1)),
        ],
        out_specs=pl.BlockSpec((tm, D_MODEL), row),
        out_shape=jax.ShapeDtypeStruct((t, D_MODEL), F32),
        compiler_params=pltpu.CompilerParams(
            dimension_semantics=("arbitrary",), vmem_limit_bytes=VMEM_LIMIT),
    )(x2d, a, s, w_out_b)


def _attn_kernel(q_ref, k_ref, v_ref, g_ref, bias_ref, o_ref, qp, kp, vp, mid, m_s, l_s, acc_s, s_buf, m_buf):
    s_idx = pl.program_id(2)
    first = (s_idx == 0).astype(jnp.int32)
    slot = s_idx % 2
    pslot = 1 - slot
    lane = lax.broadcasted_iota(jnp.int32, (BLK, LANES), 1)
    is_a = lane < HEAD_DIM
    qscale = HEAD_DIM ** -0.5 * LOG2E
    ones = jnp.ones((2 * BLK, LANES), BF16)
    quarter = SPAN // 4

    @pl.when(s_idx == 0)
    def _():
        kp[pslot] = jnp.zeros((SPAN, LANES), F32)
        vp[pslot] = jnp.zeros((SPAN, LANES), F32)

    def regroup(src_ref, store, scale=None):
        for lo in range(4):
            mid[lo] = src_ref[pl.ds(lo, quarter, stride=4), :]
        for lo in range(4):
            for hi in range(4):
                val = mid[lo, pl.ds(hi, BLK, stride=4), :]
                store(4 * hi + lo, val if scale is None else val * scale)

    def store_q(r, val):
        qp[pl.ds(r * BLK, BLK), :] = val

    def store_k(r, val):
        kp[slot, pl.ds(r * BLK, BLK), :] = val

    def store_v(r, val):
        vp[slot, pl.ds(r * BLK, BLK), :] = val

    regroup(q_ref, store_q, qscale)
    regroup(k_ref, store_k)
    regroup(v_ref, store_v)

    def chunks(d, blk):
        n_chunk = 16 // d
        length = BLK // n_chunk
        u, r = blk // d, blk % d
        is_u0 = u == 0
        cur = [pl.multiple_of((d * c + r) * BLK + u * length, 8) for c in range(n_chunk)]
        back = jnp.where(is_u0, BLK - length, (u - 1) * length)
        prev = [pl.multiple_of((d * c + r) * BLK + back, 8) for c in range(n_chunk)]
        return is_u0, length, cur, jnp.where(is_u0, pslot, slot), prev

    def gather(read, starts, length):
        return jnp.concatenate([read(pl.ds(st, length)) for st in starts], axis=0)

    def scatter(write, starts, length, val):
        for c, st in enumerate(starts):
            write(pl.ds(st, length), val[c * length:(c + 1) * length])

    def qk_stage(pi, d, grp, sl):
        for j in range(ATTN_GROUP):
            is_u0, length, cur, prev_slot, prev = chunks(d, grp * ATTN_GROUP + j)
            qsc = gather(lambda rows: qp[rows, :], cur, length)
            qq = jnp.concatenate([jnp.where(is_a, qsc, 0.0), jnp.where(is_a, 0.0, qsc)], axis=0).astype(BF16)
            kk = jnp.concatenate([gather(lambda rows: kp[prev_slot, rows, :], prev, length),
                                  gather(lambda rows: kp[slot, rows, :], cur, length)], axis=0).astype(BF16)
            bias = bias_ref[jnp.where(is_u0, first, 0), pi]
            s = lax.dot_general(qq, kk, (((1,), (1,)), ((), ())), preferred_element_type=F32) + bias
            s_buf[sl, j] = s
            m_buf[sl, j] = jnp.broadcast_to(jnp.max(s, axis=1, keepdims=True), (2 * BLK, LANES))

    def pv_stage(pi, d, grp, sl):
        for j in range(ATTN_GROUP):
            _, length, cur, prev_slot, prev = chunks(d, grp * ATTN_GROUP + j)
            m = m_buf[sl, j]
            p = jnp.exp2(s_buf[sl, j] - jnp.concatenate([m, m], axis=1)).astype(BF16)
            vv = jnp.concatenate([gather(lambda rows: vp[prev_slot, rows, :], prev, length),
                                  gather(lambda rows: vp[slot, rows, :], cur, length)], axis=0).astype(BF16)
            ol = jnp.dot(p, jnp.concatenate([vv, ones], axis=1), preferred_element_type=F32)
            for ref, val in ((acc_s, jnp.where(is_a, ol[:BLK, :LANES], ol[BLK:, :LANES])),
                             (l_s, jnp.where(is_a, ol[:BLK, LANES:], ol[BLK:, LANES:])),
                             (m_s, jnp.where(is_a, m[:BLK], m[BLK:]))):
                scatter(lambda rows, v, ref=ref: ref.__setitem__((pi, rows, slice(None)), v), cur, length, val)

    groups = [(pi, d, jnp.int32(grp)) for pi, (_, d) in enumerate(PATTERNS)
              for grp in range(SPAN // BLK // ATTN_GROUP)]

    def region(fn):
        def body(i, carry):
            fn()
            return carry
        lax.fori_loop(0, 1 + jnp.minimum(pl.program_id(0), 0), body, 0)

    region(lambda: qk_stage(*groups[0], 0))
    for i, grp in enumerate(groups):
        def step(i=i, grp=grp):
            pv_stage(*grp, i % 2)
            if i + 1 < len(groups):
                qk_stage(*groups[i + 1], (i + 1) % 2)
        region(step)

    m = jnp.maximum(jnp.maximum(m_s[0], m_s[1]), m_s[2])
    l = jnp.zeros((SPAN, LANES), F32)
    acc = jnp.zeros((SPAN, LANES), F32)
    for pi in range(len(PATTERNS)):
        e = jnp.exp2(m_s[pi] - m)
        l = l + l_s[pi] * e
        acc = acc + acc_s[pi] * e
    o = acc / l
    for lo in range(4):
        for hi in range(4):
            r = 4 * hi + lo
            mid[lo, pl.ds(hi, BLK, stride=4), :] = o[r * BLK:(r + 1) * BLK]
    for lo in range(4):
        qp[pl.ds(lo, quarter, stride=4), :] = mid[lo]
    g = g_ref[...].astype(F32)
    o_ref[...] = (qp[...] * (g * _sigmoid(g))).astype(o_ref.dtype)


def _attn_prompt(q, k, v, g, bias_tbl, batch, seq):
    n_span = seq // SPAN
    n_hp = N_HEADS // 2
    cur = lambda hp, b, s: (b * n_span + s, hp)
    blk = (SPAN, LANES)
    return pl.pallas_call(
        _attn_kernel,
        grid=(n_hp, batch, n_span),
        in_specs=[
            pl.BlockSpec(blk, cur),
            pl.BlockSpec(blk, cur),
            pl.BlockSpec(blk, cur),
            pl.BlockSpec(blk, cur),
            pl.BlockSpec((None, 2, len(PATTERNS), 2 * BLK, 2 * BLK), lambda hp, b, s: (hp, 0, 0, 0, 0)),
        ],
        out_specs=pl.BlockSpec(blk, cur),
        out_shape=jax.ShapeDtypeStruct((batch * seq, D_ATTN), BF16),
        scratch_shapes=[
            pltpu.VMEM(blk, F32),
            pltpu.VMEM((2,) + blk, F32),
            pltpu.VMEM((2,) + blk, F32),
            pltpu.VMEM((4, SPAN // 4, LANES), F32),
            pltpu.VMEM((len(PATTERNS),) + blk, F32),
            pltpu.VMEM((len(PATTERNS),) + blk, F32),
            pltpu.VMEM((len(PATTERNS),) + blk, F32),
            pltpu.VMEM((2, ATTN_GROUP, 2 * BLK, 2 * BLK), F32),
            pltpu.VMEM((2, ATTN_GROUP, 2 * BLK, LANES), F32),
        ],
        compiler_params=pltpu.CompilerParams(
            dimension_semantics=("arbitrary", "arbitrary", "arbitrary"), vmem_limit_bytes=VMEM_LIMIT),
    )(q, k, v, g, bias_tbl)


def _bias_lookup(rel_bias, dist):
    bucket = _t5_bucket(dist)[..., None]
    edges = jnp.arange(N_BUCKETS)
    onehot = ((bucket >= edges) & (bucket < edges + 1)).astype(F32)
    return jnp.einsum('...b,bh->h...', onehot, rel_bias.astype(F32), precision=lax.Precision.HIGHEST)


def _prompt_bias_table(rel_bias):
    tbls = []
    for w, d in PATTERNS:
        n_chunk = 16 // d
        n = jnp.arange(BLK)
        idx = (n % (BLK // n_chunk)) * n_chunk + n // (BLK // n_chunk)
        i = idx[:, None]
        j = jnp.concatenate([idx, BLK + idx])[None, :]
        rel = i + BLK - j
        band = (rel >= 0) & (rel <= w // d)
        bias = _bias_lookup(rel_bias, jnp.maximum(rel, 0) * d) * LOG2E
        normal = jnp.where(band[None], bias, NEG)
        first = jnp.where((band & (j >= BLK))[None], bias, NEG)
        tbls.append(jnp.stack([normal, first], axis=0))
    t = jnp.stack(tbls, axis=1)
    t = t.reshape(2, len(PATTERNS), N_HEADS // 2, 2 * BLK, 2 * BLK)
    return jnp.moveaxis(t, 2, 0)


def _ssd_kernel(xbc_ref, z_ref, dt_ref, cw_ref, cb_ref, dtb_ref, alog_ref, dsk_ref, nw_ref,
                e_ref, tril_ref, s_ref, h_ref, cbuf):
    c = pl.program_id(1)

    n_slab = CONV_DIM // LANES

    @pl.when(c == 0)
    def _():
        cbuf[:, 0:8, :] = jnp.zeros((n_slab, 8, LANES), F32)
        h_ref[...] = jnp.zeros_like(h_ref)

    xc_slabs = []
    for j in range(n_slab):
        cols = slice(j * LANES, (j + 1) * LANES)
        xj = xbc_ref[:, cols]
        cbuf[j, 8:8 + CHUNK, :] = xj
        acc = cb_ref[:, cols] + xj * cw_ref[CONV_W - 1:CONV_W, cols]
        for i in range(CONV_W - 1):
            tap = cbuf[pl.ds(j, 1, stride=2), pl.ds(8 - (CONV_W - 1) + i, CHUNK), :][0]
            acc = acc + tap * cw_ref[i:i + 1, cols]
        xc_slabs.append(acc * _sigmoid(acc))
        cbuf[j, 0:8, :] = xj[CHUNK - 8:, :]

    n_x = D_SSD // LANES
    xs = jnp.concatenate(xc_slabs[:n_x], axis=1)
    lane = lax.broadcasted_iota(jnp.int32, (CHUNK, LANES), 1)
    sub = lax.broadcasted_iota(jnp.int32, (CHUNK, LANES), 0)
    head_lane = lane < N_HEADS
    dt = jnp.where(head_lane, _softplus(dt_ref[...] + dtb_ref[...]), 0.0)
    la = dt * (-jnp.exp(alog_ref[...]))

    def hi_lo(val):
        hi = val.astype(BF16).astype(F32)
        return (hi + pltpu.roll(val - hi, N_HEADS, axis=1)).astype(BF16)

    cs2 = jnp.dot(tril_ref[...], hi_lo(la), preferred_element_type=F32)
    a_cs = jnp.where(head_lane, cs2 + pltpu.roll(cs2, LANES - N_HEADS, axis=1), 0.0)
    ea = jnp.where(head_lane, jnp.exp(a_cs), 0.0)
    dte = jnp.where(head_lane, jnp.exp(a_cs[CHUNK - 1:CHUNK, :] - a_cs), 0.0)

    expanded = jnp.dot(jnp.concatenate([hi_lo(dt), hi_lo(ea), hi_lo(dte)], axis=0), e_ref[...],
                       preferred_element_type=F32)
    dtx, eax, dtex = expanded[:CHUNK], expanded[CHUNK:2 * CHUNK], expanded[2 * CHUNK:]
    xdt_f = xs * dtx
    xdt = xdt_f.astype(BF16)
    xdte = (xdt_f * dtex).astype(BF16)

    a_cs_t = a_cs.T
    causal = sub >= lane
    is_a = lane < HEAD_DIM
    ys = []
    for g in range(N_GROUPS):
        b_g = xc_slabs[n_x + g]
        c_g = xc_slabs[n_x + N_GROUPS + g]
        b_bf = b_g.astype(BF16)
        c_bf = c_g.astype(BF16)
        cb = lax.dot_general(c_bf, b_bf, (((1,), (1,)), ((), ())), preferred_element_type=F32)
        gcols = slice(g * 512, (g + 1) * 512)
        h_prev = h_ref[:, gcols]
        y_off = jnp.dot(c_bf, h_prev.astype(BF16), preferred_element_type=F32) * eax[:, gcols]
        st = jnp.dot(b_g.T.astype(BF16), xdte[:, gcols], preferred_element_type=F32)
        h_ref[:, gcols] = h_prev * eax[CHUNK - 1:CHUNK, gcols] + st
        for hp in range(4):
            pair = []
            for which in range(2):
                h = g * 8 + hp * 2 + which
                col = jnp.sum(jnp.where(lane == h, a_cs, 0.0), axis=1, keepdims=True)
                seg = col - a_cs_t[h:h + 1, :]
                lmat = jnp.exp(jnp.where(causal, seg, NEG))
                pair.append((cb * lmat).astype(BF16))
            x_pair = xdt[:, g * 512 + hp * LANES:g * 512 + (hp + 1) * LANES]
            y_a = jnp.dot(pair[0], x_pair, preferred_element_type=F32)
            y_b = jnp.dot(pair[1], x_pair, preferred_element_type=F32)
            ys.append(jnp.where(is_a, y_a, y_b) + y_off[:, hp * LANES:(hp + 1) * LANES])
    y = jnp.concatenate(ys, axis=1) + dsk_ref[...] * xs
    zf = z_ref[...].astype(F32)
    yz = y * (zf * _sigmoid(zf))
    var = jnp.mean(yz * yz, axis=-1, keepdims=True)
    s_ref[...] = (yz * lax.rsqrt(var + EPS) * nw_ref[...]).astype(s_ref.dtype)


def _ssd_prompt(xbc, z, dt, cw, cb, dtb, alog, dsk, nw, emat, tril, batch, seq):
    nc = seq // CHUNK
    row = lambda b, c: (b * nc + c, 0)
    const = lambda b, c: (0, 0)
    return pl.pallas_call(
        _ssd_kernel,
        grid=(batch, nc),
        in_specs=[
            pl.BlockSpec((CHUNK, CONV_DIM), row),
            pl.BlockSpec((CHUNK, D_SSD), row),
            pl.BlockSpec((CHUNK, LANES), row),
            pl.BlockSpec((CONV_W, CONV_DIM), const),
            pl.BlockSpec((1, CONV_DIM), const),
            pl.BlockSpec((1, LANES), const),
            pl.BlockSpec((1, LANES), const),
            pl.BlockSpec((1, D_SSD), const),
            pl.BlockSpec((1, D_SSD), const),
            pl.BlockSpec((LANES, D_SSD), const),
            pl.BlockSpec((CHUNK, CHUNK), const),
        ],
        out_specs=[
            pl.BlockSpec((CHUNK, D_SSD), row),
            pl.BlockSpec((None, D_STATE, D_SSD), lambda b, c: (b, 0, 0)),
        ],
        out_shape=(jax.ShapeDtypeStruct((batch * seq, D_SSD), BF16),
                   jax.ShapeDtypeStruct((batch, D_STATE, D_SSD), F32)),
        scratch_shapes=[pltpu.VMEM((CONV_DIM // LANES, CHUNK + 8, LANES), F32)],
        compiler_params=pltpu.CompilerParams(
            dimension_semantics=("arbitrary", "arbitrary"), vmem_limit_bytes=VMEM_LIMIT),
    )(xbc, z, dt, cw, cb, dtb, alog, dsk, nw, emat, tril)


def _attn_sample_kernel(qt_ref, knt_ref, vnt_ref, gt_ref, k_ref, v_ref, btbl_ref, bias0_ref, o_ref):
    hh = pl.program_id(1)

    @pl.when(hh == 0)
    def _():
        o_ref[...] = jnp.zeros_like(o_ref)

    lane = lax.broadcasted_iota(jnp.int32, (HEAD_DIM, LANES), 1)
    lane1 = lax.broadcasted_iota(jnp.int32, (1, LANES), 1)
    qt = qt_ref[...] * (HEAD_DIM ** -0.5)
    n_pat = float(len(PATTERNS))
    for j in range(SAMPLE_HEADS_PER_STEP):
        h = hh * SAMPLE_HEADS_PER_STEP + j
        pick = lane == h

        def col(val, pick=pick):
            return jnp.sum(jnp.where(pick, val, 0.0), axis=1, keepdims=True)

        qc, knc, vnc, gc = col(qt), col(knt_ref[...]), col(vnt_ref[...]), col(gt_ref[...])
        b0 = jnp.sum(jnp.where(lane1 == h, bias0_ref[...], 0.0), axis=1, keepdims=True)
        s0 = jnp.sum(qc * knc, axis=0, keepdims=True) + b0
        s = jnp.sum(k_ref[j] * qc, axis=0, keepdims=True)
        sp = [s + btbl_ref[pi, pl.ds(h, 1), :] for pi in range(len(PATTERNS))]
        m = s0
        for x in sp:
            m = jnp.maximum(m, jnp.max(x, axis=1, keepdims=True))
        p0 = n_pat * jnp.exp(s0 - m)
        pw = jnp.exp(sp[0] - m)
        for x in sp[1:]:
            pw = pw + jnp.exp(x - m)
        l = jnp.sum(pw, axis=1, keepdims=True) + p0
        oc = (jnp.sum(v_ref[j] * pw, axis=1, keepdims=True) + p0 * vnc) / l
        o_ref[...] = jnp.where(pick, oc * (gc * _sigmoid(gc)), o_ref[...])


def _attn_sample(qt, knt, vnt, gt, cache_k_t, cache_v_t, btbl, bias0):
    b, n_past = cache_k_t.shape[0], cache_k_t.shape[3]
    tok = pl.BlockSpec((None, HEAD_DIM, LANES), lambda i, hh: (i, 0, 0))
    cache = pl.BlockSpec((None, SAMPLE_HEADS_PER_STEP, HEAD_DIM, n_past), lambda i, hh: (i, hh, 0, 0))
    return pl.pallas_call(
        _attn_sample_kernel,
        grid=(b, N_HEADS // SAMPLE_HEADS_PER_STEP),
        in_specs=[tok, tok, tok, tok, cache, cache,
                  pl.BlockSpec((len(PATTERNS), N_HEADS, n_past), lambda i, hh: (0, 0, 0)),
                  pl.BlockSpec((1, LANES), lambda i, hh: (0, 0))],
        out_specs=tok,
        out_shape=jax.ShapeDtypeStruct((b, HEAD_DIM, LANES), F32),
        compiler_params=pltpu.CompilerParams(
            dimension_semantics=("arbitrary", "arbitrary"), vmem_limit_bytes=VMEM_LIMIT),
    )(qt, knt, vnt, gt, cache_k_t, cache_v_t, btbl, bias0)


def _sample_bias_tables(rel_bias, n_past):
    dist = n_past - jnp.arange(n_past)
    bias = _bias_lookup(rel_bias, dist)
    tbls = [jnp.where(((dist % d == 0) & (dist <= w))[None], bias, NEG) for w, d in PATTERNS]
    bias0 = _bias_lookup(rel_bias, jnp.zeros((1,), jnp.int32))
    return jnp.stack(tbls, axis=0), jnp.pad(bias0.reshape(1, N_HEADS), ((0, 0), (0, LANES - N_HEADS)))


def _ssd_sample_kernel(xbc_ref, z_ref, dt_ref, sc_ref, h_ref, cw_ref, cb_ref, dtb_ref, alog_ref, dsk_ref,
                       nw_ref, e_ref, s_ref, conv_out_ref, h_out_ref):
    xnew = xbc_ref[...]
    sc = sc_ref[...]
    acc = cb_ref[...] + xnew * cw_ref[CONV_W - 1:CONV_W, :]
    for i in range(CONV_W - 1):
        acc = acc + sc[i:i + 1, :] * cw_ref[i:i + 1, :]
    xc = acc * _sigmoid(acc)
    conv_out_ref[0:CONV_W - 2, :] = sc[1:CONV_W - 1, :]
    conv_out_ref[CONV_W - 2:CONV_W - 1, :] = xnew

    xs = xc[:, :D_SSD]
    lane1 = lax.broadcasted_iota(jnp.int32, (1, LANES), 1)
    dt = jnp.where(lane1 < N_HEADS, _softplus(dt_ref[...] + dtb_ref[...]), 0.0)
    da = jnp.where(lane1 < N_HEADS, jnp.exp(dt * (-jnp.exp(alog_ref[...]))), 0.0)

    def expand(val):
        v8 = jnp.broadcast_to(val, (8, LANES))
        out = jnp.zeros((8, D_SSD), F32)
        for _ in range(3):
            part = v8.astype(BF16)
            out = out + jnp.dot(part, e_ref[...], preferred_element_type=F32)
            v8 = v8 - part.astype(F32)
        return out[0:1, :]

    xdt = xs * expand(dt)
    dax = expand(da)

    lane = lax.broadcasted_iota(jnp.int32, (HEAD_DIM, LANES), 1)
    sub = lax.broadcasted_iota(jnp.int32, (HEAD_DIM, LANES), 0)
    eye2 = (lane % HEAD_DIM) == sub
    is_a = lane < HEAD_DIM

    def to_cols(row):
        mat = jnp.where(eye2, jnp.broadcast_to(row, (HEAD_DIM, LANES)), 0.0)
        col_a = jnp.sum(jnp.where(is_a, mat, 0.0), axis=1, keepdims=True)
        col_b = jnp.sum(jnp.where(is_a, 0.0, mat), axis=1, keepdims=True)
        return col_a, col_b

    y_rows = []
    for hp in range(N_HEADS // 2):
        g = hp // 4
        b_row = xc[:, D_SSD + g * D_STATE:D_SSD + (g + 1) * D_STATE]
        c_row = xc[:, D_SSD + (N_GROUPS + g) * D_STATE:D_SSD + (N_GROUPS + g + 1) * D_STATE]
        cols = slice(hp * LANES, (hp + 1) * LANES)
        x_cols = to_cols(xdt[:, cols])
        d_cols = to_cols(dax[:, cols])
        y_cols = []
        for which in range(2):
            h = hp * 2 + which
            h_new = h_ref[h] * d_cols[which] + x_cols[which] * b_row
            h_out_ref[h] = h_new
            y_cols.append(jnp.sum(h_new * c_row, axis=1, keepdims=True))
        y_mat = jnp.where(eye2, jnp.where(is_a, y_cols[0], y_cols[1]), 0.0)
        y_rows.append(jnp.sum(y_mat, axis=0, keepdims=True))
    y = jnp.concatenate(y_rows, axis=1) + dsk_ref[...] * xs
    zf = z_ref[...].astype(F32)
    yz = y * (zf * _sigmoid(zf))
    var = jnp.mean(yz * yz, axis=-1, keepdims=True)
    s_ref[...] = (yz * lax.rsqrt(var + EPS) * nw_ref[...]).astype(s_ref.dtype)


def _ssd_sample(xbc3, z3, dt3, state_conv, state_ssm, cw, cb, dtb, alog, dsk, nw, emat):
    b = xbc3.shape[0]
    const = lambda i: (0, 0)
    tok = lambda width: pl.BlockSpec((None, 1, width), lambda i: (i, 0, 0))
    conv_spec = pl.BlockSpec((None, CONV_W - 1, CONV_DIM), lambda i: (i, 0, 0))
    ssm_spec = pl.BlockSpec((None, N_HEADS, HEAD_DIM, D_STATE), lambda i: (i, 0, 0, 0))
    return pl.pallas_call(
        _ssd_sample_kernel,
        grid=(b,),
        in_specs=[
            tok(CONV_DIM), tok(D_SSD), tok(LANES), conv_spec, ssm_spec,
            pl.BlockSpec((CONV_W, CONV_DIM), const),
            pl.BlockSpec((1, CONV_DIM), const),
            pl.BlockSpec((1, LANES), const),
            pl.BlockSpec((1, LANES), const),
            pl.BlockSpec((1, D_SSD), const),
            pl.BlockSpec((1, D_SSD), const),
            pl.BlockSpec((LANES, D_SSD), const),
        ],
        out_specs=[tok(D_SSD), conv_spec, ssm_spec],
        out_shape=(jax.ShapeDtypeStruct((b, 1, D_SSD), BF16),
                   jax.ShapeDtypeStruct((b, CONV_W - 1, CONV_DIM), F32),
                   jax.ShapeDtypeStruct((b, N_HEADS, HEAD_DIM, D_STATE), F32)),
        compiler_params=pltpu.CompilerParams(
            dimension_semantics=("arbitrary",), vmem_limit_bytes=VMEM_LIMIT),
    )(xbc3, z3, dt3, state_conv, state_ssm, cw, cb, dtb, alog, dsk, nw, emat)


def kernel(x_prompt, x_sample, cache_win_k, cache_win_v, state_conv, state_ssm, norm_w, w_in, q_norm_w,
           k_norm_w, rel_bias, conv_w, conv_b, dt_bias, a_log, d_skip, ssd_norm_w, w_out):
    assert x_prompt.shape[-1] == D_MODEL and w_in.shape[0] == 1, "single-layer model of width 1024 only"
    batch, seq, _ = x_prompt.shape
    dec_batch, dec_seq, _ = x_sample.shape
    assert dec_seq == 1 and seq % SPAN == 0 and cache_win_k.shape[2] == WINDOW_MAX

    w_pad = jnp.pad(w_in[0], ((0, 0), (0, D_IN_PAD - D_IN_PROJ))).astype(BF16)
    w_out_b = w_out[0].astype(BF16)
    nw = norm_w[0].reshape(1, D_MODEL)
    qnw = jnp.tile(q_norm_w[0], 512 // HEAD_DIM).reshape(1, 512)
    knw = jnp.tile(k_norm_w[0], 512 // HEAD_DIM).reshape(1, 512)
    idx256 = jnp.arange(256) // HEAD_DIM
    gsum = (idx256[:, None] == idx256[None, :]).astype(BF16)
    cw, cb = conv_w[0], conv_b[0].reshape(1, CONV_DIM)
    pad_heads = lambda a: jnp.pad(a.reshape(1, N_HEADS), ((0, 0), (0, LANES - N_HEADS)))
    dtb, alog = pad_heads(dt_bias[0]), pad_heads(a_log[0])
    dsk = jnp.repeat(d_skip[0], HEAD_DIM).reshape(1, D_SSD)
    snw = ssd_norm_w[0].reshape(1, D_SSD)
    erow = jnp.arange(LANES)[:, None]
    emat = ((erow % N_HEADS == (jnp.arange(D_SSD) // HEAD_DIM)[None, :]) & (erow < 2 * N_HEADS)).astype(BF16)
    tril = (jnp.arange(CHUNK)[:, None] >= jnp.arange(CHUNK)[None, :]).astype(BF16)

    xp = x_prompt.reshape(batch * seq, D_MODEL)
    q, k, v, g, z, xbc, dt = _inproj(xp, nw, w_pad, qnw, knw, gsum, tm=512)
    a = _attn_prompt(q, k, v, g, _prompt_bias_table(rel_bias), batch, seq)
    s, h_fin = _ssd_prompt(xbc, z, dt, cw, cb, dtb, alog, dsk, snw, emat, tril, batch, seq)
    y_p = _outproj(xp, a, s, w_out_b, tm=512).reshape(batch, seq, D_MODEL)
    nwin = min(WINDOW_MAX, seq)
    heads = lambda t: t.reshape(batch, seq, D_ATTN)[:, seq - nwin:].reshape(1, batch, nwin, N_HEADS, HEAD_DIM)
    kp, vp = heads(k), heads(v)
    cp = xbc.reshape(batch, seq, CONV_DIM)[None, :, seq - (CONV_W - 1):]
    hp = jnp.swapaxes(h_fin, 1, 2).reshape(batch, N_HEADS, HEAD_DIM, D_STATE)[None]

    xs2 = x_sample.reshape(dec_batch, D_MODEL)
    qs, ks, vs, gs, zs, xbcs, dts = _inproj(xs2, nw, w_pad, qnw, knw, gsum, tm=dec_batch)
    tok_t = lambda t: jnp.pad(jnp.swapaxes(t.astype(F32).reshape(dec_batch, N_HEADS, HEAD_DIM), 1, 2),
                              ((0, 0), (0, 0), (0, LANES - N_HEADS)))
    cache_t = lambda c: jnp.transpose(c[0], (0, 2, 3, 1))
    btbl, bias0 = _sample_bias_tables(rel_bias, cache_win_k.shape[2])
    a_t = _attn_sample(tok_t(qs), tok_t(ks), tok_t(vs), tok_t(gs), cache_t(cache_win_k), cache_t(cache_win_v),
                       btbl, bias0)
    a_s = jnp.swapaxes(a_t[:, :, :N_HEADS], 1, 2)
    s_s, conv_s, h_s = _ssd_sample(xbcs.reshape(dec_batch, 1, CONV_DIM), zs.reshape(dec_batch, 1, D_SSD),
                                   dts.reshape(dec_batch, 1, LANES), state_conv[0], state_ssm[0],
                                   cw, cb, dtb, alog, dsk, snw, emat)
    y_s = _outproj(xs2, a_s.reshape(dec_batch, D_ATTN).astype(BF16), s_s.reshape(dec_batch, D_SSD),
                   w_out_b, tm=dec_batch).reshape(dec_batch, 1, D_MODEL)
    k_s = ks.reshape(1, dec_batch, 1, N_HEADS, HEAD_DIM)
    v_s = vs.reshape(1, dec_batch, 1, N_HEADS, HEAD_DIM)
    return (y_p, y_s, kp, vp, cp, hp, k_s, v_s, conv_s[None], h_s[None])
```

```python
import functools
import math

import jax
import jax.numpy as jnp
from jax import lax
from jax.experimental import pallas as pl
from jax.experimental.pallas import tpu as pltpu

F32 = jnp.float32
BF16 = jnp.bfloat16

D_MODEL = 1024
D_ATTN = 1024
D_SSD = 1024
HEAD_DIM = 64
N_HEADS = 16
PATTERNS = ((128, 1), (512, 4), (2048, 16))
WINDOW_MAX = 2048
BLK = 128
N_BUCKETS = 32
D_STATE = 128
N_GROUPS = 2
CONV_W = 4
CONV_DIM = D_SSD + 2 * N_GROUPS * D_STATE
CHUNK = 128
EPS = 1e-6
D_IN_PROJ = 4 * D_ATTN + D_SSD + CONV_DIM + N_HEADS
LANES = 128
D_IN_PAD = D_IN_PROJ - N_HEADS + LANES
SPAN = BLK * 16
NEG = -1e30
LOG2E = math.log2(math.e)
ATTN_GROUP = 16
SAMPLE_HEADS_PER_STEP = 8
SSD_CHUNKS_PER_STEP = 4
VMEM_LIMIT = 56 * 1024 * 1024


def _silu(x):
    h = 0.5 * x
    return h + h * jnp.tanh(h)


def _softplus(x):
    return jnp.maximum(x, 0.0) + jnp.log(1.0 + jnp.exp(-jnp.abs(x)))


def _t5_bucket(dist):
    max_exact = N_BUCKETS // 2
    d_f = jnp.maximum(dist, 1).astype(F32)
    large = max_exact + (jnp.log(d_f / max_exact) / math.log(WINDOW_MAX / max_exact)
                         * (N_BUCKETS - max_exact)).astype(jnp.int32)
    large = jnp.minimum(large, N_BUCKETS - 1)
    return jnp.where(dist < max_exact, dist, large)


def _inproj_kernel(x_ref, nw_ref, w_ref, qnw_ref, knw_ref,
                   q_ref, k_ref, v_ref, g_ref, z_ref, xbc_ref, dt_ref, kt_ref=None, vt_ref=None):
    x = x_ref[...]
    h = (x * nw_ref[...]).astype(BF16)
    r = lax.rsqrt(jnp.mean(x * x, axis=-1, keepdims=True) + EPS)

    def seg(c0, width):
        return jnp.dot(h, w_ref[:, c0:c0 + width], preferred_element_type=F32) * r

    is_a = lax.broadcasted_iota(jnp.int32, (x.shape[0], LANES), 1) < HEAD_DIM

    def head_rms(pj):
        p2 = pj * pj
        ss_a = jnp.sum(jnp.where(is_a, p2, 0.0), axis=1, keepdims=True)
        ss_b = jnp.sum(jnp.where(is_a, 0.0, p2), axis=1, keepdims=True)
        return jnp.where(is_a, lax.rsqrt(ss_a * (1.0 / HEAD_DIM) + EPS), lax.rsqrt(ss_b * (1.0 / HEAD_DIM) + EPS))

    for out_ref, base, hw_ref in ((q_ref, 0, qnw_ref), (k_ref, D_ATTN, knw_ref)):
        for c in range(2):
            p = seg(base + 512 * c, 512)
            rs = jnp.concatenate([head_rms(p[:, LANES * j:LANES * (j + 1)]) for j in range(512 // LANES)], axis=1)
            normed = p * rs * hw_ref[...]
            out_ref[:, 512 * c:512 * (c + 1)] = normed
            if out_ref is k_ref and kt_ref is not None:
                kt_ref[512 * c:512 * (c + 1), :] = normed.T
    for c in range(2):
        v_c = seg(2 * D_ATTN + 512 * c, 512)
        v_ref[:, 512 * c:512 * (c + 1)] = v_c
        if vt_ref is not None:
            vt_ref[512 * c:512 * (c + 1), :] = v_c.T
        g_ref[:, 512 * c:512 * (c + 1)] = seg(3 * D_ATTN + 512 * c, 512).astype(g_ref.dtype)
        z_ref[:, 512 * c:512 * (c + 1)] = seg(4 * D_ATTN + 512 * c, 512).astype(z_ref.dtype)
    for c in range(3):
        xbc_ref[:, 512 * c:512 * (c + 1)] = seg(5 * D_ATTN + 512 * c, 512)
    dt_ref[...] = seg(5 * D_ATTN + CONV_DIM, LANES)


def _inproj(x2d, nw, w_pad, qnw, knw, tm, window=None):
    t = x2d.shape[0]
    row = lambda i: (i, 0)
    const = lambda i: (0, 0)
    win_shapes, win_specs = (), []
    if window is not None:
        seq, nwin = window
        assert seq % tm == 0 and nwin % tm == 0
        per_seq, first = seq // tm, (seq - nwin) // tm
        win_spec = pl.BlockSpec((None, D_ATTN, tm),
                                lambda i: (i // per_seq, 0, jnp.maximum(i % per_seq - first, 0)))
        win_shapes = (jax.ShapeDtypeStruct((t // seq, D_ATTN, nwin), F32),) * 2
        win_specs = [win_spec, win_spec]
    outs = (
        jax.ShapeDtypeStruct((t, D_ATTN), F32),
        jax.ShapeDtypeStruct((t, D_ATTN), F32),
        jax.ShapeDtypeStruct((t, D_ATTN), F32),
        jax.ShapeDtypeStruct((t, D_ATTN), BF16),
        jax.ShapeDtypeStruct((t, D_SSD), BF16),
        jax.ShapeDtypeStruct((t, CONV_DIM), F32),
        jax.ShapeDtypeStruct((t, LANES), F32),
    )
    return pl.pallas_call(
        _inproj_kernel,
        grid=(t // tm,),
        in_specs=[
            pl.BlockSpec((tm, D_MODEL), row),
            pl.BlockSpec((1, D_MODEL), const),
            pl.BlockSpec((D_MODEL, D_IN_PAD), const, pipeline_mode=pl.Buffered(1)),
            pl.BlockSpec((1, 512), const),
            pl.BlockSpec((1, 512), const),
        ],
        out_specs=[
            pl.BlockSpec((tm, D_ATTN), row),
            pl.BlockSpec((tm, D_ATTN), row),
            pl.BlockSpec((tm, D_ATTN), row),
            pl.BlockSpec((tm, D_ATTN), row),
            pl.BlockSpec((tm, D_SSD), row),
            pl.BlockSpec((tm, CONV_DIM), row),
            pl.BlockSpec((tm, LANES), row),
        ] + win_specs,
        out_shape=outs + win_shapes,
        compiler_params=pltpu.CompilerParams(
            dimension_semantics=("arbitrary",), vmem_limit_bytes=VMEM_LIMIT),
    )(x2d, nw, w_pad, qnw, knw)


def _outproj_kernel(x_ref, a_ref, s_ref, w_ref, y_ref):
    y_ref[...] = (x_ref[...]
                  + jnp.dot(a_ref[...], w_ref[0:D_ATTN, :], preferred_element_type=F32)
                  + jnp.dot(s_ref[...], w_ref[D_ATTN:, :], preferred_element_type=F32))


def _outproj(x2d, a, s, w_out_b, tm):
    t = x2d.shape[0]
    row = lambda i: (i, 0)
    return pl.pallas_call(
        _outproj_kernel,
        grid=(t // tm,),
        in_specs=[
            pl.BlockSpec((tm, D_MODEL), row),
            pl.BlockSpec((tm, D_ATTN), row),
            pl.BlockSpec((tm, D_SSD), row),
            pl.BlockSpec((D_ATTN + D_SSD, D_MODEL), lambda i: (0, 0), pipeline_mode=pl.Buffered(1)),
        ],
        out_specs=pl.BlockSpec((tm, D_MODEL), row),
        out_shape=jax.ShapeDtypeStruct((t, D_MODEL), F32),
        compiler_params=pltpu.CompilerParams(
            dimension_semantics=("arbitrary",), vmem_limit_bytes=VMEM_LIMIT),
    )(x2d, a, s, w_out_b)


def _attn_kernel(q_ref, k_ref, v_ref, g_ref, bias_ref, o_ref, qp, kp, vp, mid, m_s, l_s, acc_s, s_buf, m_buf):
    s_idx = pl.program_id(2)
    first = (s_idx == 0).astype(jnp.int32)
    slot = s_idx % 2
    pslot = 1 - slot
    lane = lax.broadcasted_iota(jnp.int32, (BLK, LANES), 1)
    is_a = lane < HEAD_DIM
    qscale = HEAD_DIM ** -0.5 * LOG2E
    ones = jnp.ones((2 * BLK, LANES), BF16)
    quarter = SPAN // 4

    @pl.when(s_idx == 0)
    def _():
        kp[pslot] = jnp.zeros((SPAN, LANES), F32)
        vp[pslot] = jnp.zeros((SPAN, LANES), F32)

    def regroup(src_ref, store, scale=None):
        for lo in range(4):
            mid[lo] = src_ref[pl.ds(lo, quarter, stride=4), :]
        for lo in range(4):
            for hi in range(4):
                val = mid[lo, pl.ds(hi, BLK, stride=4), :]
                store(4 * hi + lo, val if scale is None else val * scale)

    def store_q(r, val):
        qp[pl.ds(r * BLK, BLK), :] = val

    def store_k(r, val):
        kp[slot, pl.ds(r * BLK, BLK), :] = val

    def store_v(r, val):
        vp[slot, pl.ds(r * BLK, BLK), :] = val

    regroup(q_ref, store_q, qscale)
    regroup(k_ref, store_k)
    regroup(v_ref, store_v)

    def chunks(d, blk):
        n_chunk = 16 // d
        length = BLK // n_chunk
        u, r = blk // d, blk % d
        is_u0 = u == 0
        cur = [pl.multiple_of((d * c + r) * BLK + u * length, 8) for c in range(n_chunk)]
        back = jnp.where(is_u0, BLK - length, (u - 1) * length)
        prev = [pl.multiple_of((d * c + r) * BLK + back, 8) for c in range(n_chunk)]
        return is_u0, length, cur, jnp.where(is_u0, pslot, slot), prev

    def gather(read, starts, length):
        return jnp.concatenate([read(pl.ds(st, length)) for st in starts], axis=0)

    def scatter(write, starts, length, val):
        for c, st in enumerate(starts):
            write(pl.ds(st, length), val[c * length:(c + 1) * length])

    def qk_stage(pi, d, grp, sl):
        for j in range(ATTN_GROUP):
            is_u0, length, cur, prev_slot, prev = chunks(d, grp * ATTN_GROUP + j)
            qsc = gather(lambda rows: qp[rows, :], cur, length)
            qq = jnp.concatenate([jnp.where(is_a, qsc, 0.0), jnp.where(is_a, 0.0, qsc)], axis=0).astype(BF16)
            kk = jnp.concatenate([gather(lambda rows: kp[prev_slot, rows, :], prev, length),
                                  gather(lambda rows: kp[slot, rows, :], cur, length)], axis=0).astype(BF16)
            bias = bias_ref[jnp.where(is_u0, first, 0), pi]
            s = lax.dot_general(qq, kk, (((1,), (1,)), ((), ())), preferred_element_type=F32) + bias
            s_buf[sl, j] = s
            m_buf[sl, j] = jnp.broadcast_to(jnp.max(s, axis=1, keepdims=True), (2 * BLK, LANES))

    def pv_stage(pi, d, grp, sl):
        for j in range(ATTN_GROUP):
            _, length, cur, prev_slot, prev = chunks(d, grp * ATTN_GROUP + j)
            m = m_buf[sl, j]
            p = jnp.exp2(s_buf[sl, j] - jnp.concatenate([m, m], axis=1)).astype(BF16)
            vv = jnp.concatenate([gather(lambda rows: vp[prev_slot, rows, :], prev, length),
                                  gather(lambda rows: vp[slot, rows, :], cur, length)], axis=0).astype(BF16)
            ol = jnp.dot(p, jnp.concatenate([vv, ones], axis=1), preferred_element_type=F32)
            for ref, val in ((acc_s, jnp.where(is_a, ol[:BLK, :LANES], ol[BLK:, :LANES])),
                             (l_s, jnp.where(is_a, ol[:BLK, LANES:], ol[BLK:, LANES:])),
                             (m_s, jnp.where(is_a, m[:BLK], m[BLK:]))):
                scatter(lambda rows, v, ref=ref: ref.__setitem__((pi, rows, slice(None)), v), cur, length, val)

    groups = [(pi, d, jnp.int32(grp)) for pi, (_, d) in enumerate(PATTERNS)
              for grp in range(SPAN // BLK // ATTN_GROUP)]

    def region(fn):
        def body(i, carry):
            fn()
            return carry
        lax.fori_loop(0, 1 + jnp.minimum(pl.program_id(0), 0), body, 0)

    region(lambda: qk_stage(*groups[0], 0))
    for i, grp in enumerate(groups):
        def step(i=i, grp=grp):
            pv_stage(*grp, i % 2)
            if i + 1 < len(groups):
                qk_stage(*groups[i + 1], (i + 1) % 2)
        region(step)

    m = jnp.maximum(jnp.maximum(m_s[0], m_s[1]), m_s[2])
    l = jnp.zeros((SPAN, LANES), F32)
    acc = jnp.zeros((SPAN, LANES), F32)
    for pi in range(len(PATTERNS)):
        e = jnp.exp2(m_s[pi] - m)
        l = l + l_s[pi] * e
        acc = acc + acc_s[pi] * e
    o = acc / l
    for lo in range(4):
        for hi in range(4):
            r = 4 * hi + lo
            mid[lo, pl.ds(hi, BLK, stride=4), :] = o[r * BLK:(r + 1) * BLK]
    for lo in range(4):
        qp[pl.ds(lo, quarter, stride=4), :] = mid[lo]
    g = g_ref[...].astype(F32)
    o_ref[...] = (qp[...] * _silu(g)).astype(o_ref.dtype)


def _attn_prompt(q, k, v, g, bias_tbl, batch, seq):
    n_span = seq // SPAN
    n_hp = N_HEADS // 2
    cur = lambda hp, b, s: (b * n_span + s, hp)
    blk = (SPAN, LANES)
    return pl.pallas_call(
        _attn_kernel,
        grid=(n_hp, batch, n_span),
        in_specs=[
            pl.BlockSpec(blk, cur),
            pl.BlockSpec(blk, cur),
            pl.BlockSpec(blk, cur),
            pl.BlockSpec(blk, cur),
            pl.BlockSpec((None, 2, len(PATTERNS), 2 * BLK, 2 * BLK), lambda hp, b, s: (hp, 0, 0, 0, 0)),
        ],
        out_specs=pl.BlockSpec(blk, cur),
        out_shape=jax.ShapeDtypeStruct((batch * seq, D_ATTN), BF16),
        scratch_shapes=[
            pltpu.VMEM(blk, F32),
            pltpu.VMEM((2,) + blk, F32),
            pltpu.VMEM((2,) + blk, F32),
            pltpu.VMEM((4, SPAN // 4, LANES), F32),
            pltpu.VMEM((len(PATTERNS),) + blk, F32),
            pltpu.VMEM((len(PATTERNS),) + blk, F32),
            pltpu.VMEM((len(PATTERNS),) + blk, F32),
            pltpu.VMEM((2, ATTN_GROUP, 2 * BLK, 2 * BLK), F32),
            pltpu.VMEM((2, ATTN_GROUP, 2 * BLK, LANES), F32),
        ],
        compiler_params=pltpu.CompilerParams(
            dimension_semantics=("arbitrary", "arbitrary", "arbitrary"), vmem_limit_bytes=VMEM_LIMIT),
    )(q, k, v, g, bias_tbl)


def _bias_lookup(rel_bias, dist):
    bucket = _t5_bucket(dist)[..., None]
    edges = jnp.arange(N_BUCKETS)
    onehot = ((bucket >= edges) & (bucket < edges + 1)).astype(F32)
    return jnp.einsum('...b,bh->h...', onehot, rel_bias.astype(F32), precision=lax.Precision.HIGHEST)


def _prompt_bias_table(rel_bias):
    tbls = []
    for w, d in PATTERNS:
        n_chunk = 16 // d
        n = jnp.arange(BLK)
        idx = (n % (BLK // n_chunk)) * n_chunk + n // (BLK // n_chunk)
        i = idx[:, None]
        j = jnp.concatenate([idx, BLK + idx])[None, :]
        rel = i + BLK - j
        band = (rel >= 0) & (rel <= w // d)
        bias = _bias_lookup(rel_bias, jnp.maximum(rel, 0) * d) * LOG2E
        normal = jnp.where(band[None], bias, NEG)
        first = jnp.where((band & (j >= BLK))[None], bias, NEG)
        tbls.append(jnp.stack([normal, first], axis=0))
    t = jnp.stack(tbls, axis=1)
    t = t.reshape(2, len(PATTERNS), N_HEADS // 2, 2 * BLK, 2 * BLK)
    return jnp.moveaxis(t, 2, 0)


def _ssd_kernel(xbc_ref, z_ref, dt_ref, cw_ref, cb_ref, dtb_ref, alog_ref, dsk_ref, nw_ref,
                e_ref, tril_ref, s_ref, h_ref, cbuf):
    @pl.when(pl.program_id(1) == 0)
    def _():
        cbuf[:, 0:8, :] = jnp.zeros((CONV_DIM // LANES, 8, LANES), F32)
        h_ref[...] = jnp.zeros_like(h_ref)

    def body(ci, carry):
        rows = pl.ds(pl.multiple_of(ci * CHUNK, CHUNK), CHUNK)
        _ssd_chunk(xbc_ref.at[rows], z_ref.at[rows], dt_ref.at[rows], cw_ref, cb_ref, dtb_ref, alog_ref, dsk_ref,
                   nw_ref, e_ref, tril_ref, s_ref.at[rows], h_ref, cbuf)
        return carry
    lax.fori_loop(0, SSD_CHUNKS_PER_STEP, body, 0)


def _ssd_chunk(xbc_ref, z_ref, dt_ref, cw_ref, cb_ref, dtb_ref, alog_ref, dsk_ref, nw_ref,
               e_ref, tril_ref, s_ref, h_ref, cbuf):
    n_slab = CONV_DIM // LANES

    xc_slabs = []
    for j in range(n_slab):
        cols = slice(j * LANES, (j + 1) * LANES)
        xj = xbc_ref[:, cols]
        cbuf[j, 8:8 + CHUNK, :] = xj
        acc = cb_ref[:, cols] + xj * cw_ref[CONV_W - 1:CONV_W, cols]
        for i in range(CONV_W - 1):
            tap = cbuf[pl.ds(j, 1, stride=2), pl.ds(8 - (CONV_W - 1) + i, CHUNK), :][0]
            acc = acc + tap * cw_ref[i:i + 1, cols]
        xc_slabs.append(_silu(acc))
        cbuf[j, 0:8, :] = xj[CHUNK - 8:, :]

    n_x = D_SSD // LANES
    xs = jnp.concatenate(xc_slabs[:n_x], axis=1)
    lane = lax.broadcasted_iota(jnp.int32, (CHUNK, LANES), 1)
    sub = lax.broadcasted_iota(jnp.int32, (CHUNK, LANES), 0)
    head_lane = lane < N_HEADS
    dt = jnp.where(head_lane, _softplus(dt_ref[...] + dtb_ref[...]), 0.0)
    la = dt * (-jnp.exp(alog_ref[...]))

    def hi_lo(val):
        hi = val.astype(BF16).astype(F32)
        return (hi + pltpu.roll(val - hi, N_HEADS, axis=1)).astype(BF16)

    cs2 = jnp.dot(tril_ref[...], hi_lo(la), preferred_element_type=F32)
    a_cs = jnp.where(head_lane, cs2 + pltpu.roll(cs2, LANES - N_HEADS, axis=1), 0.0)
    ea = jnp.where(head_lane, jnp.exp(a_cs), 0.0)
    dte = jnp.where(head_lane, jnp.exp(a_cs[CHUNK - 1:CHUNK, :] - a_cs), 0.0)

    expanded = jnp.dot(jnp.concatenate([hi_lo(dt), hi_lo(ea), hi_lo(dte)], axis=0), e_ref[...],
                       preferred_element_type=F32)
    dtx, eax, dtex = expanded[:CHUNK], expanded[CHUNK:2 * CHUNK], expanded[2 * CHUNK:]
    xdt_f = xs * dtx
    xdt = xdt_f.astype(BF16)
    xdte = (xdt_f * dtex).astype(BF16)

    a_cs_t = a_cs.T
    causal = sub >= lane
    is_a = lane < HEAD_DIM
    ys = []
    for g in range(N_GROUPS):
        b_g = xc_slabs[n_x + g]
        c_g = xc_slabs[n_x + N_GROUPS + g]
        b_bf = b_g.astype(BF16)
        c_bf = c_g.astype(BF16)
        cb = lax.dot_general(c_bf, b_bf, (((1,), (1,)), ((), ())), preferred_element_type=F32)
        gcols = slice(g * 512, (g + 1) * 512)
        h_prev = h_ref[:, gcols]
        y_off = jnp.dot(c_bf, h_prev.astype(BF16), preferred_element_type=F32) * eax[:, gcols]
        st = jnp.dot(b_g.T.astype(BF16), xdte[:, gcols], preferred_element_type=F32)
        h_ref[:, gcols] = h_prev * eax[CHUNK - 1:CHUNK, gcols] + st
        for hp in range(4):
            pair = []
            for which in range(2):
                h = g * 8 + hp * 2 + which
                col = jnp.sum(jnp.where(lane == h, a_cs, 0.0), axis=1, keepdims=True)
                seg = col - a_cs_t[h:h + 1, :]
                lmat = jnp.exp(jnp.where(causal, seg, NEG))
                pair.append((cb * lmat).astype(BF16))
            x_pair = xdt[:, g * 512 + hp * LANES:g * 512 + (hp + 1) * LANES]
            y_a = jnp.dot(pair[0], x_pair, preferred_element_type=F32)
            y_b = jnp.dot(pair[1], x_pair, preferred_element_type=F32)
            ys.append(jnp.where(is_a, y_a, y_b) + y_off[:, hp * LANES:(hp + 1) * LANES])
    y = jnp.concatenate(ys, axis=1) + dsk_ref[...] * xs
    zf = z_ref[...].astype(F32)
    yz = y * _silu(zf)
    var = jnp.mean(yz * yz, axis=-1, keepdims=True)
    s_ref[...] = (yz * lax.rsqrt(var + EPS) * nw_ref[...]).astype(s_ref.dtype)


def _ssd_prompt(xbc, z, dt, cw, cb, dtb, alog, dsk, nw, emat, tril, batch, seq):
    tc = CHUNK * SSD_CHUNKS_PER_STEP
    assert seq % tc == 0
    nc = seq // tc
    row = lambda b, c: (b * nc + c, 0)
    const = lambda b, c: (0, 0)
    return pl.pallas_call(
        _ssd_kernel,
        grid=(batch, nc),
        in_specs=[
            pl.BlockSpec((tc, CONV_DIM), row),
            pl.BlockSpec((tc, D_SSD), row),
            pl.BlockSpec((tc, LANES), row),
            pl.BlockSpec((CONV_W, CONV_DIM), const),
            pl.BlockSpec((1, CONV_DIM), const),
            pl.BlockSpec((1, LANES), const),
            pl.BlockSpec((1, LANES), const),
            pl.BlockSpec((1, D_SSD), const),
            pl.BlockSpec((1, D_SSD), const),
            pl.BlockSpec((LANES, D_SSD), const),
            pl.BlockSpec((CHUNK, CHUNK), const),
        ],
        out_specs=[
            pl.BlockSpec((tc, D_SSD), row),
            pl.BlockSpec((None, D_STATE, D_SSD), lambda b, c: (b, 0, 0)),
        ],
        out_shape=(jax.ShapeDtypeStruct((batch * seq, D_SSD), BF16),
                   jax.ShapeDtypeStruct((batch, D_STATE, D_SSD), F32)),
        scratch_shapes=[pltpu.VMEM((CONV_DIM // LANES, CHUNK + 8, LANES), F32)],
        compiler_params=pltpu.CompilerParams(
            dimension_semantics=("arbitrary", "arbitrary"), vmem_limit_bytes=VMEM_LIMIT),
    )(xbc, z, dt, cw, cb, dtb, alog, dsk, nw, emat, tril)


def _attn_sample_kernel(qt_ref, knt_ref, vnt_ref, gt_ref, k_ref, v_ref, btbl_ref, bias0_ref, o_ref):
    hh = pl.program_id(1)

    @pl.when(hh == 0)
    def _():
        o_ref[...] = jnp.zeros_like(o_ref)

    lane = lax.broadcasted_iota(jnp.int32, (HEAD_DIM, LANES), 1)
    lane1 = lax.broadcasted_iota(jnp.int32, (1, LANES), 1)
    qt = qt_ref[...] * (HEAD_DIM ** -0.5)
    n_pat = float(len(PATTERNS))
    for j in range(SAMPLE_HEADS_PER_STEP):
        h = hh * SAMPLE_HEADS_PER_STEP + j
        pick = lane == h

        def col(val, pick=pick):
            return jnp.sum(jnp.where(pick, val, 0.0), axis=1, keepdims=True)

        qc, knc, vnc, gc = col(qt), col(knt_ref[...]), col(vnt_ref[...]), col(gt_ref[...])
        b0 = jnp.sum(jnp.where(lane1 == h, bias0_ref[...], 0.0), axis=1, keepdims=True)
        s0 = jnp.sum(qc * knc, axis=0, keepdims=True) + b0
        s = jnp.sum(k_ref[j] * qc, axis=0, keepdims=True)
        sp = [s + btbl_ref[pi, pl.ds(h, 1), :] for pi in range(len(PATTERNS))]
        m = s0
        for x in sp:
            m = jnp.maximum(m, jnp.max(x, axis=1, keepdims=True))
        p0 = n_pat * jnp.exp(s0 - m)
        pw = jnp.exp(sp[0] - m)
        for x in sp[1:]:
            pw = pw + jnp.exp(x - m)
        l = jnp.sum(pw, axis=1, keepdims=True) + p0
        oc = (jnp.sum(v_ref[j] * pw, axis=1, keepdims=True) + p0 * vnc) / l
        o_ref[...] = jnp.where(pick, oc * _silu(gc), o_ref[...])


def _attn_sample(qt, knt, vnt, gt, cache_k_t, cache_v_t, btbl, bias0):
    b, n_past = cache_k_t.shape[0], cache_k_t.shape[3]
    tok = pl.BlockSpec((None, HEAD_DIM, LANES), lambda i, hh: (i, 0, 0))
    cache = pl.BlockSpec((None, SAMPLE_HEADS_PER_STEP, HEAD_DIM, n_past), lambda i, hh: (i, hh, 0, 0))
    return pl.pallas_call(
        _attn_sample_kernel,
        grid=(b, N_HEADS // SAMPLE_HEADS_PER_STEP),
        in_specs=[tok, tok, tok, tok, cache, cache,
                  pl.BlockSpec((len(PATTERNS), N_HEADS, n_past), lambda i, hh: (0, 0, 0)),
                  pl.BlockSpec((1, LANES), lambda i, hh: (0, 0))],
        out_specs=tok,
        out_shape=jax.ShapeDtypeStruct((b, HEAD_DIM, LANES), F32),
        compiler_params=pltpu.CompilerParams(
            dimension_semantics=("arbitrary", "arbitrary"), vmem_limit_bytes=VMEM_LIMIT),
    )(qt, knt, vnt, gt, cache_k_t, cache_v_t, btbl, bias0)


def _sample_bias_tables(rel_bias, n_past):
    dist = n_past - jnp.arange(n_past)
    bias = _bias_lookup(rel_bias, dist)
    tbls = [jnp.where(((dist % d == 0) & (dist <= w))[None], bias, NEG) for w, d in PATTERNS]
    bias0 = _bias_lookup(rel_bias, jnp.zeros((1,), jnp.int32))
    return jnp.stack(tbls, axis=0), jnp.pad(bias0.reshape(1, N_HEADS), ((0, 0), (0, LANES - N_HEADS)))


def _ssd_sample_kernel(xbc_ref, z_ref, dt_ref, sc_ref, h_ref, cw_ref, cb_ref, dtb_ref, alog_ref, dsk_ref,
                       nw_ref, e_ref, s_ref, conv_out_ref, h_out_ref):
    xnew = xbc_ref[...]
    sc = sc_ref[...]
    acc = cb_ref[...] + xnew * cw_ref[CONV_W - 1:CONV_W, :]
    for i in range(CONV_W - 1):
        acc = acc + sc[i:i + 1, :] * cw_ref[i:i + 1, :]
    xc = _silu(acc)
    conv_out_ref[0:CONV_W - 2, :] = sc[1:CONV_W - 1, :]
    conv_out_ref[CONV_W - 2:CONV_W - 1, :] = xnew

    xs = xc[:, :D_SSD]
    lane1 = lax.broadcasted_iota(jnp.int32, (1, LANES), 1)
    dt = jnp.where(lane1 < N_HEADS, _softplus(dt_ref[...] + dtb_ref[...]), 0.0)
    da = jnp.where(lane1 < N_HEADS, jnp.exp(dt * (-jnp.exp(alog_ref[...]))), 0.0)

    def expand(val):
        v8 = jnp.broadcast_to(val, (8, LANES))
        out = jnp.zeros((8, D_SSD), F32)
        for _ in range(3):
            part = v8.astype(BF16)
            out = out + jnp.dot(part, e_ref[...], preferred_element_type=F32)
            v8 = v8 - part.astype(F32)
        return out[0:1, :]

    xdt = xs * expand(dt)
    dax = expand(da)

    lane = lax.broadcasted_iota(jnp.int32, (HEAD_DIM, LANES), 1)
    sub = lax.broadcasted_iota(jnp.int32, (HEAD_DIM, LANES), 0)
    eye2 = (lane % HEAD_DIM) == sub
    is_a = lane < HEAD_DIM

    def to_cols(row):
        mat = jnp.where(eye2, jnp.broadcast_to(row, (HEAD_DIM, LANES)), 0.0)
        col_a = jnp.sum(jnp.where(is_a, mat, 0.0), axis=1, keepdims=True)
        col_b = jnp.sum(jnp.where(is_a, 0.0, mat), axis=1, keepdims=True)
        return col_a, col_b

    y_rows = []
    for hp in range(N_HEADS // 2):
        g = hp // 4
        b_row = xc[:, D_SSD + g * D_STATE:D_SSD + (g + 1) * D_STATE]
        c_row = xc[:, D_SSD + (N_GROUPS + g) * D_STATE:D_SSD + (N_GROUPS + g + 1) * D_STATE]
        cols = slice(hp * LANES, (hp + 1) * LANES)
        x_cols = to_cols(xdt[:, cols])
        d_cols = to_cols(dax[:, cols])
        y_cols = []
        for which in range(2):
            h = hp * 2 + which
            h_new = h_ref[h] * d_cols[which] + x_cols[which] * b_row
            h_out_ref[h] = h_new
            y_cols.append(jnp.sum(h_new * c_row, axis=1, keepdims=True))
        y_mat = jnp.where(eye2, jnp.where(is_a, y_cols[0], y_cols[1]), 0.0)
        y_rows.append(jnp.sum(y_mat, axis=0, keepdims=True))
    y = jnp.concatenate(y_rows, axis=1) + dsk_ref[...] * xs
    zf = z_ref[...].astype(F32)
    yz = y * _silu(zf)
    var = jnp.mean(yz * yz, axis=-1, keepdims=True)
    s_ref[...] = (yz * lax.rsqrt(var + EPS) * nw_ref[...]).astype(s_ref.dtype)


def _ssd_sample(xbc3, z3, dt3, state_conv, state_ssm, cw, cb, dtb, alog, dsk, nw, emat):
    b = xbc3.shape[0]
    const = lambda i: (0, 0)
    tok = lambda width: pl.BlockSpec((None, 1, width), lambda i: (i, 0, 0))
    conv_spec = pl.BlockSpec((None, CONV_W - 1, CONV_DIM), lambda i: (i, 0, 0))
    ssm_spec = pl.BlockSpec((None, N_HEADS, HEAD_DIM, D_STATE), lambda i: (i, 0, 0, 0))
    return pl.pallas_call(
        _ssd_sample_kernel,
        grid=(b,),
        in_specs=[
            tok(CONV_DIM), tok(D_SSD), tok(LANES), conv_spec, ssm_spec,
            pl.BlockSpec((CONV_W, CONV_DIM), const),
            pl.BlockSpec((1, CONV_DIM), const),
            pl.BlockSpec((1, LANES), const),
            pl.BlockSpec((1, LANES), const),
            pl.BlockSpec((1, D_SSD), const),
            pl.BlockSpec((1, D_SSD), const),
            pl.BlockSpec((LANES, D_SSD), const),
        ],
        out_specs=[tok(D_SSD), conv_spec, ssm_spec],
        out_shape=(jax.ShapeDtypeStruct((b, 1, D_SSD), BF16),
                   jax.ShapeDtypeStruct((b, CONV_W - 1, CONV_DIM), F32),
                   jax.ShapeDtypeStruct((b, N_HEADS, HEAD_DIM, D_STATE), F32)),
        compiler_params=pltpu.CompilerParams(
            dimension_semantics=("arbitrary",), vmem_limit_bytes=VMEM_LIMIT),
    )(xbc3, z3, dt3, state_conv, state_ssm, cw, cb, dtb, alog, dsk, nw, emat)


def kernel(x_prompt, x_sample, cache_win_k, cache_win_v, state_conv, state_ssm, norm_w, w_in, q_norm_w,
           k_norm_w, rel_bias, conv_w, conv_b, dt_bias, a_log, d_skip, ssd_norm_w, w_out):
    assert x_prompt.shape[-1] == D_MODEL and w_in.shape[0] == 1, "single-layer model of width 1024 only"
    batch, seq, _ = x_prompt.shape
    dec_batch, dec_seq, _ = x_sample.shape
    assert dec_seq == 1 and seq % SPAN == 0 and cache_win_k.shape[2] == WINDOW_MAX

    w_pad = jnp.pad(w_in[0], ((0, 0), (0, D_IN_PAD - D_IN_PROJ))).astype(BF16)
    w_out_b = w_out[0].astype(BF16)
    nw = norm_w[0].reshape(1, D_MODEL)
    qnw = jnp.tile(q_norm_w[0], 512 // HEAD_DIM).reshape(1, 512)
    knw = jnp.tile(k_norm_w[0], 512 // HEAD_DIM).reshape(1, 512)
    cw, cb = conv_w[0], conv_b[0].reshape(1, CONV_DIM)
    pad_heads = lambda a: jnp.pad(a.reshape(1, N_HEADS), ((0, 0), (0, LANES - N_HEADS)))
    dtb, alog = pad_heads(dt_bias[0]), pad_heads(a_log[0])
    dsk = jnp.repeat(d_skip[0], HEAD_DIM).reshape(1, D_SSD)
    snw = ssd_norm_w[0].reshape(1, D_SSD)
    erow = jnp.arange(LANES)[:, None]
    emat = ((erow % N_HEADS == (jnp.arange(D_SSD) // HEAD_DIM)[None, :]) & (erow < 2 * N_HEADS)).astype(BF16)
    tril = (jnp.arange(CHUNK)[:, None] >= jnp.arange(CHUNK)[None, :]).astype(BF16)

    xp = x_prompt.reshape(batch * seq, D_MODEL)
    nwin = min(WINDOW_MAX, seq)
    q, k, v, g, z, xbc, dt, k_win, v_win = _inproj(xp, nw, w_pad, qnw, knw, tm=512, window=(seq, nwin))
    a = _attn_prompt(q, k, v, g, _prompt_bias_table(rel_bias), batch, seq)
    s, h_fin = _ssd_prompt(xbc, z, dt, cw, cb, dtb, alog, dsk, snw, emat, tril, batch, seq)
    y_p = _outproj(xp, a, s, w_out_b, tm=512).reshape(batch, seq, D_MODEL)
    heads = lambda t: jnp.transpose(t.reshape(1, batch, N_HEADS, HEAD_DIM, nwin), (0, 1, 4, 2, 3))
    kp, vp = heads(k_win), heads(v_win)
    cp = xbc.reshape(batch, seq, CONV_DIM)[None, :, seq - (CONV_W - 1):]
    hp = jnp.swapaxes(h_fin, 1, 2).reshape(batch, N_HEADS, HEAD_DIM, D_STATE)[None]

    xs2 = x_sample.reshape(dec_batch, D_MODEL)
    qs, ks, vs, gs, zs, xbcs, dts = _inproj(xs2, nw, w_pad, qnw, knw, tm=dec_batch)
    tok_t = lambda t: jnp.pad(jnp.swapaxes(t.astype(F32).reshape(dec_batch, N_HEADS, HEAD_DIM), 1, 2),
                              ((0, 0), (0, 0), (0, LANES - N_HEADS)))
    cache_t = lambda c: jnp.transpose(c[0], (0, 2, 3, 1))
    btbl, bias0 = _sample_bias_tables(rel_bias, cache_win_k.shape[2])
    a_t = _attn_sample(tok_t(qs), tok_t(ks), tok_t(vs), tok_t(gs), cache_t(cache_win_k), cache_t(cache_win_v),
                       btbl, bias0)
    a_s = jnp.swapaxes(a_t[:, :, :N_HEADS], 1, 2)
    s_s, conv_s, h_s = _ssd_sample(xbcs.reshape(dec_batch, 1, CONV_DIM), zs.reshape(dec_batch, 1, D_SSD),
                                   dts.reshape(dec_batch, 1, LANES), state_conv[0], state_ssm[0],
                                   cw, cb, dtb, alog, dsk, snw, emat)
    y_s = _outproj(xs2, a_s.reshape(dec_batch, D_ATTN).astype(BF16), s_s.reshape(dec_batch, D_SSD),
                   w_out_b, tm=dec_batch).reshape(dec_batch, 1, D_MODEL)
    k_s = ks.reshape(1, dec_batch, 1, N_HEADS, HEAD_DIM)
    v_s = vs.reshape(1, dec_batch, 1, N_HEADS, HEAD_DIM)
    return (y_p, y_s, kp, vp, cp, hp, k_s, v_s, conv_s[None], h_s[None])
```

```python
import functools
import math

import jax
import jax.numpy as jnp
from jax import lax
from jax.experimental import pallas as pl
from jax.experimental.pallas import tpu as pltpu

F32 = jnp.float32
BF16 = jnp.bfloat16

D_MODEL = 1024
D_ATTN = 1024
D_SSD = 1024
HEAD_DIM = 64
N_HEADS = 16
PATTERNS = ((128, 1), (512, 4), (2048, 16))
WINDOW_MAX = 2048
BLK = 128
N_BUCKETS = 32
D_STATE = 128
N_GROUPS = 2
CONV_W = 4
CONV_DIM = D_SSD + 2 * N_GROUPS * D_STATE
CHUNK = 128
EPS = 1e-6
D_IN_PROJ = 4 * D_ATTN + D_SSD + CONV_DIM + N_HEADS
LANES = 128
D_IN_PAD = D_IN_PROJ - N_HEADS + LANES
SPAN = BLK * 16
NEG = -1e30
LOG2E = math.log2(math.e)
ATTN_GROUP = 16
SAMPLE_HEADS_PER_STEP = 8
SSD_CHUNKS_PER_STEP = 4
SSD_CHUNKS_UNROLL = 4
VMEM_LIMIT = 56 * 1024 * 1024


def _silu(x):
    h = 0.5 * x
    return h + h * jnp.tanh(h)


def _softplus(x):
    return jnp.maximum(x, 0.0) + jnp.log(1.0 + jnp.exp(-jnp.abs(x)))


def _region(fn):
    def body(i, carry):
        fn()
        return carry
    lax.fori_loop(0, 1 + jnp.minimum(pl.program_id(0), 0), body, 0)


def _t5_bucket(dist):
    max_exact = N_BUCKETS // 2
    d_f = jnp.maximum(dist, 1).astype(F32)
    large = max_exact + (jnp.log(d_f / max_exact) / math.log(WINDOW_MAX / max_exact)
                         * (N_BUCKETS - max_exact)).astype(jnp.int32)
    large = jnp.minimum(large, N_BUCKETS - 1)
    return jnp.where(dist < max_exact, dist, large)


def _inproj_kernel(x_ref, nw_ref, w_ref, qnw_ref, knw_ref,
                   q_ref, k_ref, v_ref, g_ref, z_ref, xbc_ref, dt_ref, kt_ref=None, vt_ref=None):
    x = x_ref[...]
    h = (x * nw_ref[...]).astype(BF16)
    r = lax.rsqrt(jnp.mean(x * x, axis=-1, keepdims=True) + EPS)

    def seg(c0, width):
        return jnp.dot(h, w_ref[:, c0:c0 + width], preferred_element_type=F32) * r

    is_a = lax.broadcasted_iota(jnp.int32, (x.shape[0], LANES), 1) < HEAD_DIM

    def head_rms(pj):
        p2 = pj * pj
        ss_a = jnp.sum(jnp.where(is_a, p2, 0.0), axis=1, keepdims=True)
        ss_b = jnp.sum(jnp.where(is_a, 0.0, p2), axis=1, keepdims=True)
        return jnp.where(is_a, lax.rsqrt(ss_a * (1.0 / HEAD_DIM) + EPS), lax.rsqrt(ss_b * (1.0 / HEAD_DIM) + EPS))

    for out_ref, base, hw_ref in ((q_ref, 0, qnw_ref), (k_ref, D_ATTN, knw_ref)):
        for c in range(2):
            p = seg(base + 512 * c, 512)
            rs = jnp.concatenate([head_rms(p[:, LANES * j:LANES * (j + 1)]) for j in range(512 // LANES)], axis=1)
            normed = p * rs * hw_ref[...]
            out_ref[:, 512 * c:512 * (c + 1)] = normed
            if out_ref is k_ref and kt_ref is not None:
                kt_ref[512 * c:512 * (c + 1), :] = normed.T
    for c in range(2):
        v_c = seg(2 * D_ATTN + 512 * c, 512)
        v_ref[:, 512 * c:512 * (c + 1)] = v_c
        if vt_ref is not None:
            vt_ref[512 * c:512 * (c + 1), :] = v_c.T
        g_ref[:, 512 * c:512 * (c + 1)] = seg(3 * D_ATTN + 512 * c, 512).astype(g_ref.dtype)
        z_ref[:, 512 * c:512 * (c + 1)] = seg(4 * D_ATTN + 512 * c, 512).astype(z_ref.dtype)
    for c in range(3):
        xbc_ref[:, 512 * c:512 * (c + 1)] = seg(5 * D_ATTN + 512 * c, 512)
    dt_ref[...] = seg(5 * D_ATTN + CONV_DIM, LANES)


def _inproj(x2d, nw, w_pad, qnw, knw, tm, window=None):
    t = x2d.shape[0]
    row = lambda i: (i, 0)
    const = lambda i: (0, 0)
    win_shapes, win_specs = (), []
    if window is not None:
        seq, nwin = window
        assert seq % tm == 0 and nwin % tm == 0
        per_seq, first = seq // tm, (seq - nwin) // tm
        win_spec = pl.BlockSpec((None, D_ATTN, tm),
                                lambda i: (i // per_seq, 0, jnp.maximum(i % per_seq - first, 0)))
        win_shapes = (jax.ShapeDtypeStruct((t // seq, D_ATTN, nwin), F32),) * 2
        win_specs = [win_spec, win_spec]
    outs = (
        jax.ShapeDtypeStruct((t, D_ATTN), F32),
        jax.ShapeDtypeStruct((t, D_ATTN), F32),
        jax.ShapeDtypeStruct((t, D_ATTN), F32),
        jax.ShapeDtypeStruct((t, D_ATTN), BF16),
        jax.ShapeDtypeStruct((t, D_SSD), BF16),
        jax.ShapeDtypeStruct((t, CONV_DIM), F32),
        jax.ShapeDtypeStruct((t, LANES), F32),
    )
    return pl.pallas_call(
        _inproj_kernel,
        grid=(t // tm,),
        in_specs=[
            pl.BlockSpec((tm, D_MODEL), row),
            pl.BlockSpec((1, D_MODEL), const),
            pl.BlockSpec((D_MODEL, D_IN_PAD), const, pipeline_mode=pl.Buffered(1)),
            pl.BlockSpec((1, 512), const),
            pl.BlockSpec((1, 512), const),
        ],
        out_specs=[
            pl.BlockSpec((tm, D_ATTN), row),
            pl.BlockSpec((tm, D_ATTN), row),
            pl.BlockSpec((tm, D_ATTN), row),
            pl.BlockSpec((tm, D_ATTN), row),
            pl.BlockSpec((tm, D_SSD), row),
            pl.BlockSpec((tm, CONV_DIM), row),
            pl.BlockSpec((tm, LANES), row),
        ] + win_specs,
        out_shape=outs + win_shapes,
        compiler_params=pltpu.CompilerParams(
            dimension_semantics=("arbitrary",), vmem_limit_bytes=VMEM_LIMIT),
    )(x2d, nw, w_pad, qnw, knw)


def _outproj_kernel(x_ref, a_ref, s_ref, w_ref, y_ref):
    y_ref[...] = (x_ref[...]
                  + jnp.dot(a_ref[...], w_ref[0:D_ATTN, :], preferred_element_type=F32)
                  + jnp.dot(s_ref[...], w_ref[D_ATTN:, :], preferred_element_type=F32))


def _outproj(x2d, a, s, w_out_b, tm):
    t = x2d.shape[0]
    row = lambda i: (i, 0)
    return pl.pallas_call(
        _outproj_kernel,
        grid=(t // tm,),
        in_specs=[
            pl.BlockSpec((tm, D_MODEL), row),
            pl.BlockSpec((tm, D_ATTN), row),
            pl.BlockSpec((tm, D_SSD), row),
            pl.BlockSpec((D_ATTN + D_SSD, D_MODEL), lambda i: (0, 0), pipeline_mode=pl.Buffered(1)),
        ],
        out_specs=pl.BlockSpec((tm, D_MODEL), row),
        out_shape=jax.ShapeDtypeStruct((t, D_MODEL), F32),
        compiler_params=pltpu.CompilerParams(
            dimension_semantics=("arbitrary",), vmem_limit_bytes=VMEM_LIMIT),
    )(x2d, a, s, w_out_b)


def _attn_kernel(q_ref, k_ref, v_ref, g_ref, bias_ref, o_ref, qp, kp, vp, mid, m_s, l_s, acc_s, s_buf, m_buf):
    s_idx = pl.program_id(2)
    first = (s_idx == 0).astype(jnp.int32)
    slot = s_idx % 2
    pslot = 1 - slot
    lane = lax.broadcasted_iota(jnp.int32, (BLK, LANES), 1)
    is_a = lane < HEAD_DIM
    qscale = HEAD_DIM ** -0.5 * LOG2E
    ones = jnp.ones((2 * BLK, LANES), BF16)
    quarter = SPAN // 4

    @pl.when(s_idx == 0)
    def _():
        kp[pslot] = jnp.zeros((SPAN, LANES), F32)
        vp[pslot] = jnp.zeros((SPAN, LANES), F32)

    def regroup(src_ref, store, scale=None):
        for lo in range(4):
            mid[lo] = src_ref[pl.ds(lo, quarter, stride=4), :]
        for lo in range(4):
            for hi in range(4):
                val = mid[lo, pl.ds(hi, BLK, stride=4), :]
                store(4 * hi + lo, val if scale is None else val * scale)

    def store_q(r, val):
        qp[pl.ds(r * BLK, BLK), :] = val

    def store_k(r, val):
        kp[slot, pl.ds(r * BLK, BLK), :] = val

    def store_v(r, val):
        vp[slot, pl.ds(r * BLK, BLK), :] = val

    regroup(q_ref, store_q, qscale)
    regroup(k_ref, store_k)

    def chunks(d, blk):
        n_chunk = 16 // d
        length = BLK // n_chunk
        u, r = blk // d, blk % d
        is_u0 = u == 0
        cur = [pl.multiple_of((d * c + r) * BLK + u * length, 8) for c in range(n_chunk)]
        back = jnp.where(is_u0, BLK - length, (u - 1) * length)
        prev = [pl.multiple_of((d * c + r) * BLK + back, 8) for c in range(n_chunk)]
        return is_u0, length, cur, jnp.where(is_u0, pslot, slot), prev

    def gather(read, starts, length):
        return jnp.concatenate([read(pl.ds(st, length)) for st in starts], axis=0)

    def scatter(write, starts, length, val):
        for c, st in enumerate(starts):
            write(pl.ds(st, length), val[c * length:(c + 1) * length])

    def qk_stage(pi, d, grp, sl):
        for j in range(ATTN_GROUP):
            is_u0, length, cur, prev_slot, prev = chunks(d, grp * ATTN_GROUP + j)
            qsc = gather(lambda rows: qp[rows, :], cur, length)
            qq = jnp.concatenate([jnp.where(is_a, qsc, 0.0), jnp.where(is_a, 0.0, qsc)], axis=0).astype(BF16)
            kk = jnp.concatenate([gather(lambda rows: kp[prev_slot, rows, :], prev, length),
                                  gather(lambda rows: kp[slot, rows, :], cur, length)], axis=0).astype(BF16)
            bias = bias_ref[jnp.where(is_u0, first, 0), pi]
            s = lax.dot_general(qq, kk, (((1,), (1,)), ((), ())), preferred_element_type=F32) + bias
            s_buf[sl, j] = s
            m_buf[sl, j] = jnp.broadcast_to(jnp.max(s, axis=1, keepdims=True), (2 * BLK, LANES))

    def pv_stage(pi, d, grp, sl):
        for j in range(ATTN_GROUP):
            _, length, cur, prev_slot, prev = chunks(d, grp * ATTN_GROUP + j)
            m = m_buf[sl, j]
            p = jnp.exp2(s_buf[sl, j] - jnp.concatenate([m, m], axis=1)).astype(BF16)
            vv = jnp.concatenate([gather(lambda rows: vp[prev_slot, rows, :], prev, length),
                                  gather(lambda rows: vp[slot, rows, :], cur, length)], axis=0).astype(BF16)
            ol = jnp.dot(p, jnp.concatenate([vv, ones], axis=1), preferred_element_type=F32)
            for ref, val in ((acc_s, jnp.where(is_a, ol[:BLK, :LANES], ol[BLK:, :LANES])),
                             (l_s, jnp.where(is_a, ol[:BLK, LANES:], ol[BLK:, LANES:])),
                             (m_s, jnp.where(is_a, m[:BLK], m[BLK:]))):
                scatter(lambda rows, v, ref=ref: ref.__setitem__((pi, rows, slice(None)), v), cur, length, val)

    groups = [(pi, d, jnp.int32(grp)) for pi, (_, d) in enumerate(PATTERNS)
              for grp in range(SPAN // BLK // ATTN_GROUP)]

    def first_stage():
        qk_stage(*groups[0], 0)
        regroup(v_ref, store_v)

    _region(first_stage)
    for i, grp in enumerate(groups):
        def step(i=i, grp=grp):
            pv_stage(*grp, i % 2)
            if i + 1 < len(groups):
                qk_stage(*groups[i + 1], (i + 1) % 2)
        _region(step)

    for lo in range(4):
        for hi in range(4):
            rows = pl.ds((4 * hi + lo) * BLK, BLK)
            m = jnp.maximum(jnp.maximum(m_s[0, rows, :], m_s[1, rows, :]), m_s[2, rows, :])
            l = jnp.zeros((BLK, LANES), F32)
            acc = jnp.zeros((BLK, LANES), F32)
            for pi in range(len(PATTERNS)):
                e = jnp.exp2(m_s[pi, rows, :] - m)
                l = l + l_s[pi, rows, :] * e
                acc = acc + acc_s[pi, rows, :] * e
            mid[lo, pl.ds(hi, BLK, stride=4), :] = acc / l
    for lo in range(4):
        qp[pl.ds(lo, quarter, stride=4), :] = mid[lo]
    g = g_ref[...].astype(F32)
    o_ref[...] = (qp[...] * _silu(g)).astype(o_ref.dtype)


def _attn_prompt(q, k, v, g, bias_tbl, batch, seq):
    n_span = seq // SPAN
    n_hp = N_HEADS // 2
    cur = lambda hp, b, s: (b * n_span + s, hp)
    blk = (SPAN, LANES)
    return pl.pallas_call(
        _attn_kernel,
        grid=(n_hp, batch, n_span),
        in_specs=[
            pl.BlockSpec(blk, cur),
            pl.BlockSpec(blk, cur),
            pl.BlockSpec(blk, cur),
            pl.BlockSpec(blk, cur),
            pl.BlockSpec((None, 2, len(PATTERNS), 2 * BLK, 2 * BLK), lambda hp, b, s: (hp, 0, 0, 0, 0)),
        ],
        out_specs=pl.BlockSpec(blk, cur),
        out_shape=jax.ShapeDtypeStruct((batch * seq, D_ATTN), BF16),
        scratch_shapes=[
            pltpu.VMEM(blk, F32),
            pltpu.VMEM((2,) + blk, F32),
            pltpu.VMEM((2,) + blk, F32),
            pltpu.VMEM((4, SPAN // 4, LANES), F32),
            pltpu.VMEM((len(PATTERNS),) + blk, F32),
            pltpu.VMEM((len(PATTERNS),) + blk, F32),
            pltpu.VMEM((len(PATTERNS),) + blk, F32),
            pltpu.VMEM((2, ATTN_GROUP, 2 * BLK, 2 * BLK), F32),
            pltpu.VMEM((2, ATTN_GROUP, 2 * BLK, LANES), F32),
        ],
        compiler_params=pltpu.CompilerParams(
            dimension_semantics=("arbitrary", "arbitrary", "arbitrary"), vmem_limit_bytes=VMEM_LIMIT),
    )(q, k, v, g, bias_tbl)


def _bias_lookup(rel_bias, dist):
    bucket = _t5_bucket(dist)[..., None]
    edges = jnp.arange(N_BUCKETS)
    onehot = ((bucket >= edges) & (bucket < edges + 1)).astype(F32)
    return jnp.einsum('...b,bh->h...', onehot, rel_bias.astype(F32), precision=lax.Precision.HIGHEST)


def _prompt_bias_table(rel_bias):
    tbls = []
    for w, d in PATTERNS:
        n_chunk = 16 // d
        n = jnp.arange(BLK)
        idx = (n % (BLK // n_chunk)) * n_chunk + n // (BLK // n_chunk)
        i = idx[:, None]
        j = jnp.concatenate([idx, BLK + idx])[None, :]
        rel = i + BLK - j
        band = (rel >= 0) & (rel <= w // d)
        bias = _bias_lookup(rel_bias, jnp.maximum(rel, 0) * d) * LOG2E
        normal = jnp.where(band[None], bias, NEG)
        first = jnp.where((band & (j >= BLK))[None], bias, NEG)
        tbls.append(jnp.stack([normal, first], axis=0))
    t = jnp.stack(tbls, axis=1)
    t = t.reshape(2, len(PATTERNS), N_HEADS // 2, 2 * BLK, 2 * BLK)
    return jnp.moveaxis(t, 2, 0)


def _ssd_kernel(xbc_ref, z_ref, dt_ref, cw_ref, cb_ref, dtb_ref, alog_ref, dsk_ref, nw_ref,
                e_ref, tril_ref, s_ref, h_ref, cbuf):
    @pl.when(pl.program_id(1) == 0)
    def _():
        cbuf[:, 0:8, :] = jnp.zeros((CONV_DIM // LANES, 8, LANES), F32)
        h_ref[...] = jnp.zeros_like(h_ref)

    def body(ci, carry):
        for k in range(SSD_CHUNKS_UNROLL):
            rows = pl.ds(pl.multiple_of((ci * SSD_CHUNKS_UNROLL + k) * CHUNK, CHUNK), CHUNK)
            _ssd_chunk(xbc_ref.at[rows], z_ref.at[rows], dt_ref.at[rows], cw_ref, cb_ref, dtb_ref, alog_ref,
                       dsk_ref, nw_ref, e_ref, tril_ref, s_ref.at[rows], h_ref, cbuf)
        return carry
    lax.fori_loop(0, SSD_CHUNKS_PER_STEP // SSD_CHUNKS_UNROLL, body, 0)


def _ssd_chunk(xbc_ref, z_ref, dt_ref, cw_ref, cb_ref, dtb_ref, alog_ref, dsk_ref, nw_ref,
               e_ref, tril_ref, s_ref, h_ref, cbuf):
    n_slab = CONV_DIM // LANES

    xc_slabs = []
    for j in range(n_slab):
        cols = slice(j * LANES, (j + 1) * LANES)
        xj = xbc_ref[:, cols]
        cbuf[j, 8:8 + CHUNK, :] = xj
        acc = cb_ref[:, cols] + xj * cw_ref[CONV_W - 1:CONV_W, cols]
        for i in range(CONV_W - 1):
            tap = cbuf[pl.ds(j, 1, stride=2), pl.ds(8 - (CONV_W - 1) + i, CHUNK), :][0]
            acc = acc + tap * cw_ref[i:i + 1, cols]
        xc_slabs.append(_silu(acc))
        cbuf[j, 0:8, :] = xj[CHUNK - 8:, :]

    n_x = D_SSD // LANES
    xs = jnp.concatenate(xc_slabs[:n_x], axis=1)
    lane = lax.broadcasted_iota(jnp.int32, (CHUNK, LANES), 1)
    sub = lax.broadcasted_iota(jnp.int32, (CHUNK, LANES), 0)
    head_lane = lane < N_HEADS
    dt = jnp.where(head_lane, _softplus(dt_ref[...] + dtb_ref[...]), 0.0)
    la = dt * (-jnp.exp(alog_ref[...]))

    def hi_lo(val):
        hi = val.astype(BF16).astype(F32)
        return (hi + pltpu.roll(val - hi, N_HEADS, axis=1)).astype(BF16)

    cs2 = jnp.dot(tril_ref[...], hi_lo(la), preferred_element_type=F32)
    a_cs = jnp.where(head_lane, cs2 + pltpu.roll(cs2, LANES - N_HEADS, axis=1), 0.0)
    ea = jnp.where(head_lane, jnp.exp(a_cs), 0.0)
    dte = jnp.where(head_lane, jnp.exp(a_cs[CHUNK - 1:CHUNK, :] - a_cs), 0.0)

    expanded = jnp.dot(jnp.concatenate([hi_lo(dt), hi_lo(ea), hi_lo(dte)], axis=0), e_ref[...],
                       preferred_element_type=F32)
    dtx, eax, dtex = expanded[:CHUNK], expanded[CHUNK:2 * CHUNK], expanded[2 * CHUNK:]
    xdt_f = xs * dtx
    xdt = xdt_f.astype(BF16)
    xdte = (xdt_f * dtex).astype(BF16)

    a_cs_t = a_cs.T
    causal = sub >= lane
    is_a = lane < HEAD_DIM
    ys = []
    for g in range(N_GROUPS):
        b_g = xc_slabs[n_x + g]
        c_g = xc_slabs[n_x + N_GROUPS + g]
        b_bf = b_g.astype(BF16)
        c_bf = c_g.astype(BF16)
        cb = lax.dot_general(c_bf, b_bf, (((1,), (1,)), ((), ())), preferred_element_type=F32)
        gcols = slice(g * 512, (g + 1) * 512)
        h_prev = h_ref[:, gcols]
        y_off = jnp.dot(c_bf, h_prev.astype(BF16), preferred_element_type=F32) * eax[:, gcols]
        st = jnp.dot(b_g.T.astype(BF16), xdte[:, gcols], preferred_element_type=F32)
        h_ref[:, gcols] = h_prev * eax[CHUNK - 1:CHUNK, gcols] + st
        for hp in range(4):
            pair = []
            for which in range(2):
                h = g * 8 + hp * 2 + which
                col = jnp.sum(jnp.where(lane == h, a_cs, 0.0), axis=1, keepdims=True)
                seg = col - a_cs_t[h:h + 1, :]
                lmat = jnp.exp(jnp.where(causal, seg, NEG))
                pair.append((cb * lmat).astype(BF16))
            x_pair = xdt[:, g * 512 + hp * LANES:g * 512 + (hp + 1) * LANES]
            y_a = jnp.dot(pair[0], x_pair, preferred_element_type=F32)
            y_b = jnp.dot(pair[1], x_pair, preferred_element_type=F32)
            ys.append(jnp.where(is_a, y_a, y_b) + y_off[:, hp * LANES:(hp + 1) * LANES])
    y = jnp.concatenate(ys, axis=1) + dsk_ref[...] * xs
    zf = z_ref[...].astype(F32)
    yz = y * _silu(zf)
    var = jnp.mean(yz * yz, axis=-1, keepdims=True)
    s_ref[...] = (yz * lax.rsqrt(var + EPS) * nw_ref[...]).astype(s_ref.dtype)


def _ssd_prompt(xbc, z, dt, cw, cb, dtb, alog, dsk, nw, emat, tril, batch, seq):
    tc = CHUNK * SSD_CHUNKS_PER_STEP
    assert seq % tc == 0
    nc = seq // tc
    row = lambda b, c: (b * nc + c, 0)
    const = lambda b, c: (0, 0)
    return pl.pallas_call(
        _ssd_kernel,
        grid=(batch, nc),
        in_specs=[
            pl.BlockSpec((tc, CONV_DIM), row),
            pl.BlockSpec((tc, D_SSD), row),
            pl.BlockSpec((tc, LANES), row),
            pl.BlockSpec((CONV_W, CONV_DIM), const),
            pl.BlockSpec((1, CONV_DIM), const),
            pl.BlockSpec((1, LANES), const),
            pl.BlockSpec((1, LANES), const),
            pl.BlockSpec((1, D_SSD), const),
            pl.BlockSpec((1, D_SSD), const),
            pl.BlockSpec((LANES, D_SSD), const),
            pl.BlockSpec((CHUNK, CHUNK), const),
        ],
        out_specs=[
            pl.BlockSpec((tc, D_SSD), row),
            pl.BlockSpec((None, D_STATE, D_SSD), lambda b, c: (b, 0, 0)),
        ],
        out_shape=(jax.ShapeDtypeStruct((batch * seq, D_SSD), BF16),
                   jax.ShapeDtypeStruct((batch, D_STATE, D_SSD), F32)),
        scratch_shapes=[pltpu.VMEM((CONV_DIM // LANES, CHUNK + 8, LANES), F32)],
        compiler_params=pltpu.CompilerParams(
            dimension_semantics=("arbitrary", "arbitrary"), vmem_limit_bytes=VMEM_LIMIT),
    )(xbc, z, dt, cw, cb, dtb, alog, dsk, nw, emat, tril)


def _attn_sample_kernel(qt_ref, knt_ref, vnt_ref, gt_ref, k_ref, v_ref, btbl_ref, bias0_ref, o_ref):
    hh = pl.program_id(1)

    @pl.when(hh == 0)
    def _():
        o_ref[...] = jnp.zeros_like(o_ref)

    lane = lax.broadcasted_iota(jnp.int32, (HEAD_DIM, LANES), 1)
    lane1 = lax.broadcasted_iota(jnp.int32, (1, LANES), 1)
    qt = qt_ref[...] * (HEAD_DIM ** -0.5)
    n_pat = float(len(PATTERNS))
    for j in range(SAMPLE_HEADS_PER_STEP):
        h = hh * SAMPLE_HEADS_PER_STEP + j
        pick = lane == h

        def col(val, pick=pick):
            return jnp.sum(jnp.where(pick, val, 0.0), axis=1, keepdims=True)

        qc, knc, vnc, gc = col(qt), col(knt_ref[...]), col(vnt_ref[...]), col(gt_ref[...])
        b0 = jnp.sum(jnp.where(lane1 == h, bias0_ref[...], 0.0), axis=1, keepdims=True)
        s0 = jnp.sum(qc * knc, axis=0, keepdims=True) + b0
        s = jnp.sum(k_ref[j] * qc, axis=0, keepdims=True)
        sp = [s + btbl_ref[pi, pl.ds(h, 1), :] for pi in range(len(PATTERNS))]
        m = s0
        for x in sp:
            m = jnp.maximum(m, jnp.max(x, axis=1, keepdims=True))
        p0 = n_pat * jnp.exp(s0 - m)
        pw = jnp.exp(sp[0] - m)
        for x in sp[1:]:
            pw = pw + jnp.exp(x - m)
        l = jnp.sum(pw, axis=1, keepdims=True) + p0
        oc = (jnp.sum(v_ref[j] * pw, axis=1, keepdims=True) + p0 * vnc) / l
        o_ref[...] = jnp.where(pick, oc * _silu(gc), o_ref[...])


def _attn_sample(qt, knt, vnt, gt, cache_k_t, cache_v_t, btbl, bias0):
    b, n_past = cache_k_t.shape[0], cache_k_t.shape[3]
    tok = pl.BlockSpec((None, HEAD_DIM, LANES), lambda i, hh: (i, 0, 0))
    cache = pl.BlockSpec((None, SAMPLE_HEADS_PER_STEP, HEAD_DIM, n_past), lambda i, hh: (i, hh, 0, 0))
    return pl.pallas_call(
        _attn_sample_kernel,
        grid=(b, N_HEADS // SAMPLE_HEADS_PER_STEP),
        in_specs=[tok, tok, tok, tok, cache, cache,
                  pl.BlockSpec((len(PATTERNS), N_HEADS, n_past), lambda i, hh: (0, 0, 0)),
                  pl.BlockSpec((1, LANES), lambda i, hh: (0, 0))],
        out_specs=tok,
        out_shape=jax.ShapeDtypeStruct((b, HEAD_DIM, LANES), F32),
        compiler_params=pltpu.CompilerParams(
            dimension_semantics=("arbitrary", "arbitrary"), vmem_limit_bytes=VMEM_LIMIT),
    )(qt, knt, vnt, gt, cache_k_t, cache_v_t, btbl, bias0)


def _sample_bias_tables(rel_bias, n_past):
    dist = n_past - jnp.arange(n_past)
    bias = _bias_lookup(rel_bias, dist)
    tbls = [jnp.where(((dist % d == 0) & (dist <= w))[None], bias, NEG) for w, d in PATTERNS]
    bias0 = _bias_lookup(rel_bias, jnp.zeros((1,), jnp.int32))
    return jnp.stack(tbls, axis=0), jnp.pad(bias0.reshape(1, N_HEADS), ((0, 0), (0, LANES - N_HEADS)))


def _ssd_sample_kernel(xbc_ref, z_ref, dt_ref, sc_ref, h_ref, cw_ref, cb_ref, dtb_ref, alog_ref, dsk_ref,
                       nw_ref, e_ref, s_ref, conv_out_ref, h_out_ref):
    xnew = xbc_ref[...]
    sc = sc_ref[...]
    acc = cb_ref[...] + xnew * cw_ref[CONV_W - 1:CONV_W, :]
    for i in range(CONV_W - 1):
        acc = acc + sc[i:i + 1, :] * cw_ref[i:i + 1, :]
    xc = _silu(acc)
    conv_out_ref[0:CONV_W - 2, :] = sc[1:CONV_W - 1, :]
    conv_out_ref[CONV_W - 2:CONV_W - 1, :] = xnew

    xs = xc[:, :D_SSD]
    lane1 = lax.broadcasted_iota(jnp.int32, (1, LANES), 1)
    dt = jnp.where(lane1 < N_HEADS, _softplus(dt_ref[...] + dtb_ref[...]), 0.0)
    da = jnp.where(lane1 < N_HEADS, jnp.exp(dt * (-jnp.exp(alog_ref[...]))), 0.0)

    def expand(val):
        v8 = jnp.broadcast_to(val, (8, LANES))
        out = jnp.zeros((8, D_SSD), F32)
        for _ in range(3):
            part = v8.astype(BF16)
            out = out + jnp.dot(part, e_ref[...], preferred_element_type=F32)
            v8 = v8 - part.astype(F32)
        return out[0:1, :]

    xdt = xs * expand(dt)
    dax = expand(da)

    lane = lax.broadcasted_iota(jnp.int32, (HEAD_DIM, LANES), 1)
    sub = lax.broadcasted_iota(jnp.int32, (HEAD_DIM, LANES), 0)
    eye2 = (lane % HEAD_DIM) == sub
    is_a = lane < HEAD_DIM

    def to_cols(row):
        mat = jnp.where(eye2, jnp.broadcast_to(row, (HEAD_DIM, LANES)), 0.0)
        col_a = jnp.sum(jnp.where(is_a, mat, 0.0), axis=1, keepdims=True)
        col_b = jnp.sum(jnp.where(is_a, 0.0, mat), axis=1, keepdims=True)
        return col_a, col_b

    y_rows = []
    for hp in range(N_HEADS // 2):
        g = hp // 4
        b_row = xc[:, D_SSD + g * D_STATE:D_SSD + (g + 1) * D_STATE]
        c_row = xc[:, D_SSD + (N_GROUPS + g) * D_STATE:D_SSD + (N_GROUPS + g + 1) * D_STATE]
        cols = slice(hp * LANES, (hp + 1) * LANES)
        x_cols = to_cols(xdt[:, cols])
        d_cols = to_cols(dax[:, cols])
        y_cols = []
        for which in range(2):
            h = hp * 2 + which
            h_new = h_ref[h] * d_cols[which] + x_cols[which] * b_row
            h_out_ref[h] = h_new
            y_cols.append(jnp.sum(h_new * c_row, axis=1, keepdims=True))
        y_mat = jnp.where(eye2, jnp.where(is_a, y_cols[0], y_cols[1]), 0.0)
        y_rows.append(jnp.sum(y_mat, axis=0, keepdims=True))
    y = jnp.concatenate(y_rows, axis=1) + dsk_ref[...] * xs
    zf = z_ref[...].astype(F32)
    yz = y * _silu(zf)
    var = jnp.mean(yz * yz, axis=-1, keepdims=True)
    s_ref[...] = (yz * lax.rsqrt(var + EPS) * nw_ref[...]).astype(s_ref.dtype)


def _ssd_sample(xbc3, z3, dt3, state_conv, state_ssm, cw, cb, dtb, alog, dsk, nw, emat):
    b = xbc3.shape[0]
    const = lambda i: (0, 0)
    tok = lambda width: pl.BlockSpec((None, 1, width), lambda i: (i, 0, 0))
    conv_spec = pl.BlockSpec((None, CONV_W - 1, CONV_DIM), lambda i: (i, 0, 0))
    ssm_spec = pl.BlockSpec((None, N_HEADS, HEAD_DIM, D_STATE), lambda i: (i, 0, 0, 0))
    return pl.pallas_call(
        _ssd_sample_kernel,
        grid=(b,),
        in_specs=[
            tok(CONV_DIM), tok(D_SSD), tok(LANES), conv_spec, ssm_spec,
            pl.BlockSpec((CONV_W, CONV_DIM), const),
            pl.BlockSpec((1, CONV_DIM), const),
            pl.BlockSpec((1, LANES), const),
            pl.BlockSpec((1, LANES), const),
            pl.BlockSpec((1, D_SSD), const),
            pl.BlockSpec((1, D_SSD), const),
            pl.BlockSpec((LANES, D_SSD), const),
        ],
        out_specs=[tok(D_SSD), conv_spec, ssm_spec],
        out_shape=(jax.ShapeDtypeStruct((b, 1, D_SSD), BF16),
                   jax.ShapeDtypeStruct((b, CONV_W - 1, CONV_DIM), F32),
                   jax.ShapeDtypeStruct((b, N_HEADS, HEAD_DIM, D_STATE), F32)),
        compiler_params=pltpu.CompilerParams(
            dimension_semantics=("arbitrary",), vmem_limit_bytes=VMEM_LIMIT),
    )(xbc3, z3, dt3, state_conv, state_ssm, cw, cb, dtb, alog, dsk, nw, emat)


def kernel(x_prompt, x_sample, cache_win_k, cache_win_v, state_conv, state_ssm, norm_w, w_in, q_norm_w,
           k_norm_w, rel_bias, conv_w, conv_b, dt_bias, a_log, d_skip, ssd_norm_w, w_out):
    assert x_prompt.shape[-1] == D_MODEL and w_in.shape[0] == 1, "single-layer model of width 1024 only"
    batch, seq, _ = x_prompt.shape
    dec_batch, dec_seq, _ = x_sample.shape
    assert dec_seq == 1 and seq % SPAN == 0 and cache_win_k.shape[2] == WINDOW_MAX

    w_pad = jnp.pad(w_in[0], ((0, 0), (0, D_IN_PAD - D_IN_PROJ))).astype(BF16)
    w_out_b = w_out[0].astype(BF16)
    nw = norm_w[0].reshape(1, D_MODEL)
    qnw = jnp.tile(q_norm_w[0], 512 // HEAD_DIM).reshape(1, 512)
    knw = jnp.tile(k_norm_w[0], 512 // HEAD_DIM).reshape(1, 512)
    cw, cb = conv_w[0], conv_b[0].reshape(1, CONV_DIM)
    pad_heads = lambda a: jnp.pad(a.reshape(1, N_HEADS), ((0, 0), (0, LANES - N_HEADS)))
    dtb, alog = pad_heads(dt_bias[0]), pad_heads(a_log[0])
    dsk = jnp.repeat(d_skip[0], HEAD_DIM).reshape(1, D_SSD)
    snw = ssd_norm_w[0].reshape(1, D_SSD)
    erow = jnp.arange(LANES)[:, None]
    emat = ((erow % N_HEADS == (jnp.arange(D_SSD) // HEAD_DIM)[None, :]) & (erow < 2 * N_HEADS)).astype(BF16)
    tril = (jnp.arange(CHUNK)[:, None] >= jnp.arange(CHUNK)[None, :]).astype(BF16)

    xp = x_prompt.reshape(batch * seq, D_MODEL)
    nwin = min(WINDOW_MAX, seq)
    q, k, v, g, z, xbc, dt, k_win, v_win = _inproj(xp, nw, w_pad, qnw, knw, tm=512, window=(seq, nwin))
    a = _attn_prompt(q, k, v, g, _prompt_bias_table(rel_bias), batch, seq)
    s, h_fin = _ssd_prompt(xbc, z, dt, cw, cb, dtb, alog, dsk, snw, emat, tril, batch, seq)
    y_p = _outproj(xp, a, s, w_out_b, tm=1024).reshape(batch, seq, D_MODEL)
    heads = lambda t: jnp.transpose(t.reshape(1, batch, N_HEADS, HEAD_DIM, nwin), (0, 1, 4, 2, 3))
    kp, vp = heads(k_win), heads(v_win)
    cp = xbc.reshape(batch, seq, CONV_DIM)[None, :, seq - (CONV_W - 1):]
    hp = jnp.swapaxes(h_fin, 1, 2).reshape(batch, N_HEADS, HEAD_DIM, D_STATE)[None]

    xs2 = x_sample.reshape(dec_batch, D_MODEL)
    qs, ks, vs, gs, zs, xbcs, dts = _inproj(xs2, nw, w_pad, qnw, knw, tm=dec_batch)
    tok_t = lambda t: jnp.pad(jnp.swapaxes(t.astype(F32).reshape(dec_batch, N_HEADS, HEAD_DIM), 1, 2),
                              ((0, 0), (0, 0), (0, LANES - N_HEADS)))
    cache_t = lambda c: jnp.transpose(c[0], (0, 2, 3, 1))
    btbl, bias0 = _sample_bias_tables(rel_bias, cache_win_k.shape[2])
    a_t = _attn_sample(tok_t(qs), tok_t(ks), tok_t(vs), tok_t(gs), cache_t(cache_win_k), cache_t(cache_win_v),
                       btbl, bias0)
    a_s = jnp.swapaxes(a_t[:, :, :N_HEADS], 1, 2)
    s_s, conv_s, h_s = _ssd_sample(xbcs.reshape(dec_batch, 1, CONV_DIM), zs.reshape(dec_batch, 1, D_SSD),
                                   dts.reshape(dec_batch, 1, LANES), state_conv[0], state_ssm[0],
                                   cw, cb, dtb, alog, dsk, snw, emat)
    y_s = _outproj(xs2, a_s.reshape(dec_batch, D_ATTN).astype(BF16), s_s.reshape(dec_batch, D_SSD),
                   w_out_b, tm=dec_batch).reshape(dec_batch, 1, D_MODEL)
    k_s = ks.reshape(1, dec_batch, 1, N_HEADS, HEAD_DIM)
    v_s = vs.reshape(1, dec_batch, 1, N_HEADS, HEAD_DIM)
    return (y_p, y_s, kp, vp, cp, hp, k_s, v_s, conv_s[None], h_s[None])
```

```python
import functools
import math

import jax
import jax.numpy as jnp
from jax import lax
from jax.experimental import pallas as pl
from jax.experimental.pallas import tpu as pltpu

F32 = jnp.float32
BF16 = jnp.bfloat16

D_MODEL = 1024
D_ATTN = 1024
D_SSD = 1024
HEAD_DIM = 64
N_HEADS = 16
PATTERNS = ((128, 1), (512, 4), (2048, 16))
WINDOW_MAX = 2048
BLK = 128
N_BUCKETS = 32
D_STATE = 128
N_GROUPS = 2
CONV_W = 4
CONV_DIM = D_SSD + 2 * N_GROUPS * D_STATE
CHUNK = 128
EPS = 1e-6
D_IN_PROJ = 4 * D_ATTN + D_SSD + CONV_DIM + N_HEADS
LANES = 128
D_IN_PAD = D_IN_PROJ - N_HEADS + LANES
SPAN = BLK * 16
NEG = -1e30
LOG2E = math.log2(math.e)
ATTN_GROUP = 16
SSD_CHUNKS_PER_STEP = 4
SSD_CHUNKS_UNROLL = 4
VMEM_LIMIT = 56 * 1024 * 1024


def _silu(x):
    h = 0.5 * x
    return h + h * jnp.tanh(h)


def _softplus(x):
    return jnp.maximum(x, 0.0) + jnp.log(1.0 + jnp.exp(-jnp.abs(x)))


def _region(fn):
    def body(i, carry):
        fn()
        return carry
    lax.fori_loop(0, 1 + jnp.minimum(pl.program_id(0), 0), body, 0)


def _t5_bucket(dist):
    max_exact = N_BUCKETS // 2
    d_f = jnp.maximum(dist, 1).astype(F32)
    large = max_exact + (jnp.log(d_f / max_exact) / math.log(WINDOW_MAX / max_exact)
                         * (N_BUCKETS - max_exact)).astype(jnp.int32)
    large = jnp.minimum(large, N_BUCKETS - 1)
    return jnp.where(dist < max_exact, dist, large)


def _inproj_kernel(*refs, has_window, sample_heads):
    x_ref, nw_ref, w_ref, qnw_ref, knw_ref = refs[:5]
    n_in = 5 + (8 if sample_heads else 0)
    q_ref, k_ref, v_ref, g_ref, z_ref, xbc_ref, dt_ref = refs[n_in:n_in + 7]
    kt_ref, vt_ref = refs[n_in + 7:n_in + 9] if has_window else (None, None)
    if sample_heads:
        per_seq = N_HEADS // sample_heads
        _attn_sample_body(pl.program_id(0) % per_seq, sample_heads, *refs[5:n_in], refs[-1])
    x = x_ref[...]
    h = (x * nw_ref[...]).astype(BF16)
    r = lax.rsqrt(jnp.mean(x * x, axis=-1, keepdims=True) + EPS)

    def seg(c0, width):
        return jnp.dot(h, w_ref[:, c0:c0 + width], preferred_element_type=F32) * r

    is_a = lax.broadcasted_iota(jnp.int32, (x.shape[0], LANES), 1) < HEAD_DIM

    def head_rms(pj):
        p2 = pj * pj
        ss_a = jnp.sum(jnp.where(is_a, p2, 0.0), axis=1, keepdims=True)
        ss_b = jnp.sum(jnp.where(is_a, 0.0, p2), axis=1, keepdims=True)
        return jnp.where(is_a, lax.rsqrt(ss_a * (1.0 / HEAD_DIM) + EPS), lax.rsqrt(ss_b * (1.0 / HEAD_DIM) + EPS))

    for out_ref, base, hw_ref in ((q_ref, 0, qnw_ref), (k_ref, D_ATTN, knw_ref)):
        for c in range(2):
            p = seg(base + 512 * c, 512)
            rs = jnp.concatenate([head_rms(p[:, LANES * j:LANES * (j + 1)]) for j in range(512 // LANES)], axis=1)
            normed = p * rs * hw_ref[...]
            out_ref[:, 512 * c:512 * (c + 1)] = normed
            if out_ref is k_ref and kt_ref is not None:
                kt_ref[512 * c:512 * (c + 1), :] = normed.T
    for c in range(2):
        v_c = seg(2 * D_ATTN + 512 * c, 512)
        v_ref[:, 512 * c:512 * (c + 1)] = v_c
        if vt_ref is not None:
            vt_ref[512 * c:512 * (c + 1), :] = v_c.T
        g_ref[:, 512 * c:512 * (c + 1)] = seg(3 * D_ATTN + 512 * c, 512).astype(g_ref.dtype)
        z_ref[:, 512 * c:512 * (c + 1)] = seg(4 * D_ATTN + 512 * c, 512).astype(z_ref.dtype)
    for c in range(3):
        xbc_ref[:, 512 * c:512 * (c + 1)] = seg(5 * D_ATTN + 512 * c, 512)
    dt_ref[...] = seg(5 * D_ATTN + CONV_DIM, LANES)


def _inproj(x2d, nw, w_pad, qnw, knw, tm, window=None, sample=None):
    t = x2d.shape[0]
    row = lambda i: (i, 0)
    const = lambda i: (0, 0)
    sample_heads, sample_specs, sample_shapes, sample_out = 0, [], (), []
    if sample is not None:
        dec_batch, n_past = sample[4].shape[0], sample[4].shape[3]
        steps = t // tm
        assert steps % dec_batch == 0 and N_HEADS % (steps // dec_batch) == 0
        n_sub = steps // dec_batch
        sample_heads = N_HEADS // n_sub
        tok = pl.BlockSpec((None, HEAD_DIM, LANES), lambda i: (i // n_sub, 0, 0))
        cache = pl.BlockSpec((None, sample_heads, HEAD_DIM, n_past), lambda i: (i // n_sub, i % n_sub, 0, 0))
        sample_specs = [tok, tok, tok, tok, cache, cache,
                        pl.BlockSpec((len(PATTERNS), N_HEADS, n_past), lambda i: (0, 0, 0)),
                        pl.BlockSpec((1, LANES), const)]
        sample_shapes = (jax.ShapeDtypeStruct((dec_batch, HEAD_DIM, LANES), F32),)
        sample_out = [tok]
    win_shapes, win_specs = (), []
    if window is not None:
        seq, nwin = window
        assert seq % tm == 0 and nwin % tm == 0
        per_seq, first = seq // tm, (seq - nwin) // tm
        win_spec = pl.BlockSpec((None, D_ATTN, tm),
                                lambda i: (i // per_seq, 0, jnp.maximum(i % per_seq - first, 0)))
        win_shapes = (jax.ShapeDtypeStruct((t // seq, D_ATTN, nwin), F32),) * 2
        win_specs = [win_spec, win_spec]
    outs = (
        jax.ShapeDtypeStruct((t, D_ATTN), F32),
        jax.ShapeDtypeStruct((t, D_ATTN), F32),
        jax.ShapeDtypeStruct((t, D_ATTN), F32),
        jax.ShapeDtypeStruct((t, D_ATTN), BF16),
        jax.ShapeDtypeStruct((t, D_SSD), BF16),
        jax.ShapeDtypeStruct((t, CONV_DIM), F32),
        jax.ShapeDtypeStruct((t, LANES), F32),
    )
    return pl.pallas_call(
        functools.partial(_inproj_kernel, has_window=window is not None, sample_heads=sample_heads),
        grid=(t // tm,),
        in_specs=[
            pl.BlockSpec((tm, D_MODEL), row),
            pl.BlockSpec((1, D_MODEL), const),
            pl.BlockSpec((D_MODEL, D_IN_PAD), const, pipeline_mode=pl.Buffered(1)),
            pl.BlockSpec((1, 512), const),
            pl.BlockSpec((1, 512), const),
        ] + sample_specs,
        out_specs=[
            pl.BlockSpec((tm, D_ATTN), row),
            pl.BlockSpec((tm, D_ATTN), row),
            pl.BlockSpec((tm, D_ATTN), row),
            pl.BlockSpec((tm, D_ATTN), row),
            pl.BlockSpec((tm, D_SSD), row),
            pl.BlockSpec((tm, CONV_DIM), row),
            pl.BlockSpec((tm, LANES), row),
        ] + win_specs + sample_out,
        out_shape=outs + win_shapes + sample_shapes,
        compiler_params=pltpu.CompilerParams(
            dimension_semantics=("arbitrary",), vmem_limit_bytes=VMEM_LIMIT),
    )(x2d, nw, w_pad, qnw, knw, *(sample or ()))


def _outproj_kernel(x_ref, a_ref, s_ref, w_ref, y_ref):
    y_ref[...] = (x_ref[...]
                  + jnp.dot(a_ref[...], w_ref[0:D_ATTN, :], preferred_element_type=F32)
                  + jnp.dot(s_ref[...], w_ref[D_ATTN:, :], preferred_element_type=F32))


def _outproj(x2d, a, s, w_out_b, tm):
    t = x2d.shape[0]
    row = lambda i: (i, 0)
    return pl.pallas_call(
        _outproj_kernel,
        grid=(t // tm,),
        in_specs=[
            pl.BlockSpec((tm, D_MODEL), row),
            pl.BlockSpec((tm, D_ATTN), row),
            pl.BlockSpec((tm, D_SSD), row),
            pl.BlockSpec((D_ATTN + D_SSD, D_MODEL), lambda i: (0, 0), pipeline_mode=pl.Buffered(1)),
        ],
        out_specs=pl.BlockSpec((tm, D_MODEL), row),
        out_shape=jax.ShapeDtypeStruct((t, D_MODEL), F32),
        compiler_params=pltpu.CompilerParams(
            dimension_semantics=("arbitrary",), vmem_limit_bytes=VMEM_LIMIT),
    )(x2d, a, s, w_out_b)


def _attn_kernel(q_ref, k_ref, v_ref, g_ref, bias_ref, o_ref, qp, kp, vp, mid, m_s, l_s, acc_s, s_buf, m_buf):
    s_idx = pl.program_id(2)
    first = (s_idx == 0).astype(jnp.int32)
    slot = s_idx % 2
    pslot = 1 - slot
    lane = lax.broadcasted_iota(jnp.int32, (BLK, LANES), 1)
    is_a = lane < HEAD_DIM
    qscale = HEAD_DIM ** -0.5 * LOG2E
    ones = jnp.ones((2 * BLK, LANES), BF16)
    quarter = SPAN // 4

    @pl.when(s_idx == 0)
    def _():
        kp[pslot] = jnp.zeros((SPAN, LANES), F32)
        vp[pslot] = jnp.zeros((SPAN, LANES), F32)

    def regroup(src_ref, store, scale=None):
        for lo in range(4):
            mid[lo] = src_ref[pl.ds(lo, quarter, stride=4), :]
        for lo in range(4):
            for hi in range(4):
                val = mid[lo, pl.ds(hi, BLK, stride=4), :]
                store(4 * hi + lo, val if scale is None else val * scale)

    def store_q(r, val):
        qp[pl.ds(r * BLK, BLK), :] = val

    def store_k(r, val):
        kp[slot, pl.ds(r * BLK, BLK), :] = val

    def store_v(r, val):
        vp[slot, pl.ds(r * BLK, BLK), :] = val

    regroup(q_ref, store_q, qscale)
    regroup(k_ref, store_k)

    def chunks(d, blk):
        n_chunk = 16 // d
        length = BLK // n_chunk
        u, r = blk // d, blk % d
        is_u0 = u == 0
        cur = [pl.multiple_of((d * c + r) * BLK + u * length, 8) for c in range(n_chunk)]
        back = jnp.where(is_u0, BLK - length, (u - 1) * length)
        prev = [pl.multiple_of((d * c + r) * BLK + back, 8) for c in range(n_chunk)]
        return is_u0, length, cur, jnp.where(is_u0, pslot, slot), prev

    def gather(read, starts, length):
        return jnp.concatenate([read(pl.ds(st, length)) for st in starts], axis=0)

    def scatter(write, starts, length, val):
        for c, st in enumerate(starts):
            write(pl.ds(st, length), val[c * length:(c + 1) * length])

    def qk_stage(pi, d, grp, sl):
        for j in range(ATTN_GROUP):
            is_u0, length, cur, prev_slot, prev = chunks(d, grp * ATTN_GROUP + j)
            qsc = gather(lambda rows: qp[rows, :], cur, length)
            qq = jnp.concatenate([jnp.where(is_a, qsc, 0.0), jnp.where(is_a, 0.0, qsc)], axis=0).astype(BF16)
            kk = jnp.concatenate([gather(lambda rows: kp[prev_slot, rows, :], prev, length),
                                  gather(lambda rows: kp[slot, rows, :], cur, length)], axis=0).astype(BF16)
            bias = bias_ref[jnp.where(is_u0, first, 0), pi]
            s = lax.dot_general(qq, kk, (((1,), (1,)), ((), ())), preferred_element_type=F32) + bias
            s_buf[sl, j] = s
            m_buf[sl, j] = jnp.broadcast_to(jnp.max(s, axis=1, keepdims=True), (2 * BLK, LANES))

    def pv_stage(pi, d, grp, sl):
        for j in range(ATTN_GROUP):
            _, length, cur, prev_slot, prev = chunks(d, grp * ATTN_GROUP + j)
            m = m_buf[sl, j]
            p = jnp.exp2(s_buf[sl, j] - jnp.concatenate([m, m], axis=1)).astype(BF16)
            vv = jnp.concatenate([gather(lambda rows: vp[prev_slot, rows, :], prev, length),
                                  gather(lambda rows: vp[slot, rows, :], cur, length)], axis=0).astype(BF16)
            ol = jnp.dot(p, jnp.concatenate([vv, ones], axis=1), preferred_element_type=F32)
            for ref, val in ((acc_s, jnp.where(is_a, ol[:BLK, :LANES], ol[BLK:, :LANES])),
                             (l_s, jnp.where(is_a, ol[:BLK, LANES:], ol[BLK:, LANES:])),
                             (m_s, jnp.where(is_a, m[:BLK], m[BLK:]))):
                scatter(lambda rows, v, ref=ref: ref.__setitem__((pi, rows, slice(None)), v), cur, length, val)

    groups = [(pi, d, jnp.int32(grp)) for pi, (_, d) in enumerate(PATTERNS)
              for grp in range(SPAN // BLK // ATTN_GROUP)]

    def first_stage():
        qk_stage(*groups[0], 0)
        regroup(v_ref, store_v)

    _region(first_stage)
    for i, grp in enumerate(groups):
        def step(i=i, grp=grp):
            pv_stage(*grp, i % 2)
            if i + 1 < len(groups):
                qk_stage(*groups[i + 1], (i + 1) % 2)
        _region(step)

    for lo in range(4):
        for hi in range(4):
            rows = pl.ds((4 * hi + lo) * BLK, BLK)
            m = jnp.maximum(jnp.maximum(m_s[0, rows, :], m_s[1, rows, :]), m_s[2, rows, :])
            l = jnp.zeros((BLK, LANES), F32)
            acc = jnp.zeros((BLK, LANES), F32)
            for pi in range(len(PATTERNS)):
                e = jnp.exp2(m_s[pi, rows, :] - m)
                l = l + l_s[pi, rows, :] * e
                acc = acc + acc_s[pi, rows, :] * e
            mid[lo, pl.ds(hi, BLK, stride=4), :] = acc / l
    for lo in range(4):
        qp[pl.ds(lo, quarter, stride=4), :] = mid[lo]
    g = g_ref[...].astype(F32)
    o_ref[...] = (qp[...] * _silu(g)).astype(o_ref.dtype)


def _attn_prompt(q, k, v, g, bias_tbl, batch, seq):
    n_span = seq // SPAN
    n_hp = N_HEADS // 2
    cur = lambda hp, b, s: (b * n_span + s, hp)
    blk = (SPAN, LANES)
    return pl.pallas_call(
        _attn_kernel,
        grid=(n_hp, batch, n_span),
        in_specs=[
            pl.BlockSpec(blk, cur),
            pl.BlockSpec(blk, cur),
            pl.BlockSpec(blk, cur),
            pl.BlockSpec(blk, cur),
            pl.BlockSpec((None, 2, len(PATTERNS), 2 * BLK, 2 * BLK), lambda hp, b, s: (hp, 0, 0, 0, 0)),
        ],
        out_specs=pl.BlockSpec(blk, cur),
        out_shape=jax.ShapeDtypeStruct((batch * seq, D_ATTN), BF16),
        scratch_shapes=[
            pltpu.VMEM(blk, F32),
            pltpu.VMEM((2,) + blk, F32),
            pltpu.VMEM((2,) + blk, F32),
            pltpu.VMEM((4, SPAN // 4, LANES), F32),
            pltpu.VMEM((len(PATTERNS),) + blk, F32),
            pltpu.VMEM((len(PATTERNS),) + blk, F32),
            pltpu.VMEM((len(PATTERNS),) + blk, F32),
            pltpu.VMEM((2, ATTN_GROUP, 2 * BLK, 2 * BLK), F32),
            pltpu.VMEM((2, ATTN_GROUP, 2 * BLK, LANES), F32),
        ],
        compiler_params=pltpu.CompilerParams(
            dimension_semantics=("arbitrary", "arbitrary", "arbitrary"), vmem_limit_bytes=VMEM_LIMIT),
    )(q, k, v, g, bias_tbl)


def _bias_lookup(rel_bias, dist):
    bucket = _t5_bucket(dist)[..., None]
    edges = jnp.arange(N_BUCKETS)
    onehot = ((bucket >= edges) & (bucket < edges + 1)).astype(F32)
    return jnp.einsum('...b,bh->h...', onehot, rel_bias.astype(F32), precision=lax.Precision.HIGHEST)


def _prompt_bias_table(rel_bias):
    tbls = []
    for w, d in PATTERNS:
        n_chunk = 16 // d
        n = jnp.arange(BLK)
        idx = (n % (BLK // n_chunk)) * n_chunk + n // (BLK // n_chunk)
        i = idx[:, None]
        j = jnp.concatenate([idx, BLK + idx])[None, :]
        rel = i + BLK - j
        band = (rel >= 0) & (rel <= w // d)
        bias = _bias_lookup(rel_bias, jnp.maximum(rel, 0) * d) * LOG2E
        normal = jnp.where(band[None], bias, NEG)
        first = jnp.where((band & (j >= BLK))[None], bias, NEG)
        tbls.append(jnp.stack([normal, first], axis=0))
    t = jnp.stack(tbls, axis=1)
    t = t.reshape(2, len(PATTERNS), N_HEADS // 2, 2 * BLK, 2 * BLK)
    return jnp.moveaxis(t, 2, 0)


def _ssd_kernel(xbc_ref, z_ref, dt_ref, cw_ref, cb_ref, dtb_ref, alog_ref, dsk_ref, nw_ref,
                e_ref, tril_ref, s_ref, h_ref, cbuf):
    @pl.when(pl.program_id(1) == 0)
    def _():
        cbuf[:, 0:8, :] = jnp.zeros((CONV_DIM // LANES, 8, LANES), F32)
        h_ref[...] = jnp.zeros_like(h_ref)

    def body(ci, carry):
        for k in range(SSD_CHUNKS_UNROLL):
            rows = pl.ds(pl.multiple_of((ci * SSD_CHUNKS_UNROLL + k) * CHUNK, CHUNK), CHUNK)
            _ssd_chunk(xbc_ref.at[rows], z_ref.at[rows], dt_ref.at[rows], cw_ref, cb_ref, dtb_ref, alog_ref,
                       dsk_ref, nw_ref, e_ref, tril_ref, s_ref.at[rows], h_ref, cbuf)
        return carry
    lax.fori_loop(0, SSD_CHUNKS_PER_STEP // SSD_CHUNKS_UNROLL, body, 0)


def _ssd_chunk(xbc_ref, z_ref, dt_ref, cw_ref, cb_ref, dtb_ref, alog_ref, dsk_ref, nw_ref,
               e_ref, tril_ref, s_ref, h_ref, cbuf):
    n_slab = CONV_DIM // LANES

    xc_slabs = []
    for j in range(n_slab):
        cols = slice(j * LANES, (j + 1) * LANES)
        xj = xbc_ref[:, cols]
        cbuf[j, 8:8 + CHUNK, :] = xj
        acc = cb_ref[:, cols] + xj * cw_ref[CONV_W - 1:CONV_W, cols]
        for i in range(CONV_W - 1):
            tap = cbuf[pl.ds(j, 1, stride=2), pl.ds(8 - (CONV_W - 1) + i, CHUNK), :][0]
            acc = acc + tap * cw_ref[i:i + 1, cols]
        xc_slabs.append(_silu(acc))
        cbuf[j, 0:8, :] = xj[CHUNK - 8:, :]

    n_x = D_SSD // LANES
    xs = jnp.concatenate(xc_slabs[:n_x], axis=1)
    lane = lax.broadcasted_iota(jnp.int32, (CHUNK, LANES), 1)
    sub = lax.broadcasted_iota(jnp.int32, (CHUNK, LANES), 0)
    head_lane = lane < N_HEADS
    dt = jnp.where(head_lane, _softplus(dt_ref[...] + dtb_ref[...]), 0.0)
    la = dt * (-jnp.exp(alog_ref[...]))

    def hi_lo(val):
        hi = val.astype(BF16).astype(F32)
        return (hi + pltpu.roll(val - hi, N_HEADS, axis=1)).astype(BF16)

    cs2 = jnp.dot(tril_ref[...], hi_lo(la), preferred_element_type=F32)
    a_cs = jnp.where(head_lane, cs2 + pltpu.roll(cs2, LANES - N_HEADS, axis=1), 0.0)
    ea = jnp.where(head_lane, jnp.exp(a_cs), 0.0)
    dte = jnp.where(head_lane, jnp.exp(a_cs[CHUNK - 1:CHUNK, :] - a_cs), 0.0)

    expanded = jnp.dot(jnp.concatenate([hi_lo(dt), hi_lo(ea), hi_lo(dte)], axis=0), e_ref[...],
                       preferred_element_type=F32)
    dtx, eax, dtex = expanded[:CHUNK], expanded[CHUNK:2 * CHUNK], expanded[2 * CHUNK:]
    xdt_f = xs * dtx
    xdt = xdt_f.astype(BF16)
    xdte = (xdt_f * dtex).astype(BF16)

    a_cs_t = a_cs.T
    causal = sub >= lane
    is_a = lane < HEAD_DIM
    ys = []
    for g in range(N_GROUPS):
        b_g = xc_slabs[n_x + g]
        c_g = xc_slabs[n_x + N_GROUPS + g]
        b_bf = b_g.astype(BF16)
        c_bf = c_g.astype(BF16)
        cb = lax.dot_general(c_bf, b_bf, (((1,), (1,)), ((), ())), preferred_element_type=F32)
        gcols = slice(g * 512, (g + 1) * 512)
        h_prev = h_ref[:, gcols]
        y_off = jnp.dot(c_bf, h_prev.astype(BF16), preferred_element_type=F32) * eax[:, gcols]
        st = jnp.dot(b_g.T.astype(BF16), xdte[:, gcols], preferred_element_type=F32)
        h_ref[:, gcols] = h_prev * eax[CHUNK - 1:CHUNK, gcols] + st
        for hp in range(4):
            pair = []
            for which in range(2):
                h = g * 8 + hp * 2 + which
                col = jnp.sum(jnp.where(lane == h, a_cs, 0.0), axis=1, keepdims=True)
                seg = col - a_cs_t[h:h + 1, :]
                lmat = jnp.exp(jnp.where(causal, seg, NEG))
                pair.append((cb * lmat).astype(BF16))
            x_pair = xdt[:, g * 512 + hp * LANES:g * 512 + (hp + 1) * LANES]
            y_a = jnp.dot(pair[0], x_pair, preferred_element_type=F32)
            y_b = jnp.dot(pair[1], x_pair, preferred_element_type=F32)
            ys.append(jnp.where(is_a, y_a, y_b) + y_off[:, hp * LANES:(hp + 1) * LANES])
    y = jnp.concatenate(ys, axis=1) + dsk_ref[...] * xs
    zf = z_ref[...].astype(F32)
    yz = y * _silu(zf)
    var = jnp.mean(yz * yz, axis=-1, keepdims=True)
    s_ref[...] = (yz * lax.rsqrt(var + EPS) * nw_ref[...]).astype(s_ref.dtype)


def _ssd_prompt(xbc, z, dt, cw, cb, dtb, alog, dsk, nw, emat, tril, batch, seq):
    tc = CHUNK * SSD_CHUNKS_PER_STEP
    assert seq % tc == 0
    nc = seq // tc
    row = lambda b, c: (b * nc + c, 0)
    const = lambda b, c: (0, 0)
    return pl.pallas_call(
        _ssd_kernel,
        grid=(batch, nc),
        in_specs=[
            pl.BlockSpec((tc, CONV_DIM), row),
            pl.BlockSpec((tc, D_SSD), row),
            pl.BlockSpec((tc, LANES), row),
            pl.BlockSpec((CONV_W, CONV_DIM), const),
            pl.BlockSpec((1, CONV_DIM), const),
            pl.BlockSpec((1, LANES), const),
            pl.BlockSpec((1, LANES), const),
            pl.BlockSpec((1, D_SSD), const),
            pl.BlockSpec((1, D_SSD), const),
            pl.BlockSpec((LANES, D_SSD), const),
            pl.BlockSpec((CHUNK, CHUNK), const),
        ],
        out_specs=[
            pl.BlockSpec((tc, D_SSD), row),
            pl.BlockSpec((None, D_STATE, D_SSD), lambda b, c: (b, 0, 0)),
        ],
        out_shape=(jax.ShapeDtypeStruct((batch * seq, D_SSD), BF16),
                   jax.ShapeDtypeStruct((batch, D_STATE, D_SSD), F32)),
        scratch_shapes=[pltpu.VMEM((CONV_DIM // LANES, CHUNK + 8, LANES), F32)],
        compiler_params=pltpu.CompilerParams(
            dimension_semantics=("arbitrary", "arbitrary"), vmem_limit_bytes=VMEM_LIMIT),
    )(xbc, z, dt, cw, cb, dtb, alog, dsk, nw, emat, tril)


def _attn_sample_body(hh, n_heads, qt_ref, knt_ref, vnt_ref, gt_ref, k_ref, v_ref, btbl_ref, bias0_ref, o_ref):
    @pl.when(hh == 0)
    def _():
        o_ref[...] = jnp.zeros_like(o_ref)

    lane = lax.broadcasted_iota(jnp.int32, (HEAD_DIM, LANES), 1)
    lane1 = lax.broadcasted_iota(jnp.int32, (1, LANES), 1)
    qt = qt_ref[...] * (HEAD_DIM ** -0.5)
    n_pat = float(len(PATTERNS))
    for j in range(n_heads):
        h = hh * n_heads + j
        pick = lane == h

        def col(val, pick=pick):
            return jnp.sum(jnp.where(pick, val, 0.0), axis=1, keepdims=True)

        qc, knc, vnc, gc = col(qt), col(knt_ref[...]), col(vnt_ref[...]), col(gt_ref[...])
        b0 = jnp.sum(jnp.where(lane1 == h, bias0_ref[...], 0.0), axis=1, keepdims=True)
        s0 = jnp.sum(qc * knc, axis=0, keepdims=True) + b0
        s = jnp.sum(k_ref[j] * qc, axis=0, keepdims=True)
        sp = [s + btbl_ref[pi, pl.ds(h, 1), :] for pi in range(len(PATTERNS))]
        m = s0
        for x in sp:
            m = jnp.maximum(m, jnp.max(x, axis=1, keepdims=True))
        p0 = n_pat * jnp.exp(s0 - m)
        pw = jnp.exp(sp[0] - m)
        for x in sp[1:]:
            pw = pw + jnp.exp(x - m)
        l = jnp.sum(pw, axis=1, keepdims=True) + p0
        oc = (jnp.sum(v_ref[j] * pw, axis=1, keepdims=True) + p0 * vnc) / l
        o_ref[...] = jnp.where(pick, oc * _silu(gc), o_ref[...])


def _sample_bias_tables(rel_bias, n_past):
    dist = n_past - jnp.arange(n_past)
    bias = _bias_lookup(rel_bias, dist)
    tbls = [jnp.where(((dist % d == 0) & (dist <= w))[None], bias, NEG) for w, d in PATTERNS]
    bias0 = _bias_lookup(rel_bias, jnp.zeros((1,), jnp.int32))
    return jnp.stack(tbls, axis=0), jnp.pad(bias0.reshape(1, N_HEADS), ((0, 0), (0, LANES - N_HEADS)))


def _ssd_sample_kernel(xbc_ref, z_ref, dt_ref, sc_ref, h_ref, cw_ref, cb_ref, dtb_ref, alog_ref, dsk_ref,
                       nw_ref, e_ref, s_ref, conv_out_ref, h_out_ref):
    xnew = xbc_ref[...]
    sc = sc_ref[...]
    acc = cb_ref[...] + xnew * cw_ref[CONV_W - 1:CONV_W, :]
    for i in range(CONV_W - 1):
        acc = acc + sc[i:i + 1, :] * cw_ref[i:i + 1, :]
    xc = _silu(acc)
    conv_out_ref[0:CONV_W - 2, :] = sc[1:CONV_W - 1, :]
    conv_out_ref[CONV_W - 2:CONV_W - 1, :] = xnew

    xs = xc[:, :D_SSD]
    lane1 = lax.broadcasted_iota(jnp.int32, (1, LANES), 1)
    dt = jnp.where(lane1 < N_HEADS, _softplus(dt_ref[...] + dtb_ref[...]), 0.0)
    da = jnp.where(lane1 < N_HEADS, jnp.exp(dt * (-jnp.exp(alog_ref[...]))), 0.0)

    def expand(val):
        v8 = jnp.broadcast_to(val, (8, LANES))
        out = jnp.zeros((8, D_SSD), F32)
        for _ in range(3):
            part = v8.astype(BF16)
            out = out + jnp.dot(part, e_ref[...], preferred_element_type=F32)
            v8 = v8 - part.astype(F32)
        return out[0:1, :]

    xdt = xs * expand(dt)
    dax = expand(da)

    lane = lax.broadcasted_iota(jnp.int32, (HEAD_DIM, LANES), 1)
    sub = lax.broadcasted_iota(jnp.int32, (HEAD_DIM, LANES), 0)
    eye2 = (lane % HEAD_DIM) == sub
    is_a = lane < HEAD_DIM

    def to_cols(row):
        mat = jnp.where(eye2, jnp.broadcast_to(row, (HEAD_DIM, LANES)), 0.0)
        col_a = jnp.sum(jnp.where(is_a, mat, 0.0), axis=1, keepdims=True)
        col_b = jnp.sum(jnp.where(is_a, 0.0, mat), axis=1, keepdims=True)
        return col_a, col_b

    y_rows = []
    for hp in range(N_HEADS // 2):
        g = hp // 4
        b_row = xc[:, D_SSD + g * D_STATE:D_SSD + (g + 1) * D_STATE]
        c_row = xc[:, D_SSD + (N_GROUPS + g) * D_STATE:D_SSD + (N_GROUPS + g + 1) * D_STATE]
        cols = slice(hp * LANES, (hp + 1) * LANES)
        x_cols = to_cols(xdt[:, cols])
        d_cols = to_cols(dax[:, cols])
        y_cols = []
        for which in range(2):
            h = hp * 2 + which
            h_new = h_ref[h] * d_cols[which] + x_cols[which] * b_row
            h_out_ref[h] = h_new
            y_cols.append(jnp.sum(h_new * c_row, axis=1, keepdims=True))
        y_mat = jnp.where(eye2, jnp.where(is_a, y_cols[0], y_cols[1]), 0.0)
        y_rows.append(jnp.sum(y_mat, axis=0, keepdims=True))
    y = jnp.concatenate(y_rows, axis=1) + dsk_ref[...] * xs
    zf = z_ref[...].astype(F32)
    yz = y * _silu(zf)
    var = jnp.mean(yz * yz, axis=-1, keepdims=True)
    s_ref[...] = (yz * lax.rsqrt(var + EPS) * nw_ref[...]).astype(s_ref.dtype)


def _ssd_sample(xbc3, z3, dt3, state_conv, state_ssm, cw, cb, dtb, alog, dsk, nw, emat):
    b = xbc3.shape[0]
    const = lambda i: (0, 0)
    tok = lambda width: pl.BlockSpec((None, 1, width), lambda i: (i, 0, 0))
    conv_spec = pl.BlockSpec((None, CONV_W - 1, CONV_DIM), lambda i: (i, 0, 0))
    ssm_spec = pl.BlockSpec((None, N_HEADS, HEAD_DIM, D_STATE), lambda i: (i, 0, 0, 0))
    return pl.pallas_call(
        _ssd_sample_kernel,
        grid=(b,),
        in_specs=[
            tok(CONV_DIM), tok(D_SSD), tok(LANES), conv_spec, ssm_spec,
            pl.BlockSpec((CONV_W, CONV_DIM), const),
            pl.BlockSpec((1, CONV_DIM), const),
            pl.BlockSpec((1, LANES), const),
            pl.BlockSpec((1, LANES), const),
            pl.BlockSpec((1, D_SSD), const),
            pl.BlockSpec((1, D_SSD), const),
            pl.BlockSpec((LANES, D_SSD), const),
        ],
        out_specs=[tok(D_SSD), conv_spec, ssm_spec],
        out_shape=(jax.ShapeDtypeStruct((b, 1, D_SSD), BF16),
                   jax.ShapeDtypeStruct((b, CONV_W - 1, CONV_DIM), F32),
                   jax.ShapeDtypeStruct((b, N_HEADS, HEAD_DIM, D_STATE), F32)),
        compiler_params=pltpu.CompilerParams(
            dimension_semantics=("arbitrary",), vmem_limit_bytes=VMEM_LIMIT),
    )(xbc3, z3, dt3, state_conv, state_ssm, cw, cb, dtb, alog, dsk, nw, emat)


def kernel(x_prompt, x_sample, cache_win_k, cache_win_v, state_conv, state_ssm, norm_w, w_in, q_norm_w,
           k_norm_w, rel_bias, conv_w, conv_b, dt_bias, a_log, d_skip, ssd_norm_w, w_out):
    assert x_prompt.shape[-1] == D_MODEL and w_in.shape[0] == 1, "single-layer model of width 1024 only"
    batch, seq, _ = x_prompt.shape
    dec_batch, dec_seq, _ = x_sample.shape
    assert dec_seq == 1 and seq % SPAN == 0 and cache_win_k.shape[2] == WINDOW_MAX

    w_pad = jnp.pad(w_in[0], ((0, 0), (0, D_IN_PAD - D_IN_PROJ))).astype(BF16)
    w_out_b = w_out[0].astype(BF16)
    nw = norm_w[0].reshape(1, D_MODEL)
    qnw = jnp.tile(q_norm_w[0], 512 // HEAD_DIM).reshape(1, 512)
    knw = jnp.tile(k_norm_w[0], 512 // HEAD_DIM).reshape(1, 512)
    cw, cb = conv_w[0], conv_b[0].reshape(1, CONV_DIM)
    pad_heads = lambda a: jnp.pad(a.reshape(1, N_HEADS), ((0, 0), (0, LANES - N_HEADS)))
    dtb, alog = pad_heads(dt_bias[0]), pad_heads(a_log[0])
    dsk = jnp.repeat(d_skip[0], HEAD_DIM).reshape(1, D_SSD)
    snw = ssd_norm_w[0].reshape(1, D_SSD)
    erow = jnp.arange(LANES)[:, None]
    emat = ((erow % N_HEADS == (jnp.arange(D_SSD) // HEAD_DIM)[None, :]) & (erow < 2 * N_HEADS)).astype(BF16)
    tril = (jnp.arange(CHUNK)[:, None] >= jnp.arange(CHUNK)[None, :]).astype(BF16)

    xs2 = x_sample.reshape(dec_batch, D_MODEL)
    qs, ks, vs, gs, zs, xbcs, dts = _inproj(xs2, nw, w_pad, qnw, knw, tm=dec_batch)
    tok_t = lambda t: jnp.pad(jnp.swapaxes(t.astype(F32).reshape(dec_batch, N_HEADS, HEAD_DIM), 1, 2),
                              ((0, 0), (0, 0), (0, LANES - N_HEADS)))
    cache_t = lambda c: jnp.transpose(c[0], (0, 2, 3, 1))
    btbl, bias0 = _sample_bias_tables(rel_bias, cache_win_k.shape[2])
    sample_attn_args = (tok_t(qs), tok_t(ks), tok_t(vs), tok_t(gs), cache_t(cache_win_k), cache_t(cache_win_v),
                        btbl, bias0)

    xp = x_prompt.reshape(batch * seq, D_MODEL)
    nwin = min(WINDOW_MAX, seq)
    q, k, v, g, z, xbc, dt, k_win, v_win, a_t = _inproj(xp, nw, w_pad, qnw, knw, tm=256, window=(seq, nwin),
                                                        sample=sample_attn_args)
    a = _attn_prompt(q, k, v, g, _prompt_bias_table(rel_bias), batch, seq)
    s, h_fin = _ssd_prompt(xbc, z, dt, cw, cb, dtb, alog, dsk, snw, emat, tril, batch, seq)
    y_p = _outproj(xp, a, s, w_out_b, tm=1024).reshape(batch, seq, D_MODEL)
    heads = lambda t: jnp.transpose(t.reshape(1, batch, N_HEADS, HEAD_DIM, nwin), (0, 1, 4, 2, 3))
    kp, vp = heads(k_win), heads(v_win)
    cp = xbc.reshape(batch, seq, CONV_DIM)[None, :, seq - (CONV_W - 1):]
    hp = jnp.swapaxes(h_fin, 1, 2).reshape(batch, N_HEADS, HEAD_DIM, D_STATE)[None]

    a_s = jnp.swapaxes(a_t[:, :, :N_HEADS], 1, 2)
    s_s, conv_s, h_s = _ssd_sample(xbcs.reshape(dec_batch, 1, CONV_DIM), zs.reshape(dec_batch, 1, D_SSD),
                                   dts.reshape(dec_batch, 1, LANES), state_conv[0], state_ssm[0],
                                   cw, cb, dtb, alog, dsk, snw, emat)
    y_s = _outproj(xs2, a_s.reshape(dec_batch, D_ATTN).astype(BF16), s_s.reshape(dec_batch, D_SSD),
                   w_out_b, tm=dec_batch).reshape(dec_batch, 1, D_MODEL)
    k_s = ks.reshape(1, dec_batch, 1, N_HEADS, HEAD_DIM)
    v_s = vs.reshape(1, dec_batch, 1, N_HEADS, HEAD_DIM)
    return (y_p, y_s, kp, vp, cp, hp, k_s, v_s, conv_s[None], h_s[None])
```

```python
import functools
import math

import jax
import jax.numpy as jnp
from jax import lax
from jax.experimental import pallas as pl
from jax.experimental.pallas import tpu as pltpu

F32 = jnp.float32
BF16 = jnp.bfloat16

D_MODEL = 1024
D_ATTN = 1024
D_SSD = 1024
HEAD_DIM = 64
N_HEADS = 16
PATTERNS = ((128, 1), (512, 4), (2048, 16))
WINDOW_MAX = 2048
BLK = 128
N_BUCKETS = 32
D_STATE = 128
N_GROUPS = 2
CONV_W = 4
CONV_DIM = D_SSD + 2 * N_GROUPS * D_STATE
CHUNK = 128
EPS = 1e-6
D_IN_PROJ = 4 * D_ATTN + D_SSD + CONV_DIM + N_HEADS
LANES = 128
D_IN_PAD = D_IN_PROJ - N_HEADS + LANES
SPAN = BLK * 16
NEG = -1e30
LOG2E = math.log2(math.e)
SSD_CHUNKS_PER_STEP = 4
SSD_CHUNKS_UNROLL = 4
VMEM_LIMIT = 56 * 1024 * 1024


def _silu(x):
    h = 0.5 * x
    return h + h * jnp.tanh(h)


def _softplus(x):
    return jnp.maximum(x, 0.0) + jnp.log(1.0 + jnp.exp(-jnp.abs(x)))


def _region(fn):
    def body(i, carry):
        fn()
        return carry
    lax.fori_loop(0, 1 + jnp.minimum(pl.program_id(0), 0), body, 0)


def _t5_bucket(dist):
    max_exact = N_BUCKETS // 2
    d_f = jnp.maximum(dist, 1).astype(F32)
    large = max_exact + (jnp.log(d_f / max_exact) / math.log(WINDOW_MAX / max_exact)
                         * (N_BUCKETS - max_exact)).astype(jnp.int32)
    large = jnp.minimum(large, N_BUCKETS - 1)
    return jnp.where(dist < max_exact, dist, large)


def _inproj_kernel(*refs, has_window, sample_heads):
    x_ref, nw_ref, w_ref, qnw_ref, knw_ref = refs[:5]
    n_in = 5 + (8 if sample_heads else 0)
    q_ref, k_ref, v_ref, g_ref, z_ref, xbc_ref, dt_ref = refs[n_in:n_in + 7]
    kt_ref, vt_ref = refs[n_in + 7:n_in + 9] if has_window else (None, None)
    if sample_heads:
        per_seq = N_HEADS // sample_heads
        _attn_sample_body(pl.program_id(0) % per_seq, sample_heads, *refs[5:n_in], refs[-1])
    x = x_ref[...]
    h = (x * nw_ref[...]).astype(BF16)
    r = lax.rsqrt(jnp.mean(x * x, axis=-1, keepdims=True) + EPS)

    def seg(c0, width):
        return jnp.dot(h, w_ref[:, c0:c0 + width], preferred_element_type=F32) * r

    is_a = lax.broadcasted_iota(jnp.int32, (x.shape[0], LANES), 1) < HEAD_DIM

    def head_rms(pj):
        p2 = pj * pj
        ss_a = jnp.sum(jnp.where(is_a, p2, 0.0), axis=1, keepdims=True)
        ss_b = jnp.sum(jnp.where(is_a, 0.0, p2), axis=1, keepdims=True)
        return jnp.where(is_a, lax.rsqrt(ss_a * (1.0 / HEAD_DIM) + EPS), lax.rsqrt(ss_b * (1.0 / HEAD_DIM) + EPS))

    for out_ref, base, hw_ref in ((q_ref, 0, qnw_ref), (k_ref, D_ATTN, knw_ref)):
        for c in range(2):
            p = seg(base + 512 * c, 512)
            rs = jnp.concatenate([head_rms(p[:, LANES * j:LANES * (j + 1)]) for j in range(512 // LANES)], axis=1)
            normed = p * rs * hw_ref[...]
            out_ref[:, 512 * c:512 * (c + 1)] = normed
            if out_ref is k_ref and kt_ref is not None:
                kt_ref[512 * c:512 * (c + 1), :] = normed.T
    for c in range(2):
        v_c = seg(2 * D_ATTN + 512 * c, 512)
        v_ref[:, 512 * c:512 * (c + 1)] = v_c
        if vt_ref is not None:
            vt_ref[512 * c:512 * (c + 1), :] = v_c.T
        g_ref[:, 512 * c:512 * (c + 1)] = seg(3 * D_ATTN + 512 * c, 512).astype(g_ref.dtype)
        z_ref[:, 512 * c:512 * (c + 1)] = seg(4 * D_ATTN + 512 * c, 512).astype(z_ref.dtype)
    for c in range(3):
        xbc_ref[:, 512 * c:512 * (c + 1)] = seg(5 * D_ATTN + 512 * c, 512)
    dt_ref[...] = seg(5 * D_ATTN + CONV_DIM, LANES)


def _inproj(x2d, nw, w_pad, qnw, knw, tm, window=None, sample=None):
    t = x2d.shape[0]
    row = lambda i: (i, 0)
    const = lambda i: (0, 0)
    sample_heads, sample_specs, sample_shapes, sample_out = 0, [], (), []
    if sample is not None:
        dec_batch, n_past = sample[4].shape[0], sample[4].shape[3]
        steps = t // tm
        assert steps % dec_batch == 0 and N_HEADS % (steps // dec_batch) == 0
        n_sub = steps // dec_batch
        sample_heads = N_HEADS // n_sub
        tok = pl.BlockSpec((None, HEAD_DIM, LANES), lambda i: (i // n_sub, 0, 0))
        cache = pl.BlockSpec((None, sample_heads, HEAD_DIM, n_past), lambda i: (i // n_sub, i % n_sub, 0, 0))
        sample_specs = [tok, tok, tok, tok, cache, cache,
                        pl.BlockSpec((len(PATTERNS), N_HEADS, n_past), lambda i: (0, 0, 0)),
                        pl.BlockSpec((1, LANES), const)]
        sample_shapes = (jax.ShapeDtypeStruct((dec_batch, HEAD_DIM, LANES), F32),)
        sample_out = [tok]
    win_shapes, win_specs = (), []
    if window is not None:
        seq, nwin = window
        assert seq % tm == 0 and nwin % tm == 0
        per_seq, first = seq // tm, (seq - nwin) // tm
        win_spec = pl.BlockSpec((None, D_ATTN, tm),
                                lambda i: (i // per_seq, 0, jnp.maximum(i % per_seq - first, 0)))
        win_shapes = (jax.ShapeDtypeStruct((t // seq, D_ATTN, nwin), F32),) * 2
        win_specs = [win_spec, win_spec]
    outs = (
        jax.ShapeDtypeStruct((t, D_ATTN), F32),
        jax.ShapeDtypeStruct((t, D_ATTN), F32),
        jax.ShapeDtypeStruct((t, D_ATTN), F32),
        jax.ShapeDtypeStruct((t, D_ATTN), BF16),
        jax.ShapeDtypeStruct((t, D_SSD), BF16),
        jax.ShapeDtypeStruct((t, CONV_DIM), F32),
        jax.ShapeDtypeStruct((t, LANES), F32),
    )
    return pl.pallas_call(
        functools.partial(_inproj_kernel, has_window=window is not None, sample_heads=sample_heads),
        grid=(t // tm,),
        in_specs=[
            pl.BlockSpec((tm, D_MODEL), row),
            pl.BlockSpec((1, D_MODEL), const),
            pl.BlockSpec((D_MODEL, D_IN_PAD), const, pipeline_mode=pl.Buffered(1)),
            pl.BlockSpec((1, 512), const),
            pl.BlockSpec((1, 512), const),
        ] + sample_specs,
        out_specs=[
            pl.BlockSpec((tm, D_ATTN), row),
            pl.BlockSpec((tm, D_ATTN), row),
            pl.BlockSpec((tm, D_ATTN), row),
            pl.BlockSpec((tm, D_ATTN), row),
            pl.BlockSpec((tm, D_SSD), row),
            pl.BlockSpec((tm, CONV_DIM), row),
            pl.BlockSpec((tm, LANES), row),
        ] + win_specs + sample_out,
        out_shape=outs + win_shapes + sample_shapes,
        compiler_params=pltpu.CompilerParams(
            dimension_semantics=("arbitrary",), vmem_limit_bytes=VMEM_LIMIT),
    )(x2d, nw, w_pad, qnw, knw, *(sample or ()))


def _outproj_kernel(x_ref, a_ref, s_ref, w_ref, y_ref):
    y_ref[...] = (x_ref[...]
                  + jnp.dot(a_ref[...], w_ref[0:D_ATTN, :], preferred_element_type=F32)
                  + jnp.dot(s_ref[...], w_ref[D_ATTN:, :], preferred_element_type=F32))


def _outproj(x2d, a, s, w_out_b, tm):
    t = x2d.shape[0]
    row = lambda i: (i, 0)
    return pl.pallas_call(
        _outproj_kernel,
        grid=(t // tm,),
        in_specs=[
            pl.BlockSpec((tm, D_MODEL), row),
            pl.BlockSpec((tm, D_ATTN), row),
            pl.BlockSpec((tm, D_SSD), row),
            pl.BlockSpec((D_ATTN + D_SSD, D_MODEL), lambda i: (0, 0), pipeline_mode=pl.Buffered(1)),
        ],
        out_specs=pl.BlockSpec((tm, D_MODEL), row),
        out_shape=jax.ShapeDtypeStruct((t, D_MODEL), F32),
        compiler_params=pltpu.CompilerParams(
            dimension_semantics=("arbitrary",), vmem_limit_bytes=VMEM_LIMIT),
    )(x2d, a, s, w_out_b)


def _attn_kernel(q_ref, k_ref, v_ref, g_ref, bias_ref, o_ref, qp, kp, vp, mid, ma_s, mb_s, l_s, acc_s, s_buf):
    s_idx = pl.program_id(2)
    first = (s_idx == 0).astype(jnp.int32)
    slot = s_idx % 2
    pslot = 1 - slot
    lane = lax.broadcasted_iota(jnp.int32, (BLK, LANES), 1)
    is_a = lane < HEAD_DIM
    qscale = HEAD_DIM ** -0.5 * LOG2E
    ones = jnp.ones((2 * BLK, LANES), BF16)
    quarter = SPAN // 4

    @pl.when(s_idx == 0)
    def _():
        kp[pslot] = jnp.zeros((SPAN, LANES), F32)
        vp[pslot] = jnp.zeros((SPAN, LANES), F32)

    def regroup(src_ref, store, scale=None):
        for lo in range(4):
            mid[lo] = src_ref[pl.ds(lo, quarter, stride=4), :]
        for lo in range(4):
            for hi in range(4):
                val = mid[lo, pl.ds(hi, BLK, stride=4), :]
                store(4 * hi + lo, val if scale is None else val * scale)

    def store_q(r, val):
        qp[pl.ds(r * BLK, BLK), :] = val

    def store_k(r, val):
        kp[slot, pl.ds(r * BLK, BLK), :] = val

    def store_v(r, val):
        vp[slot, pl.ds(r * BLK, BLK), :] = val

    def chunks(d, blk):
        n_chunk = 16 // d
        length = BLK // n_chunk
        u, r = blk // d, blk % d
        is_u0 = u == 0
        cur = [pl.multiple_of((d * c + r) * BLK + u * length, 8) for c in range(n_chunk)]
        back = jnp.where(is_u0, BLK - length, (u - 1) * length)
        prev = [pl.multiple_of((d * c + r) * BLK + back, 8) for c in range(n_chunk)]
        return is_u0, length, cur, jnp.where(is_u0, pslot, slot), prev

    def gather(read, starts, length):
        return jnp.concatenate([read(pl.ds(st, length)) for st in starts], axis=0)

    def scatter(write, starts, length, val):
        for c, st in enumerate(starts):
            write(pl.ds(st, length), val[c * length:(c + 1) * length])

    n_blk = SPAN // BLK
    zero = jnp.int32(0)

    def qk_stage(pi, d):
        for j in range(n_blk):
            is_u0, length, cur, prev_slot, prev = chunks(d, zero + j)
            qsc = gather(lambda rows: qp[rows, :], cur, length)
            qq = jnp.concatenate([jnp.where(is_a, qsc, 0.0), jnp.where(is_a, 0.0, qsc)], axis=0).astype(BF16)
            kk = jnp.concatenate([gather(lambda rows: kp[prev_slot, rows, :], prev, length),
                                  gather(lambda rows: kp[slot, rows, :], cur, length)], axis=0).astype(BF16)
            bias = bias_ref[jnp.where(is_u0, first, 0), pi]
            s = lax.dot_general(qq, kk, (((1,), (1,)), ((), ())), preferred_element_type=F32) + bias
            s_buf[pi, j] = s
            m = jnp.broadcast_to(jnp.max(s, axis=1, keepdims=True), (2 * BLK, LANES))
            scatter(lambda rows, v: ma_s.__setitem__((pi, rows, slice(None)), v), cur, length, m[:BLK])
            scatter(lambda rows, v: mb_s.__setitem__((pi, rows, slice(None)), v), cur, length, m[BLK:])

    def pv_stage(pi, d):
        for j in range(n_blk):
            _, length, cur, prev_slot, prev = chunks(d, zero + j)
            m = jnp.concatenate([gather(lambda rows: ma_s[0, rows, :], cur, length),
                                 gather(lambda rows: mb_s[0, rows, :], cur, length)], axis=0)
            p = jnp.exp2(s_buf[pi, j] - jnp.concatenate([m, m], axis=1)).astype(BF16)
            vv = jnp.concatenate([gather(lambda rows: vp[prev_slot, rows, :], prev, length),
                                  gather(lambda rows: vp[slot, rows, :], cur, length)], axis=0).astype(BF16)
            ol = jnp.dot(p, jnp.concatenate([vv, ones], axis=1), preferred_element_type=F32)
            for ref, val in ((acc_s, jnp.where(is_a, ol[:BLK, :LANES], ol[BLK:, :LANES])),
                             (l_s, jnp.where(is_a, ol[:BLK, LANES:], ol[BLK:, LANES:]))):
                scatter(lambda rows, v, ref=ref: ref.__setitem__((pi, rows, slice(None)), v), cur, length, val)

    def shared_max():
        for m_ref in (ma_s, mb_s):
            m_ref[0] = jnp.maximum(jnp.maximum(m_ref[0], m_ref[1]), m_ref[2])

    def all_scores():
        regroup(q_ref, store_q, qscale)
        regroup(k_ref, store_k)
        for pi, (_, d) in reversed(list(enumerate(PATTERNS))):
            qk_stage(pi, d)
        regroup(v_ref, store_v)
        shared_max()

    def all_pv():
        for pi, (_, d) in enumerate(PATTERNS):
            pv_stage(pi, d)
        for lo in range(4):
            for hi in range(4):
                rows = pl.ds((4 * hi + lo) * BLK, BLK)
                acc = acc_s[0, rows, :] + acc_s[1, rows, :] + acc_s[2, rows, :]
                l = l_s[0, rows, :] + l_s[1, rows, :] + l_s[2, rows, :]
                mid[lo, pl.ds(hi, BLK, stride=4), :] = acc / l
        for lo in range(4):
            qp[pl.ds(lo, quarter, stride=4), :] = mid[lo]
        g = g_ref[...].astype(F32)
        o_ref[...] = (qp[...] * _silu(g)).astype(o_ref.dtype)

    _region(all_scores)
    _region(all_pv)


def _attn_prompt(q, k, v, g, bias_tbl, batch, seq):
    n_span = seq // SPAN
    n_hp = N_HEADS // 2
    cur = lambda hp, b, s: (b * n_span + s, hp)
    blk = (SPAN, LANES)
    return pl.pallas_call(
        _attn_kernel,
        grid=(n_hp, batch, n_span),
        in_specs=[
            pl.BlockSpec(blk, cur),
            pl.BlockSpec(blk, cur),
            pl.BlockSpec(blk, cur),
            pl.BlockSpec(blk, cur),
            pl.BlockSpec((None, 2, len(PATTERNS), 2 * BLK, 2 * BLK), lambda hp, b, s: (hp, 0, 0, 0, 0)),
        ],
        out_specs=pl.BlockSpec(blk, cur),
        out_shape=jax.ShapeDtypeStruct((batch * seq, D_ATTN), BF16),
        scratch_shapes=[
            pltpu.VMEM(blk, F32),
            pltpu.VMEM((2,) + blk, F32),
            pltpu.VMEM((2,) + blk, F32),
            pltpu.VMEM((4, SPAN // 4, LANES), F32),
            pltpu.VMEM((len(PATTERNS),) + blk, F32),
            pltpu.VMEM((len(PATTERNS),) + blk, F32),
            pltpu.VMEM((len(PATTERNS),) + blk, F32),
            pltpu.VMEM((len(PATTERNS),) + blk, F32),
            pltpu.VMEM((len(PATTERNS), SPAN // BLK, 2 * BLK, 2 * BLK), F32),
        ],
        compiler_params=pltpu.CompilerParams(
            dimension_semantics=("arbitrary", "arbitrary", "arbitrary"), vmem_limit_bytes=VMEM_LIMIT),
    )(q, k, v, g, bias_tbl)


def _bias_lookup(rel_bias, dist):
    bucket = _t5_bucket(dist)[..., None]
    edges = jnp.arange(N_BUCKETS)
    onehot = ((bucket >= edges) & (bucket < edges + 1)).astype(F32)
    return jnp.einsum('...b,bh->h...', onehot, rel_bias.astype(F32), precision=lax.Precision.HIGHEST)


def _prompt_bias_table(rel_bias):
    tbls = []
    for w, d in PATTERNS:
        n_chunk = 16 // d
        n = jnp.arange(BLK)
        idx = (n % (BLK // n_chunk)) * n_chunk + n // (BLK // n_chunk)
        i = idx[:, None]
        j = jnp.concatenate([idx, BLK + idx])[None, :]
        rel = i + BLK - j
        band = (rel >= 0) & (rel <= w // d)
        bias = _bias_lookup(rel_bias, jnp.maximum(rel, 0) * d) * LOG2E
        normal = jnp.where(band[None], bias, NEG)
        first = jnp.where((band & (j >= BLK))[None], bias, NEG)
        tbls.append(jnp.stack([normal, first], axis=0))
    t = jnp.stack(tbls, axis=1)
    t = t.reshape(2, len(PATTERNS), N_HEADS // 2, 2 * BLK, 2 * BLK)
    return jnp.moveaxis(t, 2, 0)


def _ssd_kernel(xbc_ref, z_ref, dt_ref, cw_ref, cb_ref, dtb_ref, alog_ref, dsk_ref, nw_ref,
                e_ref, tril_ref, s_ref, h_ref, cbuf):
    @pl.when(pl.program_id(1) == 0)
    def _():
        cbuf[:, 0:8, :] = jnp.zeros((CONV_DIM // LANES, 8, LANES), F32)
        h_ref[...] = jnp.zeros_like(h_ref)

    def body(ci, carry):
        for k in range(SSD_CHUNKS_UNROLL):
            rows = pl.ds(pl.multiple_of((ci * SSD_CHUNKS_UNROLL + k) * CHUNK, CHUNK), CHUNK)
            _ssd_chunk(xbc_ref.at[rows], z_ref.at[rows], dt_ref.at[rows], cw_ref, cb_ref, dtb_ref, alog_ref,
                       dsk_ref, nw_ref, e_ref, tril_ref, s_ref.at[rows], h_ref, cbuf)
        return carry
    lax.fori_loop(0, SSD_CHUNKS_PER_STEP // SSD_CHUNKS_UNROLL, body, 0)


def _ssd_chunk(xbc_ref, z_ref, dt_ref, cw_ref, cb_ref, dtb_ref, alog_ref, dsk_ref, nw_ref,
               e_ref, tril_ref, s_ref, h_ref, cbuf):
    n_slab = CONV_DIM // LANES

    xc_slabs = []
    for j in range(n_slab):
        cols = slice(j * LANES, (j + 1) * LANES)
        xj = xbc_ref[:, cols]
        cbuf[j, 8:8 + CHUNK, :] = xj
        acc = cb_ref[:, cols] + xj * cw_ref[CONV_W - 1:CONV_W, cols]
        for i in range(CONV_W - 1):
            tap = cbuf[pl.ds(j, 1, stride=2), pl.ds(8 - (CONV_W - 1) + i, CHUNK), :][0]
            acc = acc + tap * cw_ref[i:i + 1, cols]
        xc_slabs.append(_silu(acc))
        cbuf[j, 0:8, :] = xj[CHUNK - 8:, :]

    n_x = D_SSD // LANES
    xs = jnp.concatenate(xc_slabs[:n_x], axis=1)
    lane = lax.broadcasted_iota(jnp.int32, (CHUNK, LANES), 1)
    sub = lax.broadcasted_iota(jnp.int32, (CHUNK, LANES), 0)
    head_lane = lane < N_HEADS
    dt = jnp.where(head_lane, _softplus(dt_ref[...] + dtb_ref[...]), 0.0)
    la = dt * (-jnp.exp(alog_ref[...]))

    def hi_lo(val):
        hi = val.astype(BF16).astype(F32)
        return (hi + pltpu.roll(val - hi, N_HEADS, axis=1)).astype(BF16)

    cs2 = jnp.dot(tril_ref[...], hi_lo(la), preferred_element_type=F32)
    a_cs = jnp.where(head_lane, cs2 + pltpu.roll(cs2, LANES - N_HEADS, axis=1), 0.0)
    ea = jnp.where(head_lane, jnp.exp(a_cs), 0.0)
    dte = jnp.where(head_lane, jnp.exp(a_cs[CHUNK - 1:CHUNK, :] - a_cs), 0.0)

    expanded = jnp.dot(jnp.concatenate([hi_lo(dt), hi_lo(ea), hi_lo(dte)], axis=0), e_ref[...],
                       preferred_element_type=F32)
    dtx, eax, dtex = expanded[:CHUNK], expanded[CHUNK:2 * CHUNK], expanded[2 * CHUNK:]
    xdt_f = xs * dtx
    xdt = xdt_f.astype(BF16)
    xdte = (xdt_f * dtex).astype(BF16)

    a_cs_t = a_cs.T
    causal = sub >= lane
    is_a = lane < HEAD_DIM
    ys = []
    for g in range(N_GROUPS):
        b_g = xc_slabs[n_x + g]
        c_g = xc_slabs[n_x + N_GROUPS + g]
        b_bf = b_g.astype(BF16)
        c_bf = c_g.astype(BF16)
        cb = lax.dot_general(c_bf, b_bf, (((1,), (1,)), ((), ())), preferred_element_type=F32)
        gcols = slice(g * 512, (g + 1) * 512)
        h_prev = h_ref[:, gcols]
        y_off = jnp.dot(c_bf, h_prev.astype(BF16), preferred_element_type=F32) * eax[:, gcols]
        st = jnp.dot(b_g.T.astype(BF16), xdte[:, gcols], preferred_element_type=F32)
        h_ref[:, gcols] = h_prev * eax[CHUNK - 1:CHUNK, gcols] + st
        for hp in range(4):
            pair = []
            for which in range(2):
                h = g * 8 + hp * 2 + which
                col = jnp.sum(jnp.where(lane == h, a_cs, 0.0), axis=1, keepdims=True)
                seg = col - a_cs_t[h:h + 1, :]
                lmat = jnp.exp(jnp.where(causal, seg, NEG))
                pair.append((cb * lmat).astype(BF16))
            x_pair = xdt[:, g * 512 + hp * LANES:g * 512 + (hp + 1) * LANES]
            y_a = jnp.dot(pair[0], x_pair, preferred_element_type=F32)
            y_b = jnp.dot(pair[1], x_pair, preferred_element_type=F32)
            ys.append(jnp.where(is_a, y_a, y_b) + y_off[:, hp * LANES:(hp + 1) * LANES])
    y = jnp.concatenate(ys, axis=1) + dsk_ref[...] * xs
    zf = z_ref[...].astype(F32)
    yz = y * _silu(zf)
    var = jnp.mean(yz * yz, axis=-1, keepdims=True)
    s_ref[...] = (yz * lax.rsqrt(var + EPS) * nw_ref[...]).astype(s_ref.dtype)


def _ssd_prompt(xbc, z, dt, cw, cb, dtb, alog, dsk, nw, emat, tril, batch, seq):
    tc = CHUNK * SSD_CHUNKS_PER_STEP
    assert seq % tc == 0
    nc = seq // tc
    row = lambda b, c: (b * nc + c, 0)
    const = lambda b, c: (0, 0)
    return pl.pallas_call(
        _ssd_kernel,
        grid=(batch, nc),
        in_specs=[
            pl.BlockSpec((tc, CONV_DIM), row),
            pl.BlockSpec((tc, D_SSD), row),
            pl.BlockSpec((tc, LANES), row),
            pl.BlockSpec((CONV_W, CONV_DIM), const),
            pl.BlockSpec((1, CONV_DIM), const),
            pl.BlockSpec((1, LANES), const),
            pl.BlockSpec((1, LANES), const),
            pl.BlockSpec((1, D_SSD), const),
            pl.BlockSpec((1, D_SSD), const),
            pl.BlockSpec((LANES, D_SSD), const),
            pl.BlockSpec((CHUNK, CHUNK), const),
        ],
        out_specs=[
            pl.BlockSpec((tc, D_SSD), row),
            pl.BlockSpec((None, D_STATE, D_SSD), lambda b, c: (b, 0, 0)),
        ],
        out_shape=(jax.ShapeDtypeStruct((batch * seq, D_SSD), BF16),
                   jax.ShapeDtypeStruct((batch, D_STATE, D_SSD), F32)),
        scratch_shapes=[pltpu.VMEM((CONV_DIM // LANES, CHUNK + 8, LANES), F32)],
        compiler_params=pltpu.CompilerParams(
            dimension_semantics=("arbitrary", "arbitrary"), vmem_limit_bytes=VMEM_LIMIT),
    )(xbc, z, dt, cw, cb, dtb, alog, dsk, nw, emat, tril)


def _attn_sample_body(hh, n_heads, qt_ref, knt_ref, vnt_ref, gt_ref, k_ref, v_ref, btbl_ref, bias0_ref, o_ref):
    @pl.when(hh == 0)
    def _():
        o_ref[...] = jnp.zeros_like(o_ref)

    lane = lax.broadcasted_iota(jnp.int32, (HEAD_DIM, LANES), 1)
    lane1 = lax.broadcasted_iota(jnp.int32, (1, LANES), 1)
    qt = qt_ref[...] * (HEAD_DIM ** -0.5)
    n_pat = float(len(PATTERNS))
    for j in range(n_heads):
        h = hh * n_heads + j
        pick = lane == h

        def col(val, pick=pick):
            return jnp.sum(jnp.where(pick, val, 0.0), axis=1, keepdims=True)

        qc, knc, vnc, gc = col(qt), col(knt_ref[...]), col(vnt_ref[...]), col(gt_ref[...])
        b0 = jnp.sum(jnp.where(lane1 == h, bias0_ref[...], 0.0), axis=1, keepdims=True)
        s0 = jnp.sum(qc * knc, axis=0, keepdims=True) + b0
        s = jnp.sum(k_ref[j] * qc, axis=0, keepdims=True)
        sp = [s + btbl_ref[pi, pl.ds(h, 1), :] for pi in range(len(PATTERNS))]
        m = s0
        for x in sp:
            m = jnp.maximum(m, jnp.max(x, axis=1, keepdims=True))
        p0 = n_pat * jnp.exp(s0 - m)
        pw = jnp.exp(sp[0] - m)
        for x in sp[1:]:
            pw = pw + jnp.exp(x - m)
        l = jnp.sum(pw, axis=1, keepdims=True) + p0
        oc = (jnp.sum(v_ref[j] * pw, axis=1, keepdims=True) + p0 * vnc) / l
        o_ref[...] = jnp.where(pick, oc * _silu(gc), o_ref[...])


def _sample_bias_tables(rel_bias, n_past):
    dist = n_past - jnp.arange(n_past)
    bias = _bias_lookup(rel_bias, dist)
    tbls = [jnp.where(((dist % d == 0) & (dist <= w))[None], bias, NEG) for w, d in PATTERNS]
    bias0 = _bias_lookup(rel_bias, jnp.zeros((1,), jnp.int32))
    return jnp.stack(tbls, axis=0), jnp.pad(bias0.reshape(1, N_HEADS), ((0, 0), (0, LANES - N_HEADS)))


def _ssd_sample_kernel(xbc_ref, z_ref, dt_ref, sc_ref, h_ref, cw_ref, cb_ref, dtb_ref, alog_ref, dsk_ref,
                       nw_ref, e_ref, s_ref, conv_out_ref, h_out_ref):
    xnew = xbc_ref[...]
    sc = sc_ref[...]
    acc = cb_ref[...] + xnew * cw_ref[CONV_W - 1:CONV_W, :]
    for i in range(CONV_W - 1):
        acc = acc + sc[i:i + 1, :] * cw_ref[i:i + 1, :]
    xc = _silu(acc)
    conv_out_ref[0:CONV_W - 2, :] = sc[1:CONV_W - 1, :]
    conv_out_ref[CONV_W - 2:CONV_W - 1, :] = xnew

    xs = xc[:, :D_SSD]
    lane1 = lax.broadcasted_iota(jnp.int32, (1, LANES), 1)
    dt = jnp.where(lane1 < N_HEADS, _softplus(dt_ref[...] + dtb_ref[...]), 0.0)
    da = jnp.where(lane1 < N_HEADS, jnp.exp(dt * (-jnp.exp(alog_ref[...]))), 0.0)

    def expand(val):
        v8 = jnp.broadcast_to(val, (8, LANES))
        out = jnp.zeros((8, D_SSD), F32)
        for _ in range(3):
            part = v8.astype(BF16)
            out = out + jnp.dot(part, e_ref[...], preferred_element_type=F32)
            v8 = v8 - part.astype(F32)
        return out[0:1, :]

    xdt = xs * expand(dt)
    dax = expand(da)

    lane = lax.broadcasted_iota(jnp.int32, (HEAD_DIM, LANES), 1)
    sub = lax.broadcasted_iota(jnp.int32, (HEAD_DIM, LANES), 0)
    eye2 = (lane % HEAD_DIM) == sub
    is_a = lane < HEAD_DIM

    def to_cols(row):
        mat = jnp.where(eye2, jnp.broadcast_to(row, (HEAD_DIM, LANES)), 0.0)
        col_a = jnp.sum(jnp.where(is_a, mat, 0.0), axis=1, keepdims=True)
        col_b = jnp.sum(jnp.where(is_a, 0.0, mat), axis=1, keepdims=True)
        return col_a, col_b

    y_rows = []
    for hp in range(N_HEADS // 2):
        g = hp // 4
        b_row = xc[:, D_SSD + g * D_STATE:D_SSD + (g + 1) * D_STATE]
        c_row = xc[:, D_SSD + (N_GROUPS + g) * D_STATE:D_SSD + (N_GROUPS + g + 1) * D_STATE]
        cols = slice(hp * LANES, (hp + 1) * LANES)
        x_cols = to_cols(xdt[:, cols])
        d_cols = to_cols(dax[:, cols])
        y_cols = []
        for which in range(2):
            h = hp * 2 + which
            h_new = h_ref[h] * d_cols[which] + x_cols[which] * b_row
            h_out_ref[h] = h_new
            y_cols.append(jnp.sum(h_new * c_row, axis=1, keepdims=True))
        y_mat = jnp.where(eye2, jnp.where(is_a, y_cols[0], y_cols[1]), 0.0)
        y_rows.append(jnp.sum(y_mat, axis=0, keepdims=True))
    y = jnp.concatenate(y_rows, axis=1) + dsk_ref[...] * xs
    zf = z_ref[...].astype(F32)
    yz = y * _silu(zf)
    var = jnp.mean(yz * yz, axis=-1, keepdims=True)
    s_ref[...] = (yz * lax.rsqrt(var + EPS) * nw_ref[...]).astype(s_ref.dtype)


def _ssd_sample(xbc3, z3, dt3, state_conv, state_ssm, cw, cb, dtb, alog, dsk, nw, emat):
    b = xbc3.shape[0]
    const = lambda i: (0, 0)
    tok = lambda width: pl.BlockSpec((None, 1, width), lambda i: (i, 0, 0))
    conv_spec = pl.BlockSpec((None, CONV_W - 1, CONV_DIM), lambda i: (i, 0, 0))
    ssm_spec = pl.BlockSpec((None, N_HEADS, HEAD_DIM, D_STATE), lambda i: (i, 0, 0, 0))
    return pl.pallas_call(
        _ssd_sample_kernel,
        grid=(b,),
        in_specs=[
            tok(CONV_DIM), tok(D_SSD), tok(LANES), conv_spec, ssm_spec,
            pl.BlockSpec((CONV_W, CONV_DIM), const),
            pl.BlockSpec((1, CONV_DIM), const),
            pl.BlockSpec((1, LANES), const),
            pl.BlockSpec((1, LANES), const),
            pl.BlockSpec((1, D_SSD), const),
            pl.BlockSpec((1, D_SSD), const),
            pl.BlockSpec((LANES, D_SSD), const),
        ],
        out_specs=[tok(D_SSD), conv_spec, ssm_spec],
        out_shape=(jax.ShapeDtypeStruct((b, 1, D_SSD), BF16),
                   jax.ShapeDtypeStruct((b, CONV_W - 1, CONV_DIM), F32),
                   jax.ShapeDtypeStruct((b, N_HEADS, HEAD_DIM, D_STATE), F32)),
        compiler_params=pltpu.CompilerParams(
            dimension_semantics=("arbitrary",), vmem_limit_bytes=VMEM_LIMIT),
    )(xbc3, z3, dt3, state_conv, state_ssm, cw, cb, dtb, alog, dsk, nw, emat)


def kernel(x_prompt, x_sample, cache_win_k, cache_win_v, state_conv, state_ssm, norm_w, w_in, q_norm_w,
           k_norm_w, rel_bias, conv_w, conv_b, dt_bias, a_log, d_skip, ssd_norm_w, w_out):
    assert x_prompt.shape[-1] == D_MODEL and w_in.shape[0] == 1, "single-layer model of width 1024 only"
    batch, seq, _ = x_prompt.shape
    dec_batch, dec_seq, _ = x_sample.shape
    assert dec_seq == 1 and seq % SPAN == 0 and cache_win_k.shape[2] == WINDOW_MAX

    w_pad = jnp.pad(w_in[0], ((0, 0), (0, D_IN_PAD - D_IN_PROJ))).astype(BF16)
    w_out_b = w_out[0].astype(BF16)
    nw = norm_w[0].reshape(1, D_MODEL)
    qnw = jnp.tile(q_norm_w[0], 512 // HEAD_DIM).reshape(1, 512)
    knw = jnp.tile(k_norm_w[0], 512 // HEAD_DIM).reshape(1, 512)
    cw, cb = conv_w[0], conv_b[0].reshape(1, CONV_DIM)
    pad_heads = lambda a: jnp.pad(a.reshape(1, N_HEADS), ((0, 0), (0, LANES - N_HEADS)))
    dtb, alog = pad_heads(dt_bias[0]), pad_heads(a_log[0])
    dsk = jnp.repeat(d_skip[0], HEAD_DIM).reshape(1, D_SSD)
    snw = ssd_norm_w[0].reshape(1, D_SSD)
    erow = jnp.arange(LANES)[:, None]
    emat = ((erow % N_HEADS == (jnp.arange(D_SSD) // HEAD_DIM)[None, :]) & (erow < 2 * N_HEADS)).astype(BF16)
    tril = (jnp.arange(CHUNK)[:, None] >= jnp.arange(CHUNK)[None, :]).astype(BF16)

    xs2 = x_sample.reshape(dec_batch, D_MODEL)
    qs, ks, vs, gs, zs, xbcs, dts = _inproj(xs2, nw, w_pad, qnw, knw, tm=dec_batch)
    tok_t = lambda t: jnp.pad(jnp.swapaxes(t.astype(F32).reshape(dec_batch, N_HEADS, HEAD_DIM), 1, 2),
                              ((0, 0), (0, 0), (0, LANES - N_HEADS)))
    cache_t = lambda c: jnp.transpose(c[0], (0, 2, 3, 1))
    btbl, bias0 = _sample_bias_tables(rel_bias, cache_win_k.shape[2])
    sample_attn_args = (tok_t(qs), tok_t(ks), tok_t(vs), tok_t(gs), cache_t(cache_win_k), cache_t(cache_win_v),
                        btbl, bias0)

    xp = x_prompt.reshape(batch * seq, D_MODEL)
    nwin = min(WINDOW_MAX, seq)
    q, k, v, g, z, xbc, dt, k_win, v_win, a_t = _inproj(xp, nw, w_pad, qnw, knw, tm=256, window=(seq, nwin),
                                                        sample=sample_attn_args)
    a = _attn_prompt(q, k, v, g, _prompt_bias_table(rel_bias), batch, seq)
    s, h_fin = _ssd_prompt(xbc, z, dt, cw, cb, dtb, alog, dsk, snw, emat, tril, batch, seq)
    y_p = _outproj(xp, a, s, w_out_b, tm=1024).reshape(batch, seq, D_MODEL)
    heads = lambda t: jnp.transpose(t.reshape(1, batch, N_HEADS, HEAD_DIM, nwin), (0, 1, 4, 2, 3))
    kp, vp = heads(k_win), heads(v_win)
    cp = xbc.reshape(batch, seq, CONV_DIM)[None, :, seq - (CONV_W - 1):]
    hp = jnp.swapaxes(h_fin, 1, 2).reshape(batch, N_HEADS, HEAD_DIM, D_STATE)[None]

    a_s = jnp.swapaxes(a_t[:, :, :N_HEADS], 1, 2)
    s_s, conv_s, h_s = _ssd_sample(xbcs.reshape(dec_batch, 1, CONV_DIM), zs.reshape(dec_batch, 1, D_SSD),
                                   dts.reshape(dec_batch, 1, LANES), state_conv[0], state_ssm[0],
                                   cw, cb, dtb, alog, dsk, snw, emat)
    y_s = _outproj(xs2, a_s.reshape(dec_batch, D_ATTN).astype(BF16), s_s.reshape(dec_batch, D_SSD),
                   w_out_b, tm=dec_batch).reshape(dec_batch, 1, D_MODEL)
    k_s = ks.reshape(1, dec_batch, 1, N_HEADS, HEAD_DIM)
    v_s = vs.reshape(1, dec_batch, 1, N_HEADS, HEAD_DIM)
    return (y_p, y_s, kp, vp, cp, hp, k_s, v_s, conv_s[None], h_s[None])
```

```python
import functools
import math

import jax
import jax.numpy as jnp
from jax import lax
from jax.experimental import pallas as pl
from jax.experimental.pallas import tpu as pltpu

F32 = jnp.float32
BF16 = jnp.bfloat16

D_MODEL = 1024
D_ATTN = 1024
D_SSD = 1024
HEAD_DIM = 64
N_HEADS = 16
PATTERNS = ((128, 1), (512, 4), (2048, 16))
WINDOW_MAX = 2048
BLK = 128
N_BUCKETS = 32
D_STATE = 128
N_GROUPS = 2
CONV_W = 4
CONV_DIM = D_SSD + 2 * N_GROUPS * D_STATE
CHUNK = 128
EPS = 1e-6
D_IN_PROJ = 4 * D_ATTN + D_SSD + CONV_DIM + N_HEADS
LANES = 128
D_IN_PAD = D_IN_PROJ - N_HEADS + LANES
SPAN = BLK * 16
NEG = -1e30
LOG2E = math.log2(math.e)
SSD_CHUNKS_PER_STEP = 8
SSD_CHUNKS_UNROLL = 8
VMEM_LIMIT = 56 * 1024 * 1024


def _silu(x):
    h = 0.5 * x
    return h + h * jnp.tanh(h)


def _softplus(x):
    return jnp.maximum(x, 0.0) + jnp.log(1.0 + jnp.exp(-jnp.abs(x)))


def _region(fn):
    def body(i, carry):
        fn()
        return carry
    lax.fori_loop(0, 1 + jnp.minimum(pl.program_id(0), 0), body, 0)


def _t5_bucket(dist):
    max_exact = N_BUCKETS // 2
    d_f = jnp.maximum(dist, 1).astype(F32)
    large = max_exact + (jnp.log(d_f / max_exact) / math.log(WINDOW_MAX / max_exact)
                         * (N_BUCKETS - max_exact)).astype(jnp.int32)
    large = jnp.minimum(large, N_BUCKETS - 1)
    return jnp.where(dist < max_exact, dist, large)


def _inproj_kernel(*refs, has_window, sample_heads):
    x_ref, nw_ref, w_ref, qnw_ref, knw_ref = refs[:5]
    n_in = 5 + (8 if sample_heads else 0)
    q_ref, k_ref, v_ref, g_ref, z_ref, xbc_ref, dt_ref = refs[n_in:n_in + 7]
    kt_ref, vt_ref = refs[n_in + 7:n_in + 9] if has_window else (None, None)
    if sample_heads:
        @pl.when(pl.program_id(0) % (N_HEADS // sample_heads) == 0)
        def _():
            refs[-1][...] = jnp.zeros_like(refs[-1])
    x = x_ref[...]
    h = (x * nw_ref[...]).astype(BF16)
    r = lax.rsqrt(jnp.mean(x * x, axis=-1, keepdims=True) + EPS)

    def seg(c0, width):
        return jnp.dot(h, w_ref[:, c0:c0 + width], preferred_element_type=F32) * r

    is_a = lax.broadcasted_iota(jnp.int32, (x.shape[0], LANES), 1) < HEAD_DIM

    def head_rms(pj):
        p2 = pj * pj
        ss_a = jnp.sum(jnp.where(is_a, p2, 0.0), axis=1, keepdims=True)
        ss_b = jnp.sum(jnp.where(is_a, 0.0, p2), axis=1, keepdims=True)
        return jnp.where(is_a, lax.rsqrt(ss_a * (1.0 / HEAD_DIM) + EPS), lax.rsqrt(ss_b * (1.0 / HEAD_DIM) + EPS))

    for out_ref, base, hw_ref in ((q_ref, 0, qnw_ref), (k_ref, D_ATTN, knw_ref)):
        for c in range(2):
            p = seg(base + 512 * c, 512)
            rs = jnp.concatenate([head_rms(p[:, LANES * j:LANES * (j + 1)]) for j in range(512 // LANES)], axis=1)
            normed = p * rs * hw_ref[...]
            out_ref[:, 512 * c:512 * (c + 1)] = normed
            if out_ref is k_ref and kt_ref is not None:
                kt_ref[512 * c:512 * (c + 1), :] = normed.T

    def v_g_z(c):
        v_c = seg(2 * D_ATTN + 512 * c, 512)
        v_ref[:, 512 * c:512 * (c + 1)] = v_c
        if vt_ref is not None:
            vt_ref[512 * c:512 * (c + 1), :] = v_c.T
        g_ref[:, 512 * c:512 * (c + 1)] = seg(3 * D_ATTN + 512 * c, 512).astype(g_ref.dtype)
        z_ref[:, 512 * c:512 * (c + 1)] = seg(4 * D_ATTN + 512 * c, 512).astype(z_ref.dtype)

    def xbc(c):
        xbc_ref[:, 512 * c:512 * (c + 1)] = seg(5 * D_ATTN + 512 * c, 512)

    def xbc_tail():
        xbc(1)
        xbc(2)
        dt_ref[...] = seg(5 * D_ATTN + CONV_DIM, LANES)

    plain = [functools.partial(v_g_z, 0), functools.partial(v_g_z, 1), functools.partial(xbc, 0), xbc_tail]
    for i, segment in enumerate(plain):
        if sample_heads:
            hh = pl.program_id(0) % (N_HEADS // sample_heads)
            _attn_sample_heads(hh, sample_heads, range(i, sample_heads, len(plain)), *refs[5:n_in], refs[-1])
        segment()


def _inproj(x2d, nw, w_pad, qnw, knw, tm, window=None, sample=None):
    t = x2d.shape[0]
    row = lambda i: (i, 0)
    const = lambda i: (0, 0)
    sample_heads, sample_specs, sample_shapes, sample_out = 0, [], (), []
    if sample is not None:
        dec_batch, n_past = sample[4].shape[0], sample[4].shape[3]
        steps = t // tm
        assert steps % dec_batch == 0 and N_HEADS % (steps // dec_batch) == 0
        n_sub = steps // dec_batch
        sample_heads = N_HEADS // n_sub
        tok = pl.BlockSpec((None, HEAD_DIM, LANES), lambda i: (i // n_sub, 0, 0))
        cache = pl.BlockSpec((None, sample_heads, HEAD_DIM, n_past), lambda i: (i // n_sub, i % n_sub, 0, 0))
        sample_specs = [tok, tok, tok, tok, cache, cache,
                        pl.BlockSpec((len(PATTERNS), N_HEADS, n_past), lambda i: (0, 0, 0)),
                        pl.BlockSpec((1, LANES), const)]
        sample_shapes = (jax.ShapeDtypeStruct((dec_batch, HEAD_DIM, LANES), F32),)
        sample_out = [tok]
    win_shapes, win_specs = (), []
    if window is not None:
        seq, nwin = window
        assert seq % tm == 0 and nwin % tm == 0
        per_seq, first = seq // tm, (seq - nwin) // tm
        win_spec = pl.BlockSpec((None, D_ATTN, tm),
                                lambda i: (i // per_seq, 0, jnp.maximum(i % per_seq - first, 0)))
        win_shapes = (jax.ShapeDtypeStruct((t // seq, D_ATTN, nwin), F32),) * 2
        win_specs = [win_spec, win_spec]
    outs = (
        jax.ShapeDtypeStruct((t, D_ATTN), F32),
        jax.ShapeDtypeStruct((t, D_ATTN), F32),
        jax.ShapeDtypeStruct((t, D_ATTN), F32),
        jax.ShapeDtypeStruct((t, D_ATTN), BF16),
        jax.ShapeDtypeStruct((t, D_SSD), BF16),
        jax.ShapeDtypeStruct((t, CONV_DIM), F32),
        jax.ShapeDtypeStruct((t, LANES), F32),
    )
    return pl.pallas_call(
        functools.partial(_inproj_kernel, has_window=window is not None, sample_heads=sample_heads),
        grid=(t // tm,),
        in_specs=[
            pl.BlockSpec((tm, D_MODEL), row),
            pl.BlockSpec((1, D_MODEL), const),
            pl.BlockSpec((D_MODEL, D_IN_PAD), const, pipeline_mode=pl.Buffered(1)),
            pl.BlockSpec((1, 512), const),
            pl.BlockSpec((1, 512), const),
        ] + sample_specs,
        out_specs=[
            pl.BlockSpec((tm, D_ATTN), row),
            pl.BlockSpec((tm, D_ATTN), row),
            pl.BlockSpec((tm, D_ATTN), row),
            pl.BlockSpec((tm, D_ATTN), row),
            pl.BlockSpec((tm, D_SSD), row),
            pl.BlockSpec((tm, CONV_DIM), row),
            pl.BlockSpec((tm, LANES), row),
        ] + win_specs + sample_out,
        out_shape=outs + win_shapes + sample_shapes,
        compiler_params=pltpu.CompilerParams(
            dimension_semantics=("arbitrary",), vmem_limit_bytes=VMEM_LIMIT),
    )(x2d, nw, w_pad, qnw, knw, *(sample or ()))


N_SSD_SAMPLE_IN = 12


def _outproj_kernel(x_ref, a_ref, s_ref, w_ref, *rest):
    if len(rest) > 1:
        _ssd_sample_kernel(*rest[:N_SSD_SAMPLE_IN], *rest[N_SSD_SAMPLE_IN + 1:])
    y_ref = rest[N_SSD_SAMPLE_IN] if len(rest) > 1 else rest[0]
    y_ref[...] = (x_ref[...]
                  + jnp.dot(a_ref[...], w_ref[0:D_ATTN, :], preferred_element_type=F32)
                  + jnp.dot(s_ref[...], w_ref[D_ATTN:, :], preferred_element_type=F32))


def _outproj(x2d, a, s, w_out_b, tm, ssd_sample=None):
    t = x2d.shape[0]
    row = lambda i: (i, 0)
    extra_in, extra_out, extra_shapes = [], [], ()
    if ssd_sample is not None:
        assert t // tm == ssd_sample[0].shape[0], "one sampled sequence per grid step"
        extra_in, extra_out, extra_shapes = _ssd_sample_specs(ssd_sample[0].shape[0])
    return pl.pallas_call(
        _outproj_kernel,
        grid=(t // tm,),
        in_specs=[
            pl.BlockSpec((tm, D_MODEL), row),
            pl.BlockSpec((tm, D_ATTN), row),
            pl.BlockSpec((tm, D_SSD), row),
            pl.BlockSpec((D_ATTN + D_SSD, D_MODEL), lambda i: (0, 0), pipeline_mode=pl.Buffered(1)),
        ] + extra_in,
        out_specs=[pl.BlockSpec((tm, D_MODEL), row)] + extra_out,
        out_shape=(jax.ShapeDtypeStruct((t, D_MODEL), F32),) + extra_shapes,
        compiler_params=pltpu.CompilerParams(
            dimension_semantics=("arbitrary",), vmem_limit_bytes=VMEM_LIMIT),
    )(x2d, a, s, w_out_b, *(ssd_sample or ()))


def _attn_kernel(q_ref, k_ref, v_ref, g_ref, bias_ref, o_ref, qp, kp, vp, mid, ma_s, mb_s, l_s, acc_s, s_buf):
    s_idx = pl.program_id(2)
    first = (s_idx == 0).astype(jnp.int32)
    slot = s_idx % 2
    pslot = 1 - slot
    lane = lax.broadcasted_iota(jnp.int32, (BLK, LANES), 1)
    is_a = lane < HEAD_DIM
    qscale = HEAD_DIM ** -0.5 * LOG2E
    ones = jnp.ones((2 * BLK, LANES), BF16)
    quarter = SPAN // 4

    @pl.when(s_idx == 0)
    def _():
        kp[pslot] = jnp.zeros((SPAN, LANES), F32)
        vp[pslot] = jnp.zeros((SPAN, LANES), F32)

    def regroup(src_ref, store, scale=None):
        for lo in range(4):
            mid[lo] = src_ref[pl.ds(lo, quarter, stride=4), :]
        for lo in range(4):
            for hi in range(4):
                val = mid[lo, pl.ds(hi, BLK, stride=4), :]
                store(4 * hi + lo, val if scale is None else val * scale)

    def store_q(r, val):
        qp[pl.ds(r * BLK, BLK), :] = val

    def store_k(r, val):
        kp[slot, pl.ds(r * BLK, BLK), :] = val

    def store_v(r, val):
        vp[slot, pl.ds(r * BLK, BLK), :] = val

    def chunks(d, blk):
        n_chunk = 16 // d
        length = BLK // n_chunk
        u, r = blk // d, blk % d
        is_u0 = u == 0
        cur = [pl.multiple_of((d * c + r) * BLK + u * length, 8) for c in range(n_chunk)]
        back = jnp.where(is_u0, BLK - length, (u - 1) * length)
        prev = [pl.multiple_of((d * c + r) * BLK + back, 8) for c in range(n_chunk)]
        return is_u0, length, cur, jnp.where(is_u0, pslot, slot), prev

    def gather(read, starts, length):
        return jnp.concatenate([read(pl.ds(st, length)) for st in starts], axis=0)

    def scatter(write, starts, length, val):
        for c, st in enumerate(starts):
            write(pl.ds(st, length), val[c * length:(c + 1) * length])

    n_blk = SPAN // BLK
    zero = jnp.int32(0)

    def qk_stage(pi, d):
        for j in range(n_blk):
            is_u0, length, cur, prev_slot, prev = chunks(d, zero + j)
            qsc = gather(lambda rows: qp[rows, :], cur, length)
            qq = jnp.concatenate([jnp.where(is_a, qsc, 0.0), jnp.where(is_a, 0.0, qsc)], axis=0).astype(BF16)
            kk = jnp.concatenate([gather(lambda rows: kp[prev_slot, rows, :], prev, length),
                                  gather(lambda rows: kp[slot, rows, :], cur, length)], axis=0).astype(BF16)
            bias = bias_ref[jnp.where(is_u0, first, 0), pi]
            s = lax.dot_general(qq, kk, (((1,), (1,)), ((), ())), preferred_element_type=F32) + bias
            s_buf[pi, j] = s
            m = jnp.broadcast_to(jnp.max(s, axis=1, keepdims=True), (2 * BLK, LANES))
            scatter(lambda rows, v: ma_s.__setitem__((pi, rows, slice(None)), v), cur, length, m[:BLK])
            scatter(lambda rows, v: mb_s.__setitem__((pi, rows, slice(None)), v), cur, length, m[BLK:])

    def pv_stage(pi, d):
        for j in range(n_blk):
            _, length, cur, prev_slot, prev = chunks(d, zero + j)
            m = jnp.concatenate([gather(lambda rows: ma_s[0, rows, :], cur, length),
                                 gather(lambda rows: mb_s[0, rows, :], cur, length)], axis=0)
            p = jnp.exp2(s_buf[pi, j] - jnp.concatenate([m, m], axis=1)).astype(BF16)
            vv = jnp.concatenate([gather(lambda rows: vp[prev_slot, rows, :], prev, length),
                                  gather(lambda rows: vp[slot, rows, :], cur, length)], axis=0).astype(BF16)
            ol = jnp.dot(p, jnp.concatenate([vv, ones], axis=1), preferred_element_type=F32)
            for ref, val in ((acc_s, jnp.where(is_a, ol[:BLK, :LANES], ol[BLK:, :LANES])),
                             (l_s, jnp.where(is_a, ol[:BLK, LANES:], ol[BLK:, LANES:]))):
                scatter(lambda rows, v, ref=ref: ref.__setitem__((pi, rows, slice(None)), v), cur, length, val)

    def shared_max():
        for m_ref in (ma_s, mb_s):
            m_ref[0] = jnp.maximum(jnp.maximum(m_ref[0], m_ref[1]), m_ref[2])

    def all_scores():
        regroup(q_ref, store_q, qscale)
        regroup(k_ref, store_k)
        for pi, (_, d) in reversed(list(enumerate(PATTERNS))):
            qk_stage(pi, d)
        regroup(v_ref, store_v)
        shared_max()

    def all_pv():
        for pi, (_, d) in enumerate(PATTERNS):
            pv_stage(pi, d)
        for lo in range(4):
            for hi in range(4):
                rows = pl.ds((4 * hi + lo) * BLK, BLK)
                acc = acc_s[0, rows, :] + acc_s[1, rows, :] + acc_s[2, rows, :]
                l = l_s[0, rows, :] + l_s[1, rows, :] + l_s[2, rows, :]
                mid[lo, pl.ds(hi, BLK, stride=4), :] = acc / l
        for lo in range(4):
            qp[pl.ds(lo, quarter, stride=4), :] = mid[lo]
        g = g_ref[...].astype(F32)
        o_ref[...] = (qp[...] * _silu(g)).astype(o_ref.dtype)

    _region(all_scores)
    _region(all_pv)


def _attn_prompt(q, k, v, g, bias_tbl, batch, seq):
    n_span = seq // SPAN
    n_hp = N_HEADS // 2
    cur = lambda hp, b, s: (b * n_span + s, hp)
    blk = (SPAN, LANES)
    return pl.pallas_call(
        _attn_kernel,
        grid=(n_hp, batch, n_span),
        in_specs=[
            pl.BlockSpec(blk, cur),
            pl.BlockSpec(blk, cur),
            pl.BlockSpec(blk, cur),
            pl.BlockSpec(blk, cur),
            pl.BlockSpec((None, 2, len(PATTERNS), 2 * BLK, 2 * BLK), lambda hp, b, s: (hp, 0, 0, 0, 0)),
        ],
        out_specs=pl.BlockSpec(blk, cur),
        out_shape=jax.ShapeDtypeStruct((batch * seq, D_ATTN), BF16),
        scratch_shapes=[
            pltpu.VMEM(blk, F32),
            pltpu.VMEM((2,) + blk, F32),
            pltpu.VMEM((2,) + blk, F32),
            pltpu.VMEM((4, SPAN // 4, LANES), F32),
            pltpu.VMEM((len(PATTERNS),) + blk, F32),
            pltpu.VMEM((len(PATTERNS),) + blk, F32),
            pltpu.VMEM((len(PATTERNS),) + blk, F32),
            pltpu.VMEM((len(PATTERNS),) + blk, F32),
            pltpu.VMEM((len(PATTERNS), SPAN // BLK, 2 * BLK, 2 * BLK), F32),
        ],
        compiler_params=pltpu.CompilerParams(
            dimension_semantics=("arbitrary", "arbitrary", "arbitrary"), vmem_limit_bytes=VMEM_LIMIT),
    )(q, k, v, g, bias_tbl)


def _bias_lookup(rel_bias, dist):
    bucket = _t5_bucket(dist)[..., None]
    edges = jnp.arange(N_BUCKETS)
    onehot = ((bucket >= edges) & (bucket < edges + 1)).astype(F32)
    return jnp.einsum('...b,bh->h...', onehot, rel_bias.astype(F32), precision=lax.Precision.HIGHEST)


def _prompt_bias_table(rel_bias):
    tbls = []
    for w, d in PATTERNS:
        n_chunk = 16 // d
        n = jnp.arange(BLK)
        idx = (n % (BLK // n_chunk)) * n_chunk + n // (BLK // n_chunk)
        i = idx[:, None]
        j = jnp.concatenate([idx, BLK + idx])[None, :]
        rel = i + BLK - j
        band = (rel >= 0) & (rel <= w // d)
        bias = _bias_lookup(rel_bias, jnp.maximum(rel, 0) * d) * LOG2E
        normal = jnp.where(band[None], bias, NEG)
        first = jnp.where((band & (j >= BLK))[None], bias, NEG)
        tbls.append(jnp.stack([normal, first], axis=0))
    t = jnp.stack(tbls, axis=1)
    t = t.reshape(2, len(PATTERNS), N_HEADS // 2, 2 * BLK, 2 * BLK)
    return jnp.moveaxis(t, 2, 0)


def _ssd_kernel(xbc_ref, z_ref, dt_ref, cw_ref, cb_ref, dtb_ref, alog_ref, dsk_ref, nw_ref,
                e_ref, tril_ref, s_ref, h_ref, cbuf):
    @pl.when(pl.program_id(1) == 0)
    def _():
        cbuf[:, 0:8, :] = jnp.zeros((CONV_DIM // LANES, 8, LANES), F32)
        h_ref[...] = jnp.zeros_like(h_ref)

    def body(ci, carry):
        for k in range(SSD_CHUNKS_UNROLL):
            rows = pl.ds(pl.multiple_of((ci * SSD_CHUNKS_UNROLL + k) * CHUNK, CHUNK), CHUNK)
            _ssd_chunk(xbc_ref.at[rows], z_ref.at[rows], dt_ref.at[rows], cw_ref, cb_ref, dtb_ref, alog_ref,
                       dsk_ref, nw_ref, e_ref, tril_ref, s_ref.at[rows], h_ref, cbuf)
        return carry
    lax.fori_loop(0, SSD_CHUNKS_PER_STEP // SSD_CHUNKS_UNROLL, body, 0)


def _ssd_chunk(xbc_ref, z_ref, dt_ref, cw_ref, cb_ref, dtb_ref, alog_ref, dsk_ref, nw_ref,
               e_ref, tril_ref, s_ref, h_ref, cbuf):
    n_slab = CONV_DIM // LANES

    xc_slabs = []
    for j in range(n_slab):
        cols = slice(j * LANES, (j + 1) * LANES)
        xj = xbc_ref[:, cols]
        cbuf[j, 8:8 + CHUNK, :] = xj
        acc = cb_ref[:, cols] + xj * cw_ref[CONV_W - 1:CONV_W, cols]
        for i in range(CONV_W - 1):
            tap = cbuf[pl.ds(j, 1, stride=2), pl.ds(8 - (CONV_W - 1) + i, CHUNK), :][0]
            acc = acc + tap * cw_ref[i:i + 1, cols]
        xc_slabs.append(_silu(acc))
        cbuf[j, 0:8, :] = xj[CHUNK - 8:, :]

    n_x = D_SSD // LANES
    xs = jnp.concatenate(xc_slabs[:n_x], axis=1)
    lane = lax.broadcasted_iota(jnp.int32, (CHUNK, LANES), 1)
    sub = lax.broadcasted_iota(jnp.int32, (CHUNK, LANES), 0)
    head_lane = lane < N_HEADS
    dt = jnp.where(head_lane, _softplus(dt_ref[...] + dtb_ref[...]), 0.0)
    la = dt * (-jnp.exp(alog_ref[...]))

    def hi_lo(val):
        hi = val.astype(BF16).astype(F32)
        return (hi + pltpu.roll(val - hi, N_HEADS, axis=1)).astype(BF16)

    cs2 = jnp.dot(tril_ref[...], hi_lo(la), preferred_element_type=F32)
    a_cs = jnp.where(head_lane, cs2 + pltpu.roll(cs2, LANES - N_HEADS, axis=1), 0.0)
    ea = jnp.where(head_lane, jnp.exp(a_cs), 0.0)
    dte = jnp.where(head_lane, jnp.exp(a_cs[CHUNK - 1:CHUNK, :] - a_cs), 0.0)

    expanded = jnp.dot(jnp.concatenate([hi_lo(dt), hi_lo(ea), hi_lo(dte)], axis=0), e_ref[...],
                       preferred_element_type=F32)
    dtx, eax, dtex = expanded[:CHUNK], expanded[CHUNK:2 * CHUNK], expanded[2 * CHUNK:]
    xdt_f = xs * dtx
    xdt = xdt_f.astype(BF16)
    xdte = (xdt_f * dtex).astype(BF16)

    a_cs_t = a_cs.T
    causal = sub >= lane
    is_a = lane < HEAD_DIM
    ys = []
    for g in range(N_GROUPS):
        b_g = xc_slabs[n_x + g]
        c_g = xc_slabs[n_x + N_GROUPS + g]
        b_bf = b_g.astype(BF16)
        c_bf = c_g.astype(BF16)
        cb = lax.dot_general(c_bf, b_bf, (((1,), (1,)), ((), ())), preferred_element_type=F32)
        gcols = slice(g * 512, (g + 1) * 512)
        h_prev = h_ref[:, gcols]
        y_off = jnp.dot(c_bf, h_prev.astype(BF16), preferred_element_type=F32) * eax[:, gcols]
        st = jnp.dot(b_g.T.astype(BF16), xdte[:, gcols], preferred_element_type=F32)
        h_ref[:, gcols] = h_prev * eax[CHUNK - 1:CHUNK, gcols] + st
        for hp in range(4):
            pair = []
            for which in range(2):
                h = g * 8 + hp * 2 + which
                col = jnp.sum(jnp.where(lane == h, a_cs, 0.0), axis=1, keepdims=True)
                seg = col - a_cs_t[h:h + 1, :]
                lmat = jnp.exp(jnp.where(causal, seg, NEG))
                pair.append((cb * lmat).astype(BF16))
            x_pair = xdt[:, g * 512 + hp * LANES:g * 512 + (hp + 1) * LANES]
            y_a = jnp.dot(pair[0], x_pair, preferred_element_type=F32)
            y_b = jnp.dot(pair[1], x_pair, preferred_element_type=F32)
            ys.append(jnp.where(is_a, y_a, y_b) + y_off[:, hp * LANES:(hp + 1) * LANES])
    y = jnp.concatenate(ys, axis=1) + dsk_ref[...] * xs
    zf = z_ref[...].astype(F32)
    yz = y * _silu(zf)
    var = jnp.mean(yz * yz, axis=-1, keepdims=True)
    s_ref[...] = (yz * lax.rsqrt(var + EPS) * nw_ref[...]).astype(s_ref.dtype)


def _ssd_prompt(xbc, z, dt, cw, cb, dtb, alog, dsk, nw, emat, tril, batch, seq):
    tc = CHUNK * SSD_CHUNKS_PER_STEP
    assert seq % tc == 0
    nc = seq // tc
    row = lambda b, c: (b * nc + c, 0)
    const = lambda b, c: (0, 0)
    return pl.pallas_call(
        _ssd_kernel,
        grid=(batch, nc),
        in_specs=[
            pl.BlockSpec((tc, CONV_DIM), row),
            pl.BlockSpec((tc, D_SSD), row),
            pl.BlockSpec((tc, LANES), row),
            pl.BlockSpec((CONV_W, CONV_DIM), const),
            pl.BlockSpec((1, CONV_DIM), const),
            pl.BlockSpec((1, LANES), const),
            pl.BlockSpec((1, LANES), const),
            pl.BlockSpec((1, D_SSD), const),
            pl.BlockSpec((1, D_SSD), const),
            pl.BlockSpec((LANES, D_SSD), const),
            pl.BlockSpec((CHUNK, CHUNK), const),
        ],
        out_specs=[
            pl.BlockSpec((tc, D_SSD), row),
            pl.BlockSpec((None, D_STATE, D_SSD), lambda b, c: (b, 0, 0)),
        ],
        out_shape=(jax.ShapeDtypeStruct((batch * seq, D_SSD), BF16),
                   jax.ShapeDtypeStruct((batch, D_STATE, D_SSD), F32)),
        scratch_shapes=[pltpu.VMEM((CONV_DIM // LANES, CHUNK + 8, LANES), F32)],
        compiler_params=pltpu.CompilerParams(
            dimension_semantics=("arbitrary", "arbitrary"), vmem_limit_bytes=VMEM_LIMIT),
    )(xbc, z, dt, cw, cb, dtb, alog, dsk, nw, emat, tril)


def _attn_sample_heads(hh, n_heads, heads, qt_ref, knt_ref, vnt_ref, gt_ref, k_ref, v_ref, btbl_ref, bias0_ref,
                       o_ref):
    lane = lax.broadcasted_iota(jnp.int32, (HEAD_DIM, LANES), 1)
    lane1 = lax.broadcasted_iota(jnp.int32, (1, LANES), 1)
    qt = qt_ref[...] * (HEAD_DIM ** -0.5)
    n_pat = float(len(PATTERNS))
    for j in heads:
        h = hh * n_heads + j
        pick = lane == h

        def col(val, pick=pick):
            return jnp.sum(jnp.where(pick, val, 0.0), axis=1, keepdims=True)

        qc, knc, vnc, gc = col(qt), col(knt_ref[...]), col(vnt_ref[...]), col(gt_ref[...])
        b0 = jnp.sum(jnp.where(lane1 == h, bias0_ref[...], 0.0), axis=1, keepdims=True)
        s0 = jnp.sum(qc * knc, axis=0, keepdims=True) + b0
        s = jnp.sum(k_ref[j] * qc, axis=0, keepdims=True)
        sp = [s + btbl_ref[pi, pl.ds(h, 1), :] for pi in range(len(PATTERNS))]
        m = s0
        for x in sp:
            m = jnp.maximum(m, jnp.max(x, axis=1, keepdims=True))
        p0 = n_pat * jnp.exp(s0 - m)
        pw = jnp.exp(sp[0] - m)
        for x in sp[1:]:
            pw = pw + jnp.exp(x - m)
        l = jnp.sum(pw, axis=1, keepdims=True) + p0
        oc = (jnp.sum(v_ref[j] * pw, axis=1, keepdims=True) + p0 * vnc) / l
        o_ref[...] = jnp.where(pick, oc * _silu(gc), o_ref[...])


def _sample_bias_tables(rel_bias, n_past):
    dist = n_past - jnp.arange(n_past)
    bias = _bias_lookup(rel_bias, dist)
    tbls = [jnp.where(((dist % d == 0) & (dist <= w))[None], bias, NEG) for w, d in PATTERNS]
    bias0 = _bias_lookup(rel_bias, jnp.zeros((1,), jnp.int32))
    return jnp.stack(tbls, axis=0), jnp.pad(bias0.reshape(1, N_HEADS), ((0, 0), (0, LANES - N_HEADS)))


def _ssd_sample_kernel(xbc_ref, z_ref, dt_ref, sc_ref, h_ref, cw_ref, cb_ref, dtb_ref, alog_ref, dsk_ref,
                       nw_ref, e_ref, s_ref, conv_out_ref, h_out_ref):
    xnew = xbc_ref[...]
    sc = sc_ref[...]
    acc = cb_ref[...] + xnew * cw_ref[CONV_W - 1:CONV_W, :]
    for i in range(CONV_W - 1):
        acc = acc + sc[i:i + 1, :] * cw_ref[i:i + 1, :]
    xc = _silu(acc)
    conv_out_ref[0:CONV_W - 2, :] = sc[1:CONV_W - 1, :]
    conv_out_ref[CONV_W - 2:CONV_W - 1, :] = xnew

    xs = xc[:, :D_SSD]
    lane1 = lax.broadcasted_iota(jnp.int32, (1, LANES), 1)
    dt = jnp.where(lane1 < N_HEADS, _softplus(dt_ref[...] + dtb_ref[...]), 0.0)
    da = jnp.where(lane1 < N_HEADS, jnp.exp(dt * (-jnp.exp(alog_ref[...]))), 0.0)

    def expand(val):
        v8 = jnp.broadcast_to(val, (8, LANES))
        out = jnp.zeros((8, D_SSD), F32)
        for _ in range(3):
            part = v8.astype(BF16)
            out = out + jnp.dot(part, e_ref[...], preferred_element_type=F32)
            v8 = v8 - part.astype(F32)
        return out[0:1, :]

    xdt = xs * expand(dt)
    dax = expand(da)

    lane = lax.broadcasted_iota(jnp.int32, (HEAD_DIM, LANES), 1)
    sub = lax.broadcasted_iota(jnp.int32, (HEAD_DIM, LANES), 0)
    eye2 = (lane % HEAD_DIM) == sub
    is_a = lane < HEAD_DIM

    def to_cols(row):
        mat = jnp.where(eye2, jnp.broadcast_to(row, (HEAD_DIM, LANES)), 0.0)
        col_a = jnp.sum(jnp.where(is_a, mat, 0.0), axis=1, keepdims=True)
        col_b = jnp.sum(jnp.where(is_a, 0.0, mat), axis=1, keepdims=True)
        return col_a, col_b

    y_rows = []
    for hp in range(N_HEADS // 2):
        g = hp // 4
        b_row = xc[:, D_SSD + g * D_STATE:D_SSD + (g + 1) * D_STATE]
        c_row = xc[:, D_SSD + (N_GROUPS + g) * D_STATE:D_SSD + (N_GROUPS + g + 1) * D_STATE]
        cols = slice(hp * LANES, (hp + 1) * LANES)
        x_cols = to_cols(xdt[:, cols])
        d_cols = to_cols(dax[:, cols])
        y_cols = []
        for which in range(2):
            h = hp * 2 + which
            h_new = h_ref[h] * d_cols[which] + x_cols[which] * b_row
            h_out_ref[h] = h_new
            y_cols.append(jnp.sum(h_new * c_row, axis=1, keepdims=True))
        y_mat = jnp.where(eye2, jnp.where(is_a, y_cols[0], y_cols[1]), 0.0)
        y_rows.append(jnp.sum(y_mat, axis=0, keepdims=True))
    y = jnp.concatenate(y_rows, axis=1) + dsk_ref[...] * xs
    zf = z_ref[...].astype(F32)
    yz = y * _silu(zf)
    var = jnp.mean(yz * yz, axis=-1, keepdims=True)
    s_ref[...] = (yz * lax.rsqrt(var + EPS) * nw_ref[...]).astype(s_ref.dtype)


def _ssd_sample_specs(b):
    const = lambda i: (0, 0)
    tok = lambda width: pl.BlockSpec((None, 1, width), lambda i: (i, 0, 0))
    conv_spec = pl.BlockSpec((None, CONV_W - 1, CONV_DIM), lambda i: (i, 0, 0))
    ssm_spec = pl.BlockSpec((None, N_HEADS, HEAD_DIM, D_STATE), lambda i: (i, 0, 0, 0))
    in_specs = [
        tok(CONV_DIM), tok(D_SSD), tok(LANES), conv_spec, ssm_spec,
        pl.BlockSpec((CONV_W, CONV_DIM), const),
        pl.BlockSpec((1, CONV_DIM), const),
        pl.BlockSpec((1, LANES), const),
        pl.BlockSpec((1, LANES), const),
        pl.BlockSpec((1, D_SSD), const),
        pl.BlockSpec((1, D_SSD), const),
        pl.BlockSpec((LANES, D_SSD), const),
    ]
    assert len(in_specs) == N_SSD_SAMPLE_IN
    out_shapes = (jax.ShapeDtypeStruct((b, 1, D_SSD), BF16),
                  jax.ShapeDtypeStruct((b, CONV_W - 1, CONV_DIM), F32),
                  jax.ShapeDtypeStruct((b, N_HEADS, HEAD_DIM, D_STATE), F32))
    return in_specs, [tok(D_SSD), conv_spec, ssm_spec], out_shapes


def kernel(x_prompt, x_sample, cache_win_k, cache_win_v, state_conv, state_ssm, norm_w, w_in, q_norm_w,
           k_norm_w, rel_bias, conv_w, conv_b, dt_bias, a_log, d_skip, ssd_norm_w, w_out):
    assert x_prompt.shape[-1] == D_MODEL and w_in.shape[0] == 1, "single-layer model of width 1024 only"
    batch, seq, _ = x_prompt.shape
    dec_batch, dec_seq, _ = x_sample.shape
    assert dec_seq == 1 and seq % SPAN == 0 and cache_win_k.shape[2] == WINDOW_MAX

    w_pad = jnp.pad(w_in[0], ((0, 0), (0, D_IN_PAD - D_IN_PROJ))).astype(BF16)
    w_out_b = w_out[0].astype(BF16)
    nw = norm_w[0].reshape(1, D_MODEL)
    qnw = jnp.tile(q_norm_w[0], 512 // HEAD_DIM).reshape(1, 512)
    knw = jnp.tile(k_norm_w[0], 512 // HEAD_DIM).reshape(1, 512)
    cw, cb = conv_w[0], conv_b[0].reshape(1, CONV_DIM)
    pad_heads = lambda a: jnp.pad(a.reshape(1, N_HEADS), ((0, 0), (0, LANES - N_HEADS)))
    dtb, alog = pad_heads(dt_bias[0]), pad_heads(a_log[0])
    dsk = jnp.repeat(d_skip[0], HEAD_DIM).reshape(1, D_SSD)
    snw = ssd_norm_w[0].reshape(1, D_SSD)
    erow = jnp.arange(LANES)[:, None]
    emat = ((erow % N_HEADS == (jnp.arange(D_SSD) // HEAD_DIM)[None, :]) & (erow < 2 * N_HEADS)).astype(BF16)
    tril = (jnp.arange(CHUNK)[:, None] >= jnp.arange(CHUNK)[None, :]).astype(BF16)

    xs2 = x_sample.reshape(dec_batch, D_MODEL)
    qs, ks, vs, gs, zs, xbcs, dts = _inproj(xs2, nw, w_pad, qnw, knw, tm=dec_batch)
    tok_t = lambda t: jnp.pad(jnp.swapaxes(t.astype(F32).reshape(dec_batch, N_HEADS, HEAD_DIM), 1, 2),
                              ((0, 0), (0, 0), (0, LANES - N_HEADS)))
    cache_t = lambda c: jnp.transpose(c[0], (0, 2, 3, 1))
    btbl, bias0 = _sample_bias_tables(rel_bias, cache_win_k.shape[2])
    sample_attn_args = (tok_t(qs), tok_t(ks), tok_t(vs), tok_t(gs), cache_t(cache_win_k), cache_t(cache_win_v),
                        btbl, bias0)

    xp = x_prompt.reshape(batch * seq, D_MODEL)
    nwin = min(WINDOW_MAX, seq)
    q, k, v, g, z, xbc, dt, k_win, v_win, a_t = _inproj(xp, nw, w_pad, qnw, knw, tm=256, window=(seq, nwin),
                                                        sample=sample_attn_args)
    a = _attn_prompt(q, k, v, g, _prompt_bias_table(rel_bias), batch, seq)
    s, h_fin = _ssd_prompt(xbc, z, dt, cw, cb, dtb, alog, dsk, snw, emat, tril, batch, seq)
    ssd_sample_args = (xbcs.reshape(dec_batch, 1, CONV_DIM), zs.reshape(dec_batch, 1, D_SSD),
                       dts.reshape(dec_batch, 1, LANES), state_conv[0], state_ssm[0],
                       cw, cb, dtb, alog, dsk, snw, emat)
    assert (batch * seq) % dec_batch == 0
    y_p, s_s, conv_s, h_s = _outproj(xp, a, s, w_out_b, tm=batch * seq // dec_batch, ssd_sample=ssd_sample_args)
    y_p = y_p.reshape(batch, seq, D_MODEL)
    heads = lambda t: jnp.transpose(t.reshape(1, batch, N_HEADS, HEAD_DIM, nwin), (0, 1, 4, 2, 3))
    kp, vp = heads(k_win), heads(v_win)
    cp = xbc.reshape(batch, seq, CONV_DIM)[None, :, seq - (CONV_W - 1):]
    hp = jnp.swapaxes(h_fin, 1, 2).reshape(batch, N_HEADS, HEAD_DIM, D_STATE)[None]

    a_s = jnp.swapaxes(a_t[:, :, :N_HEADS], 1, 2)
    (y_s,) = _outproj(xs2, a_s.reshape(dec_batch, D_ATTN).astype(BF16), s_s.reshape(dec_batch, D_SSD),
                      w_out_b, tm=dec_batch)
    y_s = y_s.reshape(dec_batch, 1, D_MODEL)
    k_s = ks.reshape(1, dec_batch, 1, N_HEADS, HEAD_DIM)
    v_s = vs.reshape(1, dec_batch, 1, N_HEADS, HEAD_DIM)
    return (y_p, y_s, kp, vp, cp, hp, k_s, v_s, conv_s[None], h_s[None])
```

```python
import functools
import math

import jax
import jax.numpy as jnp
from jax import lax
from jax.experimental import pallas as pl
from jax.experimental.pallas import tpu as pltpu

F32 = jnp.float32
BF16 = jnp.bfloat16

D_MODEL = 1024
D_ATTN = 1024
D_SSD = 1024
HEAD_DIM = 64
N_HEADS = 16
PATTERNS = ((128, 1), (512, 4), (2048, 16))
WINDOW_MAX = 2048
BLK = 128
N_BUCKETS = 32
D_STATE = 128
N_GROUPS = 2
CONV_W = 4
CONV_DIM = D_SSD + 2 * N_GROUPS * D_STATE
CHUNK = 128
EPS = 1e-6
D_IN_PROJ = 4 * D_ATTN + D_SSD + CONV_DIM + N_HEADS
LANES = 128
D_IN_PAD = D_IN_PROJ - N_HEADS + LANES
SPAN = BLK * 16
NEG = -1e30
LOG2E = math.log2(math.e)
SSD_CHUNKS_PER_STEP = 8
SSD_CHUNKS_UNROLL = 8
VMEM_LIMIT = 56 * 1024 * 1024


def _silu(x):
    h = 0.5 * x
    return h + h * jnp.tanh(h)


def _softplus(x):
    return jnp.maximum(x, 0.0) + jnp.log(1.0 + jnp.exp(-jnp.abs(x)))


def _region(fn):
    def body(i, carry):
        fn()
        return carry
    lax.fori_loop(0, 1 + jnp.minimum(pl.program_id(0), 0), body, 0)


def _t5_bucket(dist):
    max_exact = N_BUCKETS // 2
    d_f = jnp.maximum(dist, 1).astype(F32)
    large = max_exact + (jnp.log(d_f / max_exact) / math.log(WINDOW_MAX / max_exact)
                         * (N_BUCKETS - max_exact)).astype(jnp.int32)
    large = jnp.minimum(large, N_BUCKETS - 1)
    return jnp.where(dist < max_exact, dist, large)


def _inproj_kernel(*refs, tiles_per_seq, sample_heads):
    x_ref, nw_ref, w_ref, qnw_ref, knw_ref = refs[:5]
    n_base = 7 if tiles_per_seq else 5
    cw_ref, cb_ref = refs[5:n_base] if tiles_per_seq else (None, None)
    n_in = n_base + (8 if sample_heads else 0)
    q_ref, k_ref, v_ref, g_ref, z_ref, xbc_ref, dt_ref = refs[n_in:n_in + 7]
    kt_ref, vt_ref, tail_ref = refs[n_in + 7:n_in + 10] if tiles_per_seq else (None, None, None)
    cbuf = refs[-1] if tiles_per_seq else None
    a_ref = refs[-2 if tiles_per_seq else -1] if sample_heads else None
    sample_refs = refs[n_base:n_in] + (a_ref,)
    if tiles_per_seq:
        @pl.when(pl.program_id(0) % tiles_per_seq == 0)
        def _():
            cbuf[:, 0:8, :] = jnp.zeros((CONV_DIM // LANES, 8, LANES), F32)
    if sample_heads:
        @pl.when(pl.program_id(0) % (N_HEADS // sample_heads) == 0)
        def _():
            a_ref[...] = jnp.zeros_like(a_ref)
    x = x_ref[...]
    h = (x * nw_ref[...]).astype(BF16)
    r = lax.rsqrt(jnp.mean(x * x, axis=-1, keepdims=True) + EPS)

    def seg(c0, width):
        return jnp.dot(h, w_ref[:, c0:c0 + width], preferred_element_type=F32) * r

    is_a = lax.broadcasted_iota(jnp.int32, (x.shape[0], LANES), 1) < HEAD_DIM

    def head_rms(pj):
        p2 = pj * pj
        ss_a = jnp.sum(jnp.where(is_a, p2, 0.0), axis=1, keepdims=True)
        ss_b = jnp.sum(jnp.where(is_a, 0.0, p2), axis=1, keepdims=True)
        return jnp.where(is_a, lax.rsqrt(ss_a * (1.0 / HEAD_DIM) + EPS), lax.rsqrt(ss_b * (1.0 / HEAD_DIM) + EPS))

    for out_ref, base, hw_ref in ((q_ref, 0, qnw_ref), (k_ref, D_ATTN, knw_ref)):
        for c in range(2):
            p = seg(base + 512 * c, 512)
            rs = jnp.concatenate([head_rms(p[:, LANES * j:LANES * (j + 1)]) for j in range(512 // LANES)], axis=1)
            normed = p * rs * hw_ref[...]
            out_ref[:, 512 * c:512 * (c + 1)] = normed
            if out_ref is k_ref and kt_ref is not None:
                kt_ref[512 * c:512 * (c + 1), :] = normed.T

    def v_g_z(c):
        v_c = seg(2 * D_ATTN + 512 * c, 512)
        v_ref[:, 512 * c:512 * (c + 1)] = v_c
        if vt_ref is not None:
            vt_ref[512 * c:512 * (c + 1), :] = v_c.T
        g_ref[:, 512 * c:512 * (c + 1)] = _silu(seg(3 * D_ATTN + 512 * c, 512)).astype(g_ref.dtype)
        z_ref[:, 512 * c:512 * (c + 1)] = _silu(seg(4 * D_ATTN + 512 * c, 512)).astype(z_ref.dtype)

    tm = x.shape[0]

    def xbc(c):
        val = seg(5 * D_ATTN + 512 * c, 512)
        if not tiles_per_seq:
            xbc_ref[:, 512 * c:512 * (c + 1)] = val
            return
        for jj in range(512 // LANES):
            j = c * (512 // LANES) + jj
            cols = slice(j * LANES, (j + 1) * LANES)
            xj = val[:, jj * LANES:(jj + 1) * LANES]
            cbuf[j, 8:8 + tm, :] = xj
            acc = cb_ref[:, cols] + xj * cw_ref[CONV_W - 1:CONV_W, cols]
            for i in range(CONV_W - 1):
                tap = cbuf[pl.ds(j, 1, stride=2), pl.ds(8 - (CONV_W - 1) + i, tm), :][0]
                acc = acc + tap * cw_ref[i:i + 1, cols]
            xbc_ref[:, cols] = _silu(acc)
            tail_ref[:, cols] = xj[tm - 8:, :]
            cbuf[j, 0:8, :] = xj[tm - 8:, :]

    def xbc_tail():
        xbc(1)
        xbc(2)
        dt_ref[...] = seg(5 * D_ATTN + CONV_DIM, LANES)

    plain = [functools.partial(v_g_z, 0), functools.partial(v_g_z, 1), functools.partial(xbc, 0), xbc_tail]
    for i, segment in enumerate(plain):
        if sample_heads:
            hh = pl.program_id(0) % (N_HEADS // sample_heads)
            _attn_sample_heads(hh, sample_heads, range(i, sample_heads, len(plain)), *sample_refs)
        segment()


def _inproj(x2d, nw, w_pad, qnw, knw, tm, window=None, sample=None):
    t = x2d.shape[0]
    row = lambda i: (i, 0)
    const = lambda i: (0, 0)
    sample_heads, sample_specs, sample_shapes, sample_out = 0, [], (), []
    if sample is not None:
        dec_batch, n_past = sample[4].shape[0], sample[4].shape[3]
        steps = t // tm
        assert steps % dec_batch == 0 and N_HEADS % (steps // dec_batch) == 0
        n_sub = steps // dec_batch
        sample_heads = N_HEADS // n_sub
        tok = pl.BlockSpec((None, HEAD_DIM, LANES), lambda i: (i // n_sub, 0, 0))
        cache = pl.BlockSpec((None, sample_heads, HEAD_DIM, n_past), lambda i: (i // n_sub, i % n_sub, 0, 0))
        sample_specs = [tok, tok, tok, tok, cache, cache,
                        pl.BlockSpec((len(PATTERNS), N_HEADS, n_past), lambda i: (0, 0, 0)),
                        pl.BlockSpec((1, LANES), const)]
        sample_shapes = (jax.ShapeDtypeStruct((dec_batch, HEAD_DIM, LANES), F32),)
        sample_out = [tok]
    win_shapes, win_specs, conv_args, conv_specs, scratch, per_seq = (), [], (), [], [], 0
    if window is not None:
        seq, nwin, conv_w, conv_b = window
        assert seq % tm == 0 and nwin % tm == 0
        per_seq, first = seq // tm, (seq - nwin) // tm
        win_spec = pl.BlockSpec((None, D_ATTN, tm),
                                lambda i: (i // per_seq, 0, jnp.maximum(i % per_seq - first, 0)))
        win_shapes = (jax.ShapeDtypeStruct((t // seq, D_ATTN, nwin), F32),) * 2 + (
            jax.ShapeDtypeStruct((t // seq, 8, CONV_DIM), F32),)
        win_specs = [win_spec, win_spec, pl.BlockSpec((None, 8, CONV_DIM), lambda i: (i // per_seq, 0, 0))]
        conv_args = (conv_w, conv_b)
        conv_specs = [pl.BlockSpec((CONV_W, CONV_DIM), const), pl.BlockSpec((1, CONV_DIM), const)]
        scratch = [pltpu.VMEM((CONV_DIM // LANES, tm + 8, LANES), F32)]
    outs = (
        jax.ShapeDtypeStruct((t, D_ATTN), F32),
        jax.ShapeDtypeStruct((t, D_ATTN), F32),
        jax.ShapeDtypeStruct((t, D_ATTN), F32),
        jax.ShapeDtypeStruct((t, D_ATTN), BF16),
        jax.ShapeDtypeStruct((t, D_SSD), BF16),
        jax.ShapeDtypeStruct((t, CONV_DIM), F32),
        jax.ShapeDtypeStruct((t, LANES), F32),
    )
    return pl.pallas_call(
        functools.partial(_inproj_kernel, tiles_per_seq=per_seq, sample_heads=sample_heads),
        grid=(t // tm,),
        in_specs=[
            pl.BlockSpec((tm, D_MODEL), row),
            pl.BlockSpec((1, D_MODEL), const),
            pl.BlockSpec((D_MODEL, D_IN_PAD), const, pipeline_mode=pl.Buffered(1)),
            pl.BlockSpec((1, 512), const),
            pl.BlockSpec((1, 512), const),
        ] + conv_specs + sample_specs,
        out_specs=[
            pl.BlockSpec((tm, D_ATTN), row),
            pl.BlockSpec((tm, D_ATTN), row),
            pl.BlockSpec((tm, D_ATTN), row),
            pl.BlockSpec((tm, D_ATTN), row),
            pl.BlockSpec((tm, D_SSD), row),
            pl.BlockSpec((tm, CONV_DIM), row),
            pl.BlockSpec((tm, LANES), row),
        ] + win_specs + sample_out,
        out_shape=outs + win_shapes + sample_shapes,
        scratch_shapes=scratch,
        compiler_params=pltpu.CompilerParams(
            dimension_semantics=("arbitrary",), vmem_limit_bytes=VMEM_LIMIT),
    )(x2d, nw, w_pad, qnw, knw, *conv_args, *(sample or ()))


N_SSD_SAMPLE_IN = 12


def _outproj_kernel(x_ref, a_ref, s_ref, w_ref, *rest):
    if len(rest) > 1:
        _ssd_sample_kernel(*rest[:N_SSD_SAMPLE_IN], *rest[N_SSD_SAMPLE_IN + 1:])
    y_ref = rest[N_SSD_SAMPLE_IN] if len(rest) > 1 else rest[0]
    y_ref[...] = (x_ref[...]
                  + jnp.dot(a_ref[...], w_ref[0:D_ATTN, :], preferred_element_type=F32)
                  + jnp.dot(s_ref[...], w_ref[D_ATTN:, :], preferred_element_type=F32))


def _outproj(x2d, a, s, w_out_b, tm, ssd_sample=None):
    t = x2d.shape[0]
    row = lambda i: (i, 0)
    extra_in, extra_out, extra_shapes = [], [], ()
    if ssd_sample is not None:
        assert t // tm == ssd_sample[0].shape[0], "one sampled sequence per grid step"
        extra_in, extra_out, extra_shapes = _ssd_sample_specs(ssd_sample[0].shape[0])
    return pl.pallas_call(
        _outproj_kernel,
        grid=(t // tm,),
        in_specs=[
            pl.BlockSpec((tm, D_MODEL), row),
            pl.BlockSpec((tm, D_ATTN), row),
            pl.BlockSpec((tm, D_SSD), row),
            pl.BlockSpec((D_ATTN + D_SSD, D_MODEL), lambda i: (0, 0), pipeline_mode=pl.Buffered(1)),
        ] + extra_in,
        out_specs=[pl.BlockSpec((tm, D_MODEL), row)] + extra_out,
        out_shape=(jax.ShapeDtypeStruct((t, D_MODEL), F32),) + extra_shapes,
        compiler_params=pltpu.CompilerParams(
            dimension_semantics=("arbitrary",), vmem_limit_bytes=VMEM_LIMIT),
    )(x2d, a, s, w_out_b, *(ssd_sample or ()))


def _attn_kernel(q_ref, k_ref, v_ref, g_ref, bias_ref, o_ref, qp, kp, vp, mid, ma_s, mb_s, l_s, acc_s, s_buf):
    s_idx = pl.program_id(2)
    first = (s_idx == 0).astype(jnp.int32)
    slot = s_idx % 2
    pslot = 1 - slot
    lane = lax.broadcasted_iota(jnp.int32, (BLK, LANES), 1)
    is_a = lane < HEAD_DIM
    qscale = HEAD_DIM ** -0.5 * LOG2E
    ones = jnp.ones((2 * BLK, LANES), BF16)
    quarter = SPAN // 4

    @pl.when(s_idx == 0)
    def _():
        kp[pslot] = jnp.zeros((SPAN, LANES), F32)
        vp[pslot] = jnp.zeros((SPAN, LANES), F32)

    def regroup(src_ref, store, scale=None):
        for lo in range(4):
            mid[lo] = src_ref[pl.ds(lo, quarter, stride=4), :]
        for lo in range(4):
            for hi in range(4):
                val = mid[lo, pl.ds(hi, BLK, stride=4), :]
                store(4 * hi + lo, val if scale is None else val * scale)

    def store_q(r, val):
        qp[pl.ds(r * BLK, BLK), :] = val

    def store_k(r, val):
        kp[slot, pl.ds(r * BLK, BLK), :] = val

    def store_v(r, val):
        vp[slot, pl.ds(r * BLK, BLK), :] = val

    def chunks(d, blk):
        n_chunk = 16 // d
        length = BLK // n_chunk
        u, r = blk // d, blk % d
        is_u0 = u == 0
        cur = [pl.multiple_of((d * c + r) * BLK + u * length, 8) for c in range(n_chunk)]
        back = jnp.where(is_u0, BLK - length, (u - 1) * length)
        prev = [pl.multiple_of((d * c + r) * BLK + back, 8) for c in range(n_chunk)]
        return is_u0, length, cur, jnp.where(is_u0, pslot, slot), prev

    def gather(read, starts, length):
        return jnp.concatenate([read(pl.ds(st, length)) for st in starts], axis=0)

    def scatter(write, starts, length, val):
        for c, st in enumerate(starts):
            write(pl.ds(st, length), val[c * length:(c + 1) * length])

    n_blk = SPAN // BLK
    zero = jnp.int32(0)

    def qk_stage(pi, d):
        for j in range(n_blk):
            is_u0, length, cur, prev_slot, prev = chunks(d, zero + j)
            qsc = gather(lambda rows: qp[rows, :], cur, length)
            qq = jnp.concatenate([jnp.where(is_a, qsc, 0.0), jnp.where(is_a, 0.0, qsc)], axis=0).astype(BF16)
            kk = jnp.concatenate([gather(lambda rows: kp[prev_slot, rows, :], prev, length),
                                  gather(lambda rows: kp[slot, rows, :], cur, length)], axis=0).astype(BF16)
            bias = bias_ref[jnp.where(is_u0, first, 0), pi]
            s = lax.dot_general(qq, kk, (((1,), (1,)), ((), ())), preferred_element_type=F32) + bias
            s_buf[pi, j] = s
            m = jnp.broadcast_to(jnp.max(s, axis=1, keepdims=True), (2 * BLK, LANES))
            scatter(lambda rows, v: ma_s.__setitem__((pi, rows, slice(None)), v), cur, length, m[:BLK])
            scatter(lambda rows, v: mb_s.__setitem__((pi, rows, slice(None)), v), cur, length, m[BLK:])

    def pv_stage(pi, d):
        for j in range(n_blk):
            _, length, cur, prev_slot, prev = chunks(d, zero + j)
            m = jnp.concatenate([gather(lambda rows: ma_s[0, rows, :], cur, length),
                                 gather(lambda rows: mb_s[0, rows, :], cur, length)], axis=0)
            p = jnp.exp2(s_buf[pi, j] - jnp.concatenate([m, m], axis=1)).astype(BF16)
            vv = jnp.concatenate([gather(lambda rows: vp[prev_slot, rows, :], prev, length),
                                  gather(lambda rows: vp[slot, rows, :], cur, length)], axis=0).astype(BF16)
            ol = jnp.dot(p, jnp.concatenate([vv, ones], axis=1), preferred_element_type=F32)
            for ref, val in ((acc_s, jnp.where(is_a, ol[:BLK, :LANES], ol[BLK:, :LANES])),
                             (l_s, jnp.where(is_a, ol[:BLK, LANES:], ol[BLK:, LANES:]))):
                scatter(lambda rows, v, ref=ref: ref.__setitem__((pi, rows, slice(None)), v), cur, length, val)

    def shared_max():
        for m_ref in (ma_s, mb_s):
            m_ref[0] = jnp.maximum(jnp.maximum(m_ref[0], m_ref[1]), m_ref[2])

    def all_scores():
        regroup(q_ref, store_q, qscale)
        regroup(k_ref, store_k)
        for pi, (_, d) in reversed(list(enumerate(PATTERNS))):
            qk_stage(pi, d)
        regroup(v_ref, store_v)
        shared_max()

    def all_pv():
        for pi, (_, d) in enumerate(PATTERNS):
            pv_stage(pi, d)
        for lo in range(4):
            for hi in range(4):
                rows = pl.ds((4 * hi + lo) * BLK, BLK)
                acc = acc_s[0, rows, :] + acc_s[1, rows, :] + acc_s[2, rows, :]
                l = l_s[0, rows, :] + l_s[1, rows, :] + l_s[2, rows, :]
                mid[lo, pl.ds(hi, BLK, stride=4), :] = acc / l
        for lo in range(4):
            qp[pl.ds(lo, quarter, stride=4), :] = mid[lo]
        o_ref[...] = (qp[...] * g_ref[...].astype(F32)).astype(o_ref.dtype)

    _region(all_scores)
    _region(all_pv)


def _attn_prompt(q, k, v, g, bias_tbl, batch, seq):
    n_span = seq // SPAN
    n_hp = N_HEADS // 2
    cur = lambda hp, b, s: (b * n_span + s, hp)
    blk = (SPAN, LANES)
    return pl.pallas_call(
        _attn_kernel,
        grid=(n_hp, batch, n_span),
        in_specs=[
            pl.BlockSpec(blk, cur),
            pl.BlockSpec(blk, cur),
            pl.BlockSpec(blk, cur),
            pl.BlockSpec(blk, cur),
            pl.BlockSpec((None, 2, len(PATTERNS), 2 * BLK, 2 * BLK), lambda hp, b, s: (hp, 0, 0, 0, 0)),
        ],
        out_specs=pl.BlockSpec(blk, cur),
        out_shape=jax.ShapeDtypeStruct((batch * seq, D_ATTN), BF16),
        scratch_shapes=[
            pltpu.VMEM(blk, F32),
            pltpu.VMEM((2,) + blk, F32),
            pltpu.VMEM((2,) + blk, F32),
            pltpu.VMEM((4, SPAN // 4, LANES), F32),
            pltpu.VMEM((len(PATTERNS),) + blk, F32),
            pltpu.VMEM((len(PATTERNS),) + blk, F32),
            pltpu.VMEM((len(PATTERNS),) + blk, F32),
            pltpu.VMEM((len(PATTERNS),) + blk, F32),
            pltpu.VMEM((len(PATTERNS), SPAN // BLK, 2 * BLK, 2 * BLK), F32),
        ],
        compiler_params=pltpu.CompilerParams(
            dimension_semantics=("arbitrary", "arbitrary", "arbitrary"), vmem_limit_bytes=VMEM_LIMIT),
    )(q, k, v, g, bias_tbl)


def _bias_lookup(rel_bias, dist):
    bucket = _t5_bucket(dist)[..., None]
    edges = jnp.arange(N_BUCKETS)
    onehot = ((bucket >= edges) & (bucket < edges + 1)).astype(F32)
    return jnp.einsum('...b,bh->h...', onehot, rel_bias.astype(F32), precision=lax.Precision.HIGHEST)


def _prompt_bias_table(rel_bias):
    tbls = []
    for w, d in PATTERNS:
        n_chunk = 16 // d
        n = jnp.arange(BLK)
        idx = (n % (BLK // n_chunk)) * n_chunk + n // (BLK // n_chunk)
        i = idx[:, None]
        j = jnp.concatenate([idx, BLK + idx])[None, :]
        rel = i + BLK - j
        band = (rel >= 0) & (rel <= w // d)
        bias = _bias_lookup(rel_bias, jnp.maximum(rel, 0) * d) * LOG2E
        normal = jnp.where(band[None], bias, NEG)
        first = jnp.where((band & (j >= BLK))[None], bias, NEG)
        tbls.append(jnp.stack([normal, first], axis=0))
    t = jnp.stack(tbls, axis=1)
    t = t.reshape(2, len(PATTERNS), N_HEADS // 2, 2 * BLK, 2 * BLK)
    return jnp.moveaxis(t, 2, 0)


def _ssd_kernel(xc_ref, z_ref, dt_ref, dtb_ref, alog_ref, dsk_ref, nw_ref,
                e_ref, tril_ref, s_ref, h_ref, ex_buf):
    @pl.when(pl.program_id(1) == 0)
    def _():
        h_ref[...] = jnp.zeros_like(h_ref)

    def body(ci, carry):
        for k in range(SSD_CHUNKS_UNROLL):
            rows = pl.ds(pl.multiple_of((ci * SSD_CHUNKS_UNROLL + k) * CHUNK, CHUNK), CHUNK)
            _ssd_chunk(xc_ref.at[rows], z_ref.at[rows], dt_ref.at[rows], dtb_ref, alog_ref,
                       dsk_ref, nw_ref, e_ref, tril_ref, s_ref.at[rows], h_ref, ex_buf.at[k])
        return carry
    lax.fori_loop(0, SSD_CHUNKS_PER_STEP // SSD_CHUNKS_UNROLL, body, 0)


class _Slabs:
    def __init__(self, ref):
        self.ref = ref

    def __getitem__(self, j):
        return self.ref[:, j * LANES:(j + 1) * LANES]


def _ssd_chunk(xc_ref, z_ref, dt_ref, dtb_ref, alog_ref, dsk_ref, nw_ref, e_ref, tril_ref, s_ref, h_ref, ex_buf):
    xc_buf = _Slabs(xc_ref)
    n_x = D_SSD // LANES
    xs = jnp.concatenate([xc_buf[j] for j in range(n_x)], axis=1)
    lane = lax.broadcasted_iota(jnp.int32, (CHUNK, LANES), 1)
    sub = lax.broadcasted_iota(jnp.int32, (CHUNK, LANES), 0)
    head_lane = lane < N_HEADS
    dt = jnp.where(head_lane, _softplus(dt_ref[...] + dtb_ref[...]), 0.0)
    la = dt * (-jnp.exp(alog_ref[...]))

    def hi_lo(val):
        hi = val.astype(BF16).astype(F32)
        return (hi + pltpu.roll(val - hi, N_HEADS, axis=1)).astype(BF16)

    cs2 = jnp.dot(tril_ref[...], hi_lo(la), preferred_element_type=F32)
    a_cs = jnp.where(head_lane, cs2 + pltpu.roll(cs2, LANES - N_HEADS, axis=1), 0.0)
    ea = jnp.where(head_lane, jnp.exp(a_cs), 0.0)
    dte = jnp.where(head_lane, jnp.exp(a_cs[CHUNK - 1:CHUNK, :] - a_cs), 0.0)

    ex_buf[...] = jnp.dot(jnp.concatenate([hi_lo(dt), hi_lo(ea), hi_lo(dte)], axis=0), e_ref[...],
                          preferred_element_type=F32)
    dtx, dtex = ex_buf[0:CHUNK, :], ex_buf[2 * CHUNK:3 * CHUNK, :]
    eax = ex_buf.at[CHUNK:2 * CHUNK]
    xdt_f = xs * dtx
    xdt = xdt_f.astype(BF16)
    xdte = (xdt_f * dtex).astype(BF16)

    a_cs_t = a_cs.T
    causal = sub >= lane
    is_a = lane < HEAD_DIM
    ys = []
    for g in range(N_GROUPS):
        b_g = xc_buf[n_x + g]
        c_g = xc_buf[n_x + N_GROUPS + g]
        b_bf = b_g.astype(BF16)
        c_bf = c_g.astype(BF16)
        cb = lax.dot_general(c_bf, b_bf, (((1,), (1,)), ((), ())), preferred_element_type=F32)
        gcols = slice(g * 512, (g + 1) * 512)
        h_prev = h_ref[:, gcols]
        y_off = jnp.dot(c_bf, h_prev.astype(BF16), preferred_element_type=F32) * eax[:, gcols]
        st = jnp.dot(b_g.T.astype(BF16), xdte[:, gcols], preferred_element_type=F32)
        h_ref[:, gcols] = h_prev * eax[CHUNK - 1:CHUNK, gcols] + st
        for hp in range(4):
            pair = []
            for which in range(2):
                h = g * 8 + hp * 2 + which
                col = jnp.sum(jnp.where(lane == h, a_cs, 0.0), axis=1, keepdims=True)
                seg = col - a_cs_t[h:h + 1, :]
                lmat = jnp.exp(jnp.where(causal, seg, NEG))
                pair.append((cb * lmat).astype(BF16))
            x_pair = xdt[:, g * 512 + hp * LANES:g * 512 + (hp + 1) * LANES]
            y_a = jnp.dot(pair[0], x_pair, preferred_element_type=F32)
            y_b = jnp.dot(pair[1], x_pair, preferred_element_type=F32)
            ys.append(jnp.where(is_a, y_a, y_b) + y_off[:, hp * LANES:(hp + 1) * LANES])
    y = jnp.concatenate(ys, axis=1) + dsk_ref[...] * jnp.concatenate([xc_buf[j] for j in range(n_x)], axis=1)
    yz = y * z_ref[...].astype(F32)
    var = jnp.mean(yz * yz, axis=-1, keepdims=True)
    s_ref[...] = (yz * lax.rsqrt(var + EPS) * nw_ref[...]).astype(s_ref.dtype)


def _ssd_prompt(xc, z, dt, dtb, alog, dsk, nw, emat, tril, batch, seq):
    tc = CHUNK * SSD_CHUNKS_PER_STEP
    assert seq % tc == 0
    nc = seq // tc
    row = lambda b, c: (b * nc + c, 0)
    const = lambda b, c: (0, 0)
    return pl.pallas_call(
        _ssd_kernel,
        grid=(batch, nc),
        in_specs=[
            pl.BlockSpec((tc, CONV_DIM), row),
            pl.BlockSpec((tc, D_SSD), row),
            pl.BlockSpec((tc, LANES), row),
            pl.BlockSpec((1, LANES), const),
            pl.BlockSpec((1, LANES), const),
            pl.BlockSpec((1, D_SSD), const),
            pl.BlockSpec((1, D_SSD), const),
            pl.BlockSpec((LANES, D_SSD), const),
            pl.BlockSpec((CHUNK, CHUNK), const),
        ],
        out_specs=[
            pl.BlockSpec((tc, D_SSD), row),
            pl.BlockSpec((None, D_STATE, D_SSD), lambda b, c: (b, 0, 0)),
        ],
        out_shape=(jax.ShapeDtypeStruct((batch * seq, D_SSD), BF16),
                   jax.ShapeDtypeStruct((batch, D_STATE, D_SSD), F32)),
        scratch_shapes=[pltpu.VMEM((SSD_CHUNKS_UNROLL, 3 * CHUNK, D_SSD), F32)],
        compiler_params=pltpu.CompilerParams(
            dimension_semantics=("arbitrary", "arbitrary"), vmem_limit_bytes=VMEM_LIMIT),
    )(xc, z, dt, dtb, alog, dsk, nw, emat, tril)


def _attn_sample_heads(hh, n_heads, heads, qt_ref, knt_ref, vnt_ref, gt_ref, k_ref, v_ref, btbl_ref, bias0_ref,
                       o_ref):
    lane = lax.broadcasted_iota(jnp.int32, (HEAD_DIM, LANES), 1)
    lane1 = lax.broadcasted_iota(jnp.int32, (1, LANES), 1)
    qt = qt_ref[...] * (HEAD_DIM ** -0.5)
    n_pat = float(len(PATTERNS))
    for j in heads:
        h = hh * n_heads + j
        pick = lane == h

        def col(val, pick=pick):
            return jnp.sum(jnp.where(pick, val, 0.0), axis=1, keepdims=True)

        qc, knc, vnc, gc = col(qt), col(knt_ref[...]), col(vnt_ref[...]), col(gt_ref[...])
        b0 = jnp.sum(jnp.where(lane1 == h, bias0_ref[...], 0.0), axis=1, keepdims=True)
        s0 = jnp.sum(qc * knc, axis=0, keepdims=True) + b0
        s = jnp.sum(k_ref[j] * qc, axis=0, keepdims=True)
        sp = [s + btbl_ref[pi, pl.ds(h, 1), :] for pi in range(len(PATTERNS))]
        m = s0
        for x in sp:
            m = jnp.maximum(m, jnp.max(x, axis=1, keepdims=True))
        p0 = n_pat * jnp.exp(s0 - m)
        pw = jnp.exp(sp[0] - m)
        for x in sp[1:]:
            pw = pw + jnp.exp(x - m)
        l = jnp.sum(pw, axis=1, keepdims=True) + p0
        oc = (jnp.sum(v_ref[j] * pw, axis=1, keepdims=True) + p0 * vnc) / l
        o_ref[...] = jnp.where(pick, oc * gc, o_ref[...])


def _sample_bias_tables(rel_bias, n_past):
    dist = n_past - jnp.arange(n_past)
    bias = _bias_lookup(rel_bias, dist)
    tbls = [jnp.where(((dist % d == 0) & (dist <= w))[None], bias, NEG) for w, d in PATTERNS]
    bias0 = _bias_lookup(rel_bias, jnp.zeros((1,), jnp.int32))
    return jnp.stack(tbls, axis=0), jnp.pad(bias0.reshape(1, N_HEADS), ((0, 0), (0, LANES - N_HEADS)))


def _ssd_sample_kernel(xbc_ref, z_ref, dt_ref, sc_ref, h_ref, cw_ref, cb_ref, dtb_ref, alog_ref, dsk_ref,
                       nw_ref, e_ref, s_ref, conv_out_ref, h_out_ref):
    xnew = xbc_ref[...]
    sc = sc_ref[...]
    acc = cb_ref[...] + xnew * cw_ref[CONV_W - 1:CONV_W, :]
    for i in range(CONV_W - 1):
        acc = acc + sc[i:i + 1, :] * cw_ref[i:i + 1, :]
    xc = _silu(acc)
    conv_out_ref[0:CONV_W - 2, :] = sc[1:CONV_W - 1, :]
    conv_out_ref[CONV_W - 2:CONV_W - 1, :] = xnew

    xs = xc[:, :D_SSD]
    lane1 = lax.broadcasted_iota(jnp.int32, (1, LANES), 1)
    dt = jnp.where(lane1 < N_HEADS, _softplus(dt_ref[...] + dtb_ref[...]), 0.0)
    da = jnp.where(lane1 < N_HEADS, jnp.exp(dt * (-jnp.exp(alog_ref[...]))), 0.0)

    def expand(val):
        v8 = jnp.broadcast_to(val, (8, LANES))
        out = jnp.zeros((8, D_SSD), F32)
        for _ in range(3):
            part = v8.astype(BF16)
            out = out + jnp.dot(part, e_ref[...], preferred_element_type=F32)
            v8 = v8 - part.astype(F32)
        return out[0:1, :]

    xdt = xs * expand(dt)
    dax = expand(da)

    lane = lax.broadcasted_iota(jnp.int32, (HEAD_DIM, LANES), 1)
    sub = lax.broadcasted_iota(jnp.int32, (HEAD_DIM, LANES), 0)
    eye2 = (lane % HEAD_DIM) == sub
    is_a = lane < HEAD_DIM

    def to_cols(row):
        mat = jnp.where(eye2, jnp.broadcast_to(row, (HEAD_DIM, LANES)), 0.0)
        col_a = jnp.sum(jnp.where(is_a, mat, 0.0), axis=1, keepdims=True)
        col_b = jnp.sum(jnp.where(is_a, 0.0, mat), axis=1, keepdims=True)
        return col_a, col_b

    y_rows = []
    for hp in range(N_HEADS // 2):
        g = hp // 4
        b_row = xc[:, D_SSD + g * D_STATE:D_SSD + (g + 1) * D_STATE]
        c_row = xc[:, D_SSD + (N_GROUPS + g) * D_STATE:D_SSD + (N_GROUPS + g + 1) * D_STATE]
        cols = slice(hp * LANES, (hp + 1) * LANES)
        x_cols = to_cols(xdt[:, cols])
        d_cols = to_cols(dax[:, cols])
        y_cols = []
        for which in range(2):
            h = hp * 2 + which
            h_new = h_ref[h] * d_cols[which] + x_cols[which] * b_row
            h_out_ref[h] = h_new
            y_cols.append(jnp.sum(h_new * c_row, axis=1, keepdims=True))
        y_mat = jnp.where(eye2, jnp.where(is_a, y_cols[0], y_cols[1]), 0.0)
        y_rows.append(jnp.sum(y_mat, axis=0, keepdims=True))
    y = jnp.concatenate(y_rows, axis=1) + dsk_ref[...] * xs
    yz = y * z_ref[...].astype(F32)
    var = jnp.mean(yz * yz, axis=-1, keepdims=True)
    s_ref[...] = (yz * lax.rsqrt(var + EPS) * nw_ref[...]).astype(s_ref.dtype)


def _ssd_sample_specs(b):
    const = lambda i: (0, 0)
    tok = lambda width: pl.BlockSpec((None, 1, width), lambda i: (i, 0, 0))
    conv_spec = pl.BlockSpec((None, CONV_W - 1, CONV_DIM), lambda i: (i, 0, 0))
    ssm_spec = pl.BlockSpec((None, N_HEADS, HEAD_DIM, D_STATE), lambda i: (i, 0, 0, 0))
    in_specs = [
        tok(CONV_DIM), tok(D_SSD), tok(LANES), conv_spec, ssm_spec,
        pl.BlockSpec((CONV_W, CONV_DIM), const),
        pl.BlockSpec((1, CONV_DIM), const),
        pl.BlockSpec((1, LANES), const),
        pl.BlockSpec((1, LANES), const),
        pl.BlockSpec((1, D_SSD), const),
        pl.BlockSpec((1, D_SSD), const),
        pl.BlockSpec((LANES, D_SSD), const),
    ]
    assert len(in_specs) == N_SSD_SAMPLE_IN
    out_shapes = (jax.ShapeDtypeStruct((b, 1, D_SSD), BF16),
                  jax.ShapeDtypeStruct((b, CONV_W - 1, CONV_DIM), F32),
                  jax.ShapeDtypeStruct((b, N_HEADS, HEAD_DIM, D_STATE), F32))
    return in_specs, [tok(D_SSD), conv_spec, ssm_spec], out_shapes


def kernel(x_prompt, x_sample, cache_win_k, cache_win_v, state_conv, state_ssm, norm_w, w_in, q_norm_w,
           k_norm_w, rel_bias, conv_w, conv_b, dt_bias, a_log, d_skip, ssd_norm_w, w_out):
    assert x_prompt.shape[-1] == D_MODEL and w_in.shape[0] == 1, "single-layer model of width 1024 only"
    batch, seq, _ = x_prompt.shape
    dec_batch, dec_seq, _ = x_sample.shape
    assert dec_seq == 1 and seq % SPAN == 0 and cache_win_k.shape[2] == WINDOW_MAX

    w_pad = jnp.pad(w_in[0], ((0, 0), (0, D_IN_PAD - D_IN_PROJ))).astype(BF16)
    w_out_b = w_out[0].astype(BF16)
    nw = norm_w[0].reshape(1, D_MODEL)
    qnw = jnp.tile(q_norm_w[0], 512 // HEAD_DIM).reshape(1, 512)
    knw = jnp.tile(k_norm_w[0], 512 // HEAD_DIM).reshape(1, 512)
    cw, cb = conv_w[0], conv_b[0].reshape(1, CONV_DIM)
    pad_heads = lambda a: jnp.pad(a.reshape(1, N_HEADS), ((0, 0), (0, LANES - N_HEADS)))
    dtb, alog = pad_heads(dt_bias[0]), pad_heads(a_log[0])
    dsk = jnp.repeat(d_skip[0], HEAD_DIM).reshape(1, D_SSD)
    snw = ssd_norm_w[0].reshape(1, D_SSD)
    erow = jnp.arange(LANES)[:, None]
    emat = ((erow % N_HEADS == (jnp.arange(D_SSD) // HEAD_DIM)[None, :]) & (erow < 2 * N_HEADS)).astype(BF16)
    tril = (jnp.arange(CHUNK)[:, None] >= jnp.arange(CHUNK)[None, :]).astype(BF16)

    xs2 = x_sample.reshape(dec_batch, D_MODEL)
    qs, ks, vs, gs, zs, xbcs, dts = _inproj(xs2, nw, w_pad, qnw, knw, tm=dec_batch)
    tok_t = lambda t: jnp.pad(jnp.swapaxes(t.astype(F32).reshape(dec_batch, N_HEADS, HEAD_DIM), 1, 2),
                              ((0, 0), (0, 0), (0, LANES - N_HEADS)))
    cache_t = lambda c: jnp.transpose(c[0], (0, 2, 3, 1))
    btbl, bias0 = _sample_bias_tables(rel_bias, cache_win_k.shape[2])
    sample_attn_args = (tok_t(qs), tok_t(ks), tok_t(vs), tok_t(gs), cache_t(cache_win_k), cache_t(cache_win_v),
                        btbl, bias0)

    xp = x_prompt.reshape(batch * seq, D_MODEL)
    nwin = min(WINDOW_MAX, seq)
    q, k, v, g, z, xc, dt, k_win, v_win, xbc_tail, a_t = _inproj(
        xp, nw, w_pad, qnw, knw, tm=256, window=(seq, nwin, cw, cb), sample=sample_attn_args)
    a = _attn_prompt(q, k, v, g, _prompt_bias_table(rel_bias), batch, seq)
    s, h_fin = _ssd_prompt(xc, z, dt, dtb, alog, dsk, snw, emat, tril, batch, seq)
    ssd_sample_args = (xbcs.reshape(dec_batch, 1, CONV_DIM), zs.reshape(dec_batch, 1, D_SSD),
                       dts.reshape(dec_batch, 1, LANES), state_conv[0], state_ssm[0],
                       cw, cb, dtb, alog, dsk, snw, emat)
    assert (batch * seq) % dec_batch == 0
    y_p, s_s, conv_s, h_s = _outproj(xp, a, s, w_out_b, tm=batch * seq // dec_batch, ssd_sample=ssd_sample_args)
    y_p = y_p.reshape(batch, seq, D_MODEL)
    heads = lambda t: jnp.transpose(t.reshape(1, batch, N_HEADS, HEAD_DIM, nwin), (0, 1, 4, 2, 3))
    kp, vp = heads(k_win), heads(v_win)
    cp = xbc_tail[None, :, 8 - (CONV_W - 1):]
    hp = jnp.swapaxes(h_fin, 1, 2).reshape(batch, N_HEADS, HEAD_DIM, D_STATE)[None]

    a_s = jnp.swapaxes(a_t[:, :, :N_HEADS], 1, 2)
    (y_s,) = _outproj(xs2, a_s.reshape(dec_batch, D_ATTN).astype(BF16), s_s.reshape(dec_batch, D_SSD),
                      w_out_b, tm=dec_batch)
    y_s = y_s.reshape(dec_batch, 1, D_MODEL)
    k_s = ks.reshape(1, dec_batch, 1, N_HEADS, HEAD_DIM)
    v_s = vs.reshape(1, dec_batch, 1, N_HEADS, HEAD_DIM)
    return (y_p, y_s, kp, vp, cp, hp, k_s, v_s, conv_s[None], h_s[None])
```

```python
import functools
import math

import jax
import jax.numpy as jnp
from jax import lax
from jax.experimental import pallas as pl
from jax.experimental.pallas import tpu as pltpu

F32 = jnp.float32
BF16 = jnp.bfloat16

D_MODEL = 1024
D_ATTN = 1024
D_SSD = 1024
HEAD_DIM = 64
N_HEADS = 16
PATTERNS = ((128, 1), (512, 4), (2048, 16))
WINDOW_MAX = 2048
BLK = 128
N_BUCKETS = 32
D_STATE = 128
N_GROUPS = 2
CONV_W = 4
CONV_DIM = D_SSD + 2 * N_GROUPS * D_STATE
CHUNK = 128
EPS = 1e-6
D_IN_PROJ = 4 * D_ATTN + D_SSD + CONV_DIM + N_HEADS
LANES = 128
D_IN_PAD = D_IN_PROJ - N_HEADS + LANES
SPAN = BLK * 16
NEG = -1e30
LOG2E = math.log2(math.e)
SSD_CHUNKS_PER_STEP = 8
SSD_CHUNKS_UNROLL = 8
VMEM_LIMIT = 56 * 1024 * 1024


def _silu(x):
    h = 0.5 * x
    return h + h * jnp.tanh(h)


def _softplus(x):
    return jnp.maximum(x, 0.0) + jnp.log(1.0 + jnp.exp(-jnp.abs(x)))


def _region(fn):
    def body(i, carry):
        fn()
        return carry
    lax.fori_loop(0, 1 + jnp.minimum(pl.program_id(0), 0), body, 0)


def _t5_bucket(dist):
    max_exact = N_BUCKETS // 2
    d_f = jnp.maximum(dist, 1).astype(F32)
    large = max_exact + (jnp.log(d_f / max_exact) / math.log(WINDOW_MAX / max_exact)
                         * (N_BUCKETS - max_exact)).astype(jnp.int32)
    large = jnp.minimum(large, N_BUCKETS - 1)
    return jnp.where(dist < max_exact, dist, large)


def _inproj_kernel(*refs, has_window, sample_heads):
    x_ref, nw_ref, w_ref, qnw_ref, knw_ref = refs[:5]
    n_in = 5 + (8 if sample_heads else 0)
    q_ref, k_ref, v_ref, g_ref, z_ref, xbc_ref, dt_ref = refs[n_in:n_in + 7]
    kt_ref, vt_ref = refs[n_in + 7:n_in + 9] if has_window else (None, None)
    if sample_heads:
        @pl.when(pl.program_id(0) % (N_HEADS // sample_heads) == 0)
        def _():
            refs[-1][...] = jnp.zeros_like(refs[-1])
    x = x_ref[...]
    h = (x * nw_ref[...]).astype(BF16)
    r = lax.rsqrt(jnp.mean(x * x, axis=-1, keepdims=True) + EPS)

    def seg(c0, width):
        return jnp.dot(h, w_ref[:, c0:c0 + width], preferred_element_type=F32) * r

    is_a = lax.broadcasted_iota(jnp.int32, (x.shape[0], LANES), 1) < HEAD_DIM

    def head_rms(pj):
        p2 = pj * pj
        ss_a = jnp.sum(jnp.where(is_a, p2, 0.0), axis=1, keepdims=True)
        ss_b = jnp.sum(jnp.where(is_a, 0.0, p2), axis=1, keepdims=True)
        return jnp.where(is_a, lax.rsqrt(ss_a * (1.0 / HEAD_DIM) + EPS), lax.rsqrt(ss_b * (1.0 / HEAD_DIM) + EPS))

    for out_ref, base, hw_ref in ((q_ref, 0, qnw_ref), (k_ref, D_ATTN, knw_ref)):
        for c in range(2):
            p = seg(base + 512 * c, 512)
            rs = jnp.concatenate([head_rms(p[:, LANES * j:LANES * (j + 1)]) for j in range(512 // LANES)], axis=1)
            normed = p * rs * hw_ref[...]
            out_ref[:, 512 * c:512 * (c + 1)] = normed
            if out_ref is k_ref and kt_ref is not None:
                kt_ref[512 * c:512 * (c + 1), :] = normed.T

    def v_g_z(c):
        v_c = seg(2 * D_ATTN + 512 * c, 512)
        v_ref[:, 512 * c:512 * (c + 1)] = v_c
        if vt_ref is not None:
            vt_ref[512 * c:512 * (c + 1), :] = v_c.T
        g_ref[:, 512 * c:512 * (c + 1)] = _silu(seg(3 * D_ATTN + 512 * c, 512)).astype(g_ref.dtype)
        z_ref[:, 512 * c:512 * (c + 1)] = _silu(seg(4 * D_ATTN + 512 * c, 512)).astype(z_ref.dtype)

    def xbc(c):
        xbc_ref[:, 512 * c:512 * (c + 1)] = seg(5 * D_ATTN + 512 * c, 512)

    def xbc_tail():
        xbc(1)
        xbc(2)
        dt_ref[...] = seg(5 * D_ATTN + CONV_DIM, LANES)

    plain = [functools.partial(v_g_z, 0), functools.partial(v_g_z, 1), functools.partial(xbc, 0), xbc_tail]
    for i, segment in enumerate(plain):
        if sample_heads:
            hh = pl.program_id(0) % (N_HEADS // sample_heads)
            _attn_sample_heads(hh, sample_heads, range(i, sample_heads, len(plain)), *refs[5:n_in], refs[-1])
        segment()


def _inproj(x2d, nw, w_pad, qnw, knw, tm, window=None, sample=None):
    t = x2d.shape[0]
    row = lambda i: (i, 0)
    const = lambda i: (0, 0)
    sample_heads, sample_specs, sample_shapes, sample_out = 0, [], (), []
    if sample is not None:
        dec_batch, n_past = sample[4].shape[0], sample[4].shape[3]
        steps = t // tm
        assert steps % dec_batch == 0 and N_HEADS % (steps // dec_batch) == 0
        n_sub = steps // dec_batch
        sample_heads = N_HEADS // n_sub
        tok = pl.BlockSpec((None, HEAD_DIM, LANES), lambda i: (i // n_sub, 0, 0))
        cache = pl.BlockSpec((None, sample_heads, HEAD_DIM, n_past), lambda i: (i // n_sub, i % n_sub, 0, 0))
        sample_specs = [tok, tok, tok, tok, cache, cache,
                        pl.BlockSpec((len(PATTERNS), N_HEADS, n_past), lambda i: (0, 0, 0)),
                        pl.BlockSpec((1, LANES), const)]
        sample_shapes = (jax.ShapeDtypeStruct((dec_batch, HEAD_DIM, LANES), F32),)
        sample_out = [tok]
    win_shapes, win_specs = (), []
    if window is not None:
        seq, nwin = window
        assert seq % tm == 0 and nwin % tm == 0
        per_seq, first = seq // tm, (seq - nwin) // tm
        win_spec = pl.BlockSpec((None, D_ATTN, tm),
                                lambda i: (i // per_seq, 0, jnp.maximum(i % per_seq - first, 0)))
        win_shapes = (jax.ShapeDtypeStruct((t // seq, D_ATTN, nwin), F32),) * 2
        win_specs = [win_spec, win_spec]
    outs = (
        jax.ShapeDtypeStruct((t, D_ATTN), F32),
        jax.ShapeDtypeStruct((t, D_ATTN), F32),
        jax.ShapeDtypeStruct((t, D_ATTN), F32),
        jax.ShapeDtypeStruct((t, D_ATTN), BF16),
        jax.ShapeDtypeStruct((t, D_SSD), BF16),
        jax.ShapeDtypeStruct((t, CONV_DIM), F32),
        jax.ShapeDtypeStruct((t, LANES), F32),
    )
    return pl.pallas_call(
        functools.partial(_inproj_kernel, has_window=window is not None, sample_heads=sample_heads),
        grid=(t // tm,),
        in_specs=[
            pl.BlockSpec((tm, D_MODEL), row),
            pl.BlockSpec((1, D_MODEL), const),
            pl.BlockSpec((D_MODEL, D_IN_PAD), const, pipeline_mode=pl.Buffered(1)),
            pl.BlockSpec((1, 512), const),
            pl.BlockSpec((1, 512), const),
        ] + sample_specs,
        out_specs=[
            pl.BlockSpec((tm, D_ATTN), row),
            pl.BlockSpec((tm, D_ATTN), row),
            pl.BlockSpec((tm, D_ATTN), row),
            pl.BlockSpec((tm, D_ATTN), row),
            pl.BlockSpec((tm, D_SSD), row),
            pl.BlockSpec((tm, CONV_DIM), row),
            pl.BlockSpec((tm, LANES), row),
        ] + win_specs + sample_out,
        out_shape=outs + win_shapes + sample_shapes,
        compiler_params=pltpu.CompilerParams(
            dimension_semantics=("arbitrary",), vmem_limit_bytes=VMEM_LIMIT),
    )(x2d, nw, w_pad, qnw, knw, *(sample or ()))


N_SSD_SAMPLE_IN = 12


def _outproj_kernel(x_ref, a_ref, s_ref, w_ref, *rest):
    if len(rest) > 1:
        _ssd_sample_kernel(*rest[:N_SSD_SAMPLE_IN], *rest[N_SSD_SAMPLE_IN + 1:])
    y_ref = rest[N_SSD_SAMPLE_IN] if len(rest) > 1 else rest[0]
    y_ref[...] = (x_ref[...]
                  + jnp.dot(a_ref[...], w_ref[0:D_ATTN, :], preferred_element_type=F32)
                  + jnp.dot(s_ref[...], w_ref[D_ATTN:, :], preferred_element_type=F32))


def _outproj(x2d, a, s, w_out_b, tm, ssd_sample=None):
    t = x2d.shape[0]
    row = lambda i: (i, 0)
    extra_in, extra_out, extra_shapes = [], [], ()
    if ssd_sample is not None:
        assert t // tm == ssd_sample[0].shape[0], "one sampled sequence per grid step"
        extra_in, extra_out, extra_shapes = _ssd_sample_specs(ssd_sample[0].shape[0])
    return pl.pallas_call(
        _outproj_kernel,
        grid=(t // tm,),
        in_specs=[
            pl.BlockSpec((tm, D_MODEL), row),
            pl.BlockSpec((tm, D_ATTN), row),
            pl.BlockSpec((tm, D_SSD), row),
            pl.BlockSpec((D_ATTN + D_SSD, D_MODEL), lambda i: (0, 0), pipeline_mode=pl.Buffered(1)),
        ] + extra_in,
        out_specs=[pl.BlockSpec((tm, D_MODEL), row)] + extra_out,
        out_shape=(jax.ShapeDtypeStruct((t, D_MODEL), F32),) + extra_shapes,
        compiler_params=pltpu.CompilerParams(
            dimension_semantics=("arbitrary",), vmem_limit_bytes=VMEM_LIMIT),
    )(x2d, a, s, w_out_b, *(ssd_sample or ()))


def _attn_kernel(q_ref, k_ref, v_ref, g_ref, bias_ref, o_ref, qp, kp, vp, mid, ma_s, mb_s, l_s, acc_s, s_buf):
    s_idx = pl.program_id(2)
    first = (s_idx == 0).astype(jnp.int32)
    slot = s_idx % 2
    pslot = 1 - slot
    lane = lax.broadcasted_iota(jnp.int32, (BLK, LANES), 1)
    is_a = lane < HEAD_DIM
    qscale = HEAD_DIM ** -0.5 * LOG2E
    ones = jnp.ones((2 * BLK, LANES), BF16)
    quarter = SPAN // 4

    @pl.when(s_idx == 0)
    def _():
        kp[pslot] = jnp.zeros((SPAN, LANES), F32)
        vp[pslot] = jnp.zeros((SPAN, LANES), F32)

    def regroup(src_ref, store, scale=None):
        for lo in range(4):
            mid[lo] = src_ref[pl.ds(lo, quarter, stride=4), :]
        for lo in range(4):
            for hi in range(4):
                val = mid[lo, pl.ds(hi, BLK, stride=4), :]
                store(4 * hi + lo, val if scale is None else val * scale)

    def store_q(r, val):
        qp[pl.ds(r * BLK, BLK), :] = val

    def store_k(r, val):
        kp[slot, pl.ds(r * BLK, BLK), :] = val

    def store_v(r, val):
        vp[slot, pl.ds(r * BLK, BLK), :] = val

    def chunks(d, blk):
        n_chunk = 16 // d
        length = BLK // n_chunk
        u, r = blk // d, blk % d
        is_u0 = u == 0
        cur = [pl.multiple_of((d * c + r) * BLK + u * length, 8) for c in range(n_chunk)]
        back = jnp.where(is_u0, BLK - length, (u - 1) * length)
        prev = [pl.multiple_of((d * c + r) * BLK + back, 8) for c in range(n_chunk)]
        return is_u0, length, cur, jnp.where(is_u0, pslot, slot), prev

    def gather(read, starts, length):
        return jnp.concatenate([read(pl.ds(st, length)) for st in starts], axis=0)

    def scatter(write, starts, length, val):
        for c, st in enumerate(starts):
            write(pl.ds(st, length), val[c * length:(c + 1) * length])

    n_blk = SPAN // BLK
    zero = jnp.int32(0)

    def qk_stage(pi, d):
        for j in range(n_blk):
            is_u0, length, cur, prev_slot, prev = chunks(d, zero + j)
            qsc = gather(lambda rows: qp[rows, :], cur, length)
            qq = jnp.concatenate([jnp.where(is_a, qsc, 0.0), jnp.where(is_a, 0.0, qsc)], axis=0).astype(BF16)
            kk = jnp.concatenate([gather(lambda rows: kp[prev_slot, rows, :], prev, length),
                                  gather(lambda rows: kp[slot, rows, :], cur, length)], axis=0).astype(BF16)
            bias = bias_ref[jnp.where(is_u0, first, 0), pi]
            s = lax.dot_general(qq, kk, (((1,), (1,)), ((), ())), preferred_element_type=F32) + bias
            s_buf[pi, j] = s
            m = jnp.broadcast_to(jnp.max(s, axis=1, keepdims=True), (2 * BLK, LANES))
            scatter(lambda rows, v: ma_s.__setitem__((pi, rows, slice(None)), v), cur, length, m[:BLK])
            scatter(lambda rows, v: mb_s.__setitem__((pi, rows, slice(None)), v), cur, length, m[BLK:])

    def pv_stage(pi, d):
        for j in range(n_blk):
            _, length, cur, prev_slot, prev = chunks(d, zero + j)
            m = jnp.concatenate([gather(lambda rows: ma_s[0, rows, :], cur, length),
                                 gather(lambda rows: mb_s[0, rows, :], cur, length)], axis=0)
            p = jnp.exp2(s_buf[pi, j] - jnp.concatenate([m, m], axis=1)).astype(BF16)
            vv = jnp.concatenate([gather(lambda rows: vp[prev_slot, rows, :], prev, length),
                                  gather(lambda rows: vp[slot, rows, :], cur, length)], axis=0).astype(BF16)
            ol = jnp.dot(p, jnp.concatenate([vv, ones], axis=1), preferred_element_type=F32)
            for ref, val in ((acc_s, jnp.where(is_a, ol[:BLK, :LANES], ol[BLK:, :LANES])),
                             (l_s, jnp.where(is_a, ol[:BLK, LANES:], ol[BLK:, LANES:]))):
                scatter(lambda rows, v, ref=ref: ref.__setitem__((pi, rows, slice(None)), v), cur, length, val)

    def shared_max():
        for m_ref in (ma_s, mb_s):
            m_ref[0] = jnp.maximum(jnp.maximum(m_ref[0], m_ref[1]), m_ref[2])

    def all_scores():
        regroup(q_ref, store_q, qscale)
        regroup(k_ref, store_k)
        for pi, (_, d) in reversed(list(enumerate(PATTERNS))):
            qk_stage(pi, d)
        regroup(v_ref, store_v)
        shared_max()

    def all_pv():
        for pi, (_, d) in enumerate(PATTERNS):
            pv_stage(pi, d)
        for lo in range(4):
            for hi in range(4):
                rows = pl.ds((4 * hi + lo) * BLK, BLK)
                acc = acc_s[0, rows, :] + acc_s[1, rows, :] + acc_s[2, rows, :]
                l = l_s[0, rows, :] + l_s[1, rows, :] + l_s[2, rows, :]
                mid[lo, pl.ds(hi, BLK, stride=4), :] = acc / l
        for lo in range(4):
            qp[pl.ds(lo, quarter, stride=4), :] = mid[lo]
        o_ref[...] = (qp[...] * g_ref[...].astype(F32)).astype(o_ref.dtype)

    _region(all_scores)
    _region(all_pv)


def _attn_prompt(q, k, v, g, bias_tbl, batch, seq):
    n_span = seq // SPAN
    n_hp = N_HEADS // 2
    cur = lambda hp, b, s: (b * n_span + s, hp)
    blk = (SPAN, LANES)
    return pl.pallas_call(
        _attn_kernel,
        grid=(n_hp, batch, n_span),
        in_specs=[
            pl.BlockSpec(blk, cur),
            pl.BlockSpec(blk, cur),
            pl.BlockSpec(blk, cur),
            pl.BlockSpec(blk, cur),
            pl.BlockSpec((None, 2, len(PATTERNS), 2 * BLK, 2 * BLK), lambda hp, b, s: (hp, 0, 0, 0, 0)),
        ],
        out_specs=pl.BlockSpec(blk, cur),
        out_shape=jax.ShapeDtypeStruct((batch * seq, D_ATTN), BF16),
        scratch_shapes=[
            pltpu.VMEM(blk, F32),
            pltpu.VMEM((2,) + blk, F32),
            pltpu.VMEM((2,) + blk, F32),
            pltpu.VMEM((4, SPAN // 4, LANES), F32),
            pltpu.VMEM((len(PATTERNS),) + blk, F32),
            pltpu.VMEM((len(PATTERNS),) + blk, F32),
            pltpu.VMEM((len(PATTERNS),) + blk, F32),
            pltpu.VMEM((len(PATTERNS),) + blk, F32),
            pltpu.VMEM((len(PATTERNS), SPAN // BLK, 2 * BLK, 2 * BLK), F32),
        ],
        compiler_params=pltpu.CompilerParams(
            dimension_semantics=("arbitrary", "arbitrary", "arbitrary"), vmem_limit_bytes=VMEM_LIMIT),
    )(q, k, v, g, bias_tbl)


def _bias_lookup(rel_bias, dist):
    bucket = _t5_bucket(dist)[..., None]
    edges = jnp.arange(N_BUCKETS)
    onehot = ((bucket >= edges) & (bucket < edges + 1)).astype(F32)
    return jnp.einsum('...b,bh->h...', onehot, rel_bias.astype(F32), precision=lax.Precision.HIGHEST)


def _prompt_bias_table(rel_bias):
    tbls = []
    for w, d in PATTERNS:
        n_chunk = 16 // d
        n = jnp.arange(BLK)
        idx = (n % (BLK // n_chunk)) * n_chunk + n // (BLK // n_chunk)
        i = idx[:, None]
        j = jnp.concatenate([idx, BLK + idx])[None, :]
        rel = i + BLK - j
        band = (rel >= 0) & (rel <= w // d)
        bias = _bias_lookup(rel_bias, jnp.maximum(rel, 0) * d) * LOG2E
        normal = jnp.where(band[None], bias, NEG)
        first = jnp.where((band & (j >= BLK))[None], bias, NEG)
        tbls.append(jnp.stack([normal, first], axis=0))
    t = jnp.stack(tbls, axis=1)
    t = t.reshape(2, len(PATTERNS), N_HEADS // 2, 2 * BLK, 2 * BLK)
    return jnp.moveaxis(t, 2, 0)


def _ssd_kernel(xbc_ref, z_ref, dt_ref, cw_ref, cb_ref, dtb_ref, alog_ref, dsk_ref, nw_ref,
                e_ref, tril_ref, s_ref, h_ref, cbuf):
    @pl.when(pl.program_id(1) == 0)
    def _():
        cbuf[:, 0:8, :] = jnp.zeros((CONV_DIM // LANES, 8, LANES), F32)
        h_ref[...] = jnp.zeros_like(h_ref)

    def body(ci, carry):
        for k in range(SSD_CHUNKS_UNROLL):
            rows = pl.ds(pl.multiple_of((ci * SSD_CHUNKS_UNROLL + k) * CHUNK, CHUNK), CHUNK)
            _ssd_chunk(xbc_ref.at[rows], z_ref.at[rows], dt_ref.at[rows], cw_ref, cb_ref, dtb_ref, alog_ref,
                       dsk_ref, nw_ref, e_ref, tril_ref, s_ref.at[rows], h_ref, cbuf)
        return carry
    lax.fori_loop(0, SSD_CHUNKS_PER_STEP // SSD_CHUNKS_UNROLL, body, 0)


def _ssd_chunk(xbc_ref, z_ref, dt_ref, cw_ref, cb_ref, dtb_ref, alog_ref, dsk_ref, nw_ref,
               e_ref, tril_ref, s_ref, h_ref, cbuf):
    n_slab = CONV_DIM // LANES

    xc_slabs = []
    for j in range(n_slab):
        cols = slice(j * LANES, (j + 1) * LANES)
        xj = xbc_ref[:, cols]
        cbuf[j, 8:8 + CHUNK, :] = xj
        acc = cb_ref[:, cols] + xj * cw_ref[CONV_W - 1:CONV_W, cols]
        for i in range(CONV_W - 1):
            tap = cbuf[pl.ds(j, 1, stride=2), pl.ds(8 - (CONV_W - 1) + i, CHUNK), :][0]
            acc = acc + tap * cw_ref[i:i + 1, cols]
        xc_slabs.append(_silu(acc))
        cbuf[j, 0:8, :] = xj[CHUNK - 8:, :]

    n_x = D_SSD // LANES
    xs = jnp.concatenate(xc_slabs[:n_x], axis=1)
    lane = lax.broadcasted_iota(jnp.int32, (CHUNK, LANES), 1)
    sub = lax.broadcasted_iota(jnp.int32, (CHUNK, LANES), 0)
    head_lane = lane < N_HEADS
    dt = jnp.where(head_lane, _softplus(dt_ref[...] + dtb_ref[...]), 0.0)
    la = dt * (-jnp.exp(alog_ref[...]))

    def hi_lo(val):
        hi = val.astype(BF16).astype(F32)
        return (hi + pltpu.roll(val - hi, N_HEADS, axis=1)).astype(BF16)

    cs2 = jnp.dot(tril_ref[...], hi_lo(la), preferred_element_type=F32)
    a_cs = jnp.where(head_lane, cs2 + pltpu.roll(cs2, LANES - N_HEADS, axis=1), 0.0)
    ea = jnp.where(head_lane, jnp.exp(a_cs), 0.0)
    dte = jnp.where(head_lane, jnp.exp(a_cs[CHUNK - 1:CHUNK, :] - a_cs), 0.0)

    expanded = jnp.dot(jnp.concatenate([hi_lo(dt), hi_lo(ea), hi_lo(dte)], axis=0), e_ref[...],
                       preferred_element_type=F32)
    dtx, eax, dtex = expanded[:CHUNK], expanded[CHUNK:2 * CHUNK], expanded[2 * CHUNK:]
    xdt_f = xs * dtx
    xdt = xdt_f.astype(BF16)
    xdte = (xdt_f * dtex).astype(BF16)

    a_cs_t = a_cs.T
    causal = sub >= lane
    is_a = lane < HEAD_DIM
    ys = []
    for g in range(N_GROUPS):
        b_g = xc_slabs[n_x + g]
        c_g = xc_slabs[n_x + N_GROUPS + g]
        b_bf = b_g.astype(BF16)
        c_bf = c_g.astype(BF16)
        cb = lax.dot_general(c_bf, b_bf, (((1,), (1,)), ((), ())), preferred_element_type=F32)
        gcols = slice(g * 512, (g + 1) * 512)
        h_prev = h_ref[:, gcols]
        y_off = jnp.dot(c_bf, h_prev.astype(BF16), preferred_element_type=F32) * eax[:, gcols]
        st = jnp.dot(b_g.T.astype(BF16), xdte[:, gcols], preferred_element_type=F32)
        h_ref[:, gcols] = h_prev * eax[CHUNK - 1:CHUNK, gcols] + st
        for hp in range(4):
            pair = []
            for which in range(2):
                h = g * 8 + hp * 2 + which
                col = jnp.sum(jnp.where(lane == h, a_cs, 0.0), axis=1, keepdims=True)
                seg = col - a_cs_t[h:h + 1, :]
                lmat = jnp.exp(jnp.where(causal, seg, NEG))
                pair.append((cb * lmat).astype(BF16))
            x_pair = xdt[:, g * 512 + hp * LANES:g * 512 + (hp + 1) * LANES]
            y_a = jnp.dot(pair[0], x_pair, preferred_element_type=F32)
            y_b = jnp.dot(pair[1], x_pair, preferred_element_type=F32)
            ys.append(jnp.where(is_a, y_a, y_b) + y_off[:, hp * LANES:(hp + 1) * LANES])
    y = jnp.concatenate(ys, axis=1) + dsk_ref[...] * xs
    yz = y * z_ref[...].astype(F32)
    var = jnp.mean(yz * yz, axis=-1, keepdims=True)
    s_ref[...] = (yz * lax.rsqrt(var + EPS) * nw_ref[...]).astype(s_ref.dtype)


def _ssd_prompt(xbc, z, dt, cw, cb, dtb, alog, dsk, nw, emat, tril, batch, seq):
    tc = CHUNK * SSD_CHUNKS_PER_STEP
    assert seq % tc == 0
    nc = seq // tc
    row = lambda b, c: (b * nc + c, 0)
    const = lambda b, c: (0, 0)
    return pl.pallas_call(
        _ssd_kernel,
        grid=(batch, nc),
        in_specs=[
            pl.BlockSpec((tc, CONV_DIM), row),
            pl.BlockSpec((tc, D_SSD), row),
            pl.BlockSpec((tc, LANES), row),
            pl.BlockSpec((CONV_W, CONV_DIM), const),
            pl.BlockSpec((1, CONV_DIM), const),
            pl.BlockSpec((1, LANES), const),
            pl.BlockSpec((1, LANES), const),
            pl.BlockSpec((1, D_SSD), const),
            pl.BlockSpec((1, D_SSD), const),
            pl.BlockSpec((LANES, D_SSD), const),
            pl.BlockSpec((CHUNK, CHUNK), const),
        ],
        out_specs=[
            pl.BlockSpec((tc, D_SSD), row),
            pl.BlockSpec((None, D_STATE, D_SSD), lambda b, c: (b, 0, 0)),
        ],
        out_shape=(jax.ShapeDtypeStruct((batch * seq, D_SSD), BF16),
                   jax.ShapeDtypeStruct((batch, D_STATE, D_SSD), F32)),
        scratch_shapes=[pltpu.VMEM((CONV_DIM // LANES, CHUNK + 8, LANES), F32)],
        compiler_params=pltpu.CompilerParams(
            dimension_semantics=("arbitrary", "arbitrary"), vmem_limit_bytes=VMEM_LIMIT),
    )(xbc, z, dt, cw, cb, dtb, alog, dsk, nw, emat, tril)


def _attn_sample_heads(hh, n_heads, heads, qt_ref, knt_ref, vnt_ref, gt_ref, k_ref, v_ref, btbl_ref, bias0_ref,
                       o_ref):
    lane = lax.broadcasted_iota(jnp.int32, (HEAD_DIM, LANES), 1)
    lane1 = lax.broadcasted_iota(jnp.int32, (1, LANES), 1)
    qt = qt_ref[...] * (HEAD_DIM ** -0.5)
    n_pat = float(len(PATTERNS))
    for j in heads:
        h = hh * n_heads + j
        pick = lane == h

        def col(val, pick=pick):
            return jnp.sum(jnp.where(pick, val, 0.0), axis=1, keepdims=True)

        qc, knc, vnc, gc = col(qt), col(knt_ref[...]), col(vnt_ref[...]), col(gt_ref[...])
        b0 = jnp.sum(jnp.where(lane1 == h, bias0_ref[...], 0.0), axis=1, keepdims=True)
        s0 = jnp.sum(qc * knc, axis=0, keepdims=True) + b0
        s = jnp.sum(k_ref[j] * qc, axis=0, keepdims=True)
        sp = [s + btbl_ref[pi, pl.ds(h, 1), :] for pi in range(len(PATTERNS))]
        m = s0
        for x in sp:
            m = jnp.maximum(m, jnp.max(x, axis=1, keepdims=True))
        p0 = n_pat * jnp.exp(s0 - m)
        pw = jnp.exp(sp[0] - m)
        for x in sp[1:]:
            pw = pw + jnp.exp(x - m)
        l = jnp.sum(pw, axis=1, keepdims=True) + p0
        oc = (jnp.sum(v_ref[j] * pw, axis=1, keepdims=True) + p0 * vnc) / l
        o_ref[...] = jnp.where(pick, oc * gc, o_ref[...])


def _sample_bias_tables(rel_bias, n_past):
    dist = n_past - jnp.arange(n_past)
    bias = _bias_lookup(rel_bias, dist)
    tbls = [jnp.where(((dist % d == 0) & (dist <= w))[None], bias, NEG) for w, d in PATTERNS]
    bias0 = _bias_lookup(rel_bias, jnp.zeros((1,), jnp.int32))
    return jnp.stack(tbls, axis=0), jnp.pad(bias0.reshape(1, N_HEADS), ((0, 0), (0, LANES - N_HEADS)))


def _ssd_sample_kernel(xbc_ref, z_ref, dt_ref, sc_ref, h_ref, cw_ref, cb_ref, dtb_ref, alog_ref, dsk_ref,
                       nw_ref, e_ref, s_ref, conv_out_ref, h_out_ref):
    xnew = xbc_ref[...]
    sc = sc_ref[...]
    acc = cb_ref[...] + xnew * cw_ref[CONV_W - 1:CONV_W, :]
    for i in range(CONV_W - 1):
        acc = acc + sc[i:i + 1, :] * cw_ref[i:i + 1, :]
    xc = _silu(acc)
    conv_out_ref[0:CONV_W - 2, :] = sc[1:CONV_W - 1, :]
    conv_out_ref[CONV_W - 2:CONV_W - 1, :] = xnew

    xs = xc[:, :D_SSD]
    lane1 = lax.broadcasted_iota(jnp.int32, (1, LANES), 1)
    dt = jnp.where(lane1 < N_HEADS, _softplus(dt_ref[...] + dtb_ref[...]), 0.0)
    da = jnp.where(lane1 < N_HEADS, jnp.exp(dt * (-jnp.exp(alog_ref[...]))), 0.0)

    def expand(val):
        v8 = jnp.broadcast_to(val, (8, LANES))
        out = jnp.zeros((8, D_SSD), F32)
        for _ in range(3):
            part = v8.astype(BF16)
            out = out + jnp.dot(part, e_ref[...], preferred_element_type=F32)
            v8 = v8 - part.astype(F32)
        return out[0:1, :]

    xdt = xs * expand(dt)
    dax = expand(da)

    lane = lax.broadcasted_iota(jnp.int32, (HEAD_DIM, LANES), 1)
    sub = lax.broadcasted_iota(jnp.int32, (HEAD_DIM, LANES), 0)
    eye2 = (lane % HEAD_DIM) == sub
    is_a = lane < HEAD_DIM

    def to_cols(row):
        mat = jnp.where(eye2, jnp.broadcast_to(row, (HEAD_DIM, LANES)), 0.0)
        col_a = jnp.sum(jnp.where(is_a, mat, 0.0), axis=1, keepdims=True)
        col_b = jnp.sum(jnp.where(is_a, 0.0, mat), axis=1, keepdims=True)
        return col_a, col_b

    y_rows = []
    for hp in range(N_HEADS // 2):
        g = hp // 4
        b_row = xc[:, D_SSD + g * D_STATE:D_SSD + (g + 1) * D_STATE]
        c_row = xc[:, D_SSD + (N_GROUPS + g) * D_STATE:D_SSD + (N_GROUPS + g + 1) * D_STATE]
        cols = slice(hp * LANES, (hp + 1) * LANES)
        x_cols = to_cols(xdt[:, cols])
        d_cols = to_cols(dax[:, cols])
        y_cols = []
        for which in range(2):
            h = hp * 2 + which
            h_new = h_ref[h] * d_cols[which] + x_cols[which] * b_row
            h_out_ref[h] = h_new
            y_cols.append(jnp.sum(h_new * c_row, axis=1, keepdims=True))
        y_mat = jnp.where(eye2, jnp.where(is_a, y_cols[0], y_cols[1]), 0.0)
        y_rows.append(jnp.sum(y_mat, axis=0, keepdims=True))
    y = jnp.concatenate(y_rows, axis=1) + dsk_ref[...] * xs
    yz = y * z_ref[...].astype(F32)
    var = jnp.mean(yz * yz, axis=-1, keepdims=True)
    s_ref[...] = (yz * lax.rsqrt(var + EPS) * nw_ref[...]).astype(s_ref.dtype)


def _ssd_sample_specs(b):
    const = lambda i: (0, 0)
    tok = lambda width: pl.BlockSpec((None, 1, width), lambda i: (i, 0, 0))
    conv_spec = pl.BlockSpec((None, CONV_W - 1, CONV_DIM), lambda i: (i, 0, 0))
    ssm_spec = pl.BlockSpec((None, N_HEADS, HEAD_DIM, D_STATE), lambda i: (i, 0, 0, 0))
    in_specs = [
        tok(CONV_DIM), tok(D_SSD), tok(LANES), conv_spec, ssm_spec,
        pl.BlockSpec((CONV_W, CONV_DIM), const),
        pl.BlockSpec((1, CONV_DIM), const),
        pl.BlockSpec((1, LANES), const),
        pl.BlockSpec((1, LANES), const),
        pl.BlockSpec((1, D_SSD), const),
        pl.BlockSpec((1, D_SSD), const),
        pl.BlockSpec((LANES, D_SSD), const),
    ]
    assert len(in_specs) == N_SSD_SAMPLE_IN
    out_shapes = (jax.ShapeDtypeStruct((b, 1, D_SSD), BF16),
                  jax.ShapeDtypeStruct((b, CONV_W - 1, CONV_DIM), F32),
                  jax.ShapeDtypeStruct((b, N_HEADS, HEAD_DIM, D_STATE), F32))
    return in_specs, [tok(D_SSD), conv_spec, ssm_spec], out_shapes


def kernel(x_prompt, x_sample, cache_win_k, cache_win_v, state_conv, state_ssm, norm_w, w_in, q_norm_w,
           k_norm_w, rel_bias, conv_w, conv_b, dt_bias, a_log, d_skip, ssd_norm_w, w_out):
    assert x_prompt.shape[-1] == D_MODEL and w_in.shape[0] == 1, "single-layer model of width 1024 only"
    batch, seq, _ = x_prompt.shape
    dec_batch, dec_seq, _ = x_sample.shape
    assert dec_seq == 1 and seq % SPAN == 0 and cache_win_k.shape[2] == WINDOW_MAX

    w_pad = jnp.pad(w_in[0], ((0, 0), (0, D_IN_PAD - D_IN_PROJ))).astype(BF16)
    w_out_b = w_out[0].astype(BF16)
    nw = norm_w[0].reshape(1, D_MODEL)
    qnw = jnp.tile(q_norm_w[0], 512 // HEAD_DIM).reshape(1, 512)
    knw = jnp.tile(k_norm_w[0], 512 // HEAD_DIM).reshape(1, 512)
    cw, cb = conv_w[0], conv_b[0].reshape(1, CONV_DIM)
    pad_heads = lambda a: jnp.pad(a.reshape(1, N_HEADS), ((0, 0), (0, LANES - N_HEADS)))
    dtb, alog = pad_heads(dt_bias[0]), pad_heads(a_log[0])
    dsk = jnp.repeat(d_skip[0], HEAD_DIM).reshape(1, D_SSD)
    snw = ssd_norm_w[0].reshape(1, D_SSD)
    erow = jnp.arange(LANES)[:, None]
    emat = ((erow % N_HEADS == (jnp.arange(D_SSD) // HEAD_DIM)[None, :]) & (erow < 2 * N_HEADS)).astype(BF16)
    tril = (jnp.arange(CHUNK)[:, None] >= jnp.arange(CHUNK)[None, :]).astype(BF16)

    xs2 = x_sample.reshape(dec_batch, D_MODEL)
    qs, ks, vs, gs, zs, xbcs, dts = _inproj(xs2, nw, w_pad, qnw, knw, tm=dec_batch)
    tok_t = lambda t: jnp.pad(jnp.swapaxes(t.astype(F32).reshape(dec_batch, N_HEADS, HEAD_DIM), 1, 2),
                              ((0, 0), (0, 0), (0, LANES - N_HEADS)))
    cache_t = lambda c: jnp.transpose(c[0], (0, 2, 3, 1))
    btbl, bias0 = _sample_bias_tables(rel_bias, cache_win_k.shape[2])
    sample_attn_args = (tok_t(qs), tok_t(ks), tok_t(vs), tok_t(gs), cache_t(cache_win_k), cache_t(cache_win_v),
                        btbl, bias0)

    xp = x_prompt.reshape(batch * seq, D_MODEL)
    nwin = min(WINDOW_MAX, seq)
    q, k, v, g, z, xbc, dt, k_win, v_win, a_t = _inproj(xp, nw, w_pad, qnw, knw, tm=256, window=(seq, nwin),
                                                        sample=sample_attn_args)
    a = _attn_prompt(q, k, v, g, _prompt_bias_table(rel_bias), batch, seq)
    s, h_fin = _ssd_prompt(xbc, z, dt, cw, cb, dtb, alog, dsk, snw, emat, tril, batch, seq)
    ssd_sample_args = (xbcs.reshape(dec_batch, 1, CONV_DIM), zs.reshape(dec_batch, 1, D_SSD),
                       dts.reshape(dec_batch, 1, LANES), state_conv[0], state_ssm[0],
                       cw, cb, dtb, alog, dsk, snw, emat)
    assert (batch * seq) % dec_batch == 0
    y_p, s_s, conv_s, h_s = _outproj(xp, a, s, w_out_b, tm=batch * seq // dec_batch, ssd_sample=ssd_sample_args)
    y_p = y_p.reshape(batch, seq, D_MODEL)
    heads = lambda t: jnp.transpose(t.reshape(1, batch, N_HEADS, HEAD_DIM, nwin), (0, 1, 4, 2, 3))
    kp, vp = heads(k_win), heads(v_win)
    cp = xbc.reshape(batch, seq, CONV_DIM)[None, :, seq - (CONV_W - 1):]
    hp = jnp.swapaxes(h_fin, 1, 2).reshape(batch, N_HEADS, HEAD_DIM, D_STATE)[None]

    a_s = jnp.swapaxes(a_t[:, :, :N_HEADS], 1, 2)
    (y_s,) = _outproj(xs2, a_s.reshape(dec_batch, D_ATTN).astype(BF16), s_s.reshape(dec_batch, D_SSD),
                      w_out_b, tm=dec_batch)
    y_s = y_s.reshape(dec_batch, 1, D_MODEL)
    k_s = ks.reshape(1, dec_batch, 1, N_HEADS, HEAD_DIM)
    v_s = vs.reshape(1, dec_batch, 1, N_HEADS, HEAD_DIM)
    return (y_p, y_s, kp, vp, cp, hp, k_s, v_s, conv_s[None], h_s[None])
```

```python
import functools
import math

import jax
import jax.numpy as jnp
from jax import lax
from jax.experimental import pallas as pl
from jax.experimental.pallas import tpu as pltpu

F32 = jnp.float32
BF16 = jnp.bfloat16

D_MODEL = 1024
D_ATTN = 1024
D_SSD = 1024
HEAD_DIM = 64
N_HEADS = 16
PATTERNS = ((128, 1), (512, 4), (2048, 16))
WINDOW_MAX = 2048
BLK = 128
N_BUCKETS = 32
D_STATE = 128
N_GROUPS = 2
CONV_W = 4
CONV_DIM = D_SSD + 2 * N_GROUPS * D_STATE
CHUNK = 128
EPS = 1e-6
D_IN_PROJ = 4 * D_ATTN + D_SSD + CONV_DIM + N_HEADS
LANES = 128
D_IN_PAD = D_IN_PROJ - N_HEADS + LANES
SPAN = BLK * 16
NEG = -1e30
LOG2E = math.log2(math.e)
SSD_CHUNKS_PER_STEP = 8
SSD_CHUNKS_UNROLL = 8
VMEM_LIMIT = 56 * 1024 * 1024


def _silu(x):
    h = 0.5 * x
    return h + h * jnp.tanh(h)


def _softplus(x):
    return jnp.maximum(x, 0.0) + jnp.log(1.0 + jnp.exp(-jnp.abs(x)))


def _region(fn):
    def body(i, carry):
        fn()
        return carry
    lax.fori_loop(0, 1 + jnp.minimum(pl.program_id(0), 0), body, 0)


def _t5_bucket(dist):
    max_exact = N_BUCKETS // 2
    d_f = jnp.maximum(dist, 1).astype(F32)
    large = max_exact + (jnp.log(d_f / max_exact) / math.log(WINDOW_MAX / max_exact)
                         * (N_BUCKETS - max_exact)).astype(jnp.int32)
    large = jnp.minimum(large, N_BUCKETS - 1)
    return jnp.where(dist < max_exact, dist, large)


def _inproj_kernel(*refs, has_window, sample_heads):
    x_ref, nw_ref, w_ref, qnw_ref, knw_ref = refs[:5]
    n_in = 5 + (8 if sample_heads else 0)
    q_ref, k_ref, v_ref, g_ref, z_ref, xbc_ref, dt_ref = refs[n_in:n_in + 7]
    kt_ref, vt_ref = refs[n_in + 7:n_in + 9] if has_window else (None, None)
    if sample_heads:
        @pl.when(pl.program_id(0) % (N_HEADS // sample_heads) == 0)
        def _():
            refs[-1][...] = jnp.zeros_like(refs[-1])
    x = x_ref[...]
    h = (x * nw_ref[...]).astype(BF16)
    r = lax.rsqrt(jnp.mean(x * x, axis=-1, keepdims=True) + EPS)

    def seg(c0, width):
        return jnp.dot(h, w_ref[:, c0:c0 + width], preferred_element_type=F32) * r

    is_a = lax.broadcasted_iota(jnp.int32, (x.shape[0], LANES), 1) < HEAD_DIM

    def head_rms(pj):
        p2 = pj * pj
        ss_a = jnp.sum(jnp.where(is_a, p2, 0.0), axis=1, keepdims=True)
        ss_b = jnp.sum(jnp.where(is_a, 0.0, p2), axis=1, keepdims=True)
        return jnp.where(is_a, lax.rsqrt(ss_a * (1.0 / HEAD_DIM) + EPS), lax.rsqrt(ss_b * (1.0 / HEAD_DIM) + EPS))

    for out_ref, base, hw_ref in ((q_ref, 0, qnw_ref), (k_ref, D_ATTN, knw_ref)):
        for c in range(2):
            p = seg(base + 512 * c, 512)
            rs = jnp.concatenate([head_rms(p[:, LANES * j:LANES * (j + 1)]) for j in range(512 // LANES)], axis=1)
            normed = p * rs * hw_ref[...]
            out_ref[:, 512 * c:512 * (c + 1)] = normed
            if out_ref is k_ref and kt_ref is not None:
                kt_ref[512 * c:512 * (c + 1), :] = normed.T

    def v_g_z(c):
        v_c = seg(2 * D_ATTN + 512 * c, 512)
        v_ref[:, 512 * c:512 * (c + 1)] = v_c
        if vt_ref is not None:
            vt_ref[512 * c:512 * (c + 1), :] = v_c.T
        g_ref[:, 512 * c:512 * (c + 1)] = _silu(seg(3 * D_ATTN + 512 * c, 512)).astype(g_ref.dtype)
        z_ref[:, 512 * c:512 * (c + 1)] = _silu(seg(4 * D_ATTN + 512 * c, 512)).astype(z_ref.dtype)

    def xbc(c):
        xbc_ref[:, 512 * c:512 * (c + 1)] = seg(5 * D_ATTN + 512 * c, 512)

    def xbc_tail():
        xbc(1)
        xbc(2)
        dt_ref[...] = seg(5 * D_ATTN + CONV_DIM, LANES)

    plain = [functools.partial(v_g_z, 0), functools.partial(v_g_z, 1), functools.partial(xbc, 0), xbc_tail]
    for i, segment in enumerate(plain):
        if sample_heads:
            hh = pl.program_id(0) % (N_HEADS // sample_heads)
            _attn_sample_heads(hh, sample_heads, range(i, sample_heads, len(plain)), *refs[5:n_in], refs[-1])
        segment()


def _inproj(x2d, nw, w_pad, qnw, knw, tm, window=None, sample=None):
    t = x2d.shape[0]
    row = lambda i: (i, 0)
    const = lambda i: (0, 0)
    sample_heads, sample_specs, sample_shapes, sample_out = 0, [], (), []
    if sample is not None:
        dec_batch, n_past = sample[4].shape[0], sample[4].shape[3]
        steps = t // tm
        assert steps % dec_batch == 0 and N_HEADS % (steps // dec_batch) == 0
        n_sub = steps // dec_batch
        sample_heads = N_HEADS // n_sub
        tok = pl.BlockSpec((None, HEAD_DIM, LANES), lambda i: (i // n_sub, 0, 0))
        cache = pl.BlockSpec((None, sample_heads, HEAD_DIM, n_past), lambda i: (i // n_sub, i % n_sub, 0, 0))
        sample_specs = [tok, tok, tok, tok, cache, cache,
                        pl.BlockSpec((len(PATTERNS), N_HEADS, n_past), lambda i: (0, 0, 0)),
                        pl.BlockSpec((1, LANES), const)]
        sample_shapes = (jax.ShapeDtypeStruct((dec_batch, HEAD_DIM, LANES), F32),)
        sample_out = [tok]
    win_shapes, win_specs = (), []
    if window is not None:
        seq, nwin = window
        assert seq % tm == 0 and nwin % tm == 0
        per_seq, first = seq // tm, (seq - nwin) // tm
        win_spec = pl.BlockSpec((None, D_ATTN, tm),
                                lambda i: (i // per_seq, 0, jnp.maximum(i % per_seq - first, 0)))
        win_shapes = (jax.ShapeDtypeStruct((t // seq, D_ATTN, nwin), F32),) * 2
        win_specs = [win_spec, win_spec]
    outs = (
        jax.ShapeDtypeStruct((t, D_ATTN), F32),
        jax.ShapeDtypeStruct((t, D_ATTN), F32),
        jax.ShapeDtypeStruct((t, D_ATTN), F32),
        jax.ShapeDtypeStruct((t, D_ATTN), BF16),
        jax.ShapeDtypeStruct((t, D_SSD), BF16),
        jax.ShapeDtypeStruct((t, CONV_DIM), F32),
        jax.ShapeDtypeStruct((t, LANES), F32),
    )
    return pl.pallas_call(
        functools.partial(_inproj_kernel, has_window=window is not None, sample_heads=sample_heads),
        grid=(t // tm,),
        in_specs=[
            pl.BlockSpec((tm, D_MODEL), row),
            pl.BlockSpec((1, D_MODEL), const),
            pl.BlockSpec((D_MODEL, D_IN_PAD), const, pipeline_mode=pl.Buffered(1)),
            pl.BlockSpec((1, 512), const),
            pl.BlockSpec((1, 512), const),
        ] + sample_specs,
        out_specs=[
            pl.BlockSpec((tm, D_ATTN), row),
            pl.BlockSpec((tm, D_ATTN), row),
            pl.BlockSpec((tm, D_ATTN), row),
            pl.BlockSpec((tm, D_ATTN), row),
            pl.BlockSpec((tm, D_SSD), row),
            pl.BlockSpec((tm, CONV_DIM), row),
            pl.BlockSpec((tm, LANES), row),
        ] + win_specs + sample_out,
        out_shape=outs + win_shapes + sample_shapes,
        compiler_params=pltpu.CompilerParams(
            dimension_semantics=("arbitrary",), vmem_limit_bytes=VMEM_LIMIT),
    )(x2d, nw, w_pad, qnw, knw, *(sample or ()))


N_SSD_SAMPLE_IN = 12


def _outproj_kernel(x_ref, a_ref, s_ref, w_ref, *rest):
    if len(rest) > 1:
        _ssd_sample_kernel(*rest[:N_SSD_SAMPLE_IN], *rest[N_SSD_SAMPLE_IN + 1:])
    y_ref = rest[N_SSD_SAMPLE_IN] if len(rest) > 1 else rest[0]
    y_ref[...] = (x_ref[...]
                  + jnp.dot(a_ref[...], w_ref[0:D_ATTN, :], preferred_element_type=F32)
                  + jnp.dot(s_ref[...], w_ref[D_ATTN:, :], preferred_element_type=F32))


def _outproj(x2d, a, s, w_out_b, tm, ssd_sample=None):
    t = x2d.shape[0]
    row = lambda i: (i, 0)
    extra_in, extra_out, extra_shapes = [], [], ()
    if ssd_sample is not None:
        assert t // tm == ssd_sample[0].shape[0], "one sampled sequence per grid step"
        extra_in, extra_out, extra_shapes = _ssd_sample_specs(ssd_sample[0].shape[0])
    return pl.pallas_call(
        _outproj_kernel,
        grid=(t // tm,),
        in_specs=[
            pl.BlockSpec((tm, D_MODEL), row),
            pl.BlockSpec((tm, D_ATTN), row),
            pl.BlockSpec((tm, D_SSD), row),
            pl.BlockSpec((D_ATTN + D_SSD, D_MODEL), lambda i: (0, 0), pipeline_mode=pl.Buffered(1)),
        ] + extra_in,
        out_specs=[pl.BlockSpec((tm, D_MODEL), row)] + extra_out,
        out_shape=(jax.ShapeDtypeStruct((t, D_MODEL), F32),) + extra_shapes,
        compiler_params=pltpu.CompilerParams(
            dimension_semantics=("arbitrary",), vmem_limit_bytes=VMEM_LIMIT),
    )(x2d, a, s, w_out_b, *(ssd_sample or ()))


def _attn_kernel(q_ref, k_ref, v_ref, g_ref, bias_ref, o_ref, qp, kp, vp, mid, ma_s, mb_s, l_s, acc_s, s_buf):
    s_idx = pl.program_id(2)
    first = (s_idx == 0).astype(jnp.int32)
    slot = s_idx % 2
    pslot = 1 - slot
    lane = lax.broadcasted_iota(jnp.int32, (BLK, LANES), 1)
    is_a = lane < HEAD_DIM
    qscale = HEAD_DIM ** -0.5 * LOG2E
    ones = jnp.ones((2 * BLK, LANES), BF16)
    quarter = SPAN // 4

    @pl.when(s_idx == 0)
    def _():
        kp[pslot] = jnp.zeros((SPAN, LANES), F32)
        vp[pslot] = jnp.zeros((SPAN, LANES), F32)

    def regroup(src_ref, store, scale=None):
        for lo in range(4):
            mid[lo] = src_ref[pl.ds(lo, quarter, stride=4), :]
        for lo in range(4):
            for hi in range(4):
                val = mid[lo, pl.ds(hi, BLK, stride=4), :]
                store(4 * hi + lo, val if scale is None else val * scale)

    def store_q(r, val):
        qp[pl.ds(r * BLK, BLK), :] = val

    def store_k(r, val):
        kp[slot, pl.ds(r * BLK, BLK), :] = val

    def store_v(r, val):
        vp[slot, pl.ds(r * BLK, BLK), :] = val

    def chunks(d, blk):
        n_chunk = 16 // d
        length = BLK // n_chunk
        u, r = blk // d, blk % d
        is_u0 = u == 0
        cur = [pl.multiple_of((d * c + r) * BLK + u * length, 8) for c in range(n_chunk)]
        back = jnp.where(is_u0, BLK - length, (u - 1) * length)
        prev = [pl.multiple_of((d * c + r) * BLK + back, 8) for c in range(n_chunk)]
        return is_u0, length, cur, jnp.where(is_u0, pslot, slot), prev

    def gather(read, starts, length):
        return jnp.concatenate([read(pl.ds(st, length)) for st in starts], axis=0)

    def scatter(write, starts, length, val):
        for c, st in enumerate(starts):
            write(pl.ds(st, length), val[c * length:(c + 1) * length])

    n_blk = SPAN // BLK
    zero = jnp.int32(0)

    def qk_stage(pi, d):
        for j in range(n_blk):
            is_u0, length, cur, prev_slot, prev = chunks(d, zero + j)
            qsc = gather(lambda rows: qp[rows, :], cur, length)
            qq = jnp.concatenate([jnp.where(is_a, qsc, 0.0), jnp.where(is_a, 0.0, qsc)], axis=0).astype(BF16)
            kk = jnp.concatenate([gather(lambda rows: kp[prev_slot, rows, :], prev, length),
                                  gather(lambda rows: kp[slot, rows, :], cur, length)], axis=0).astype(BF16)
            bias = bias_ref[jnp.where(is_u0, first, 0), pi]
            s = lax.dot_general(qq, kk, (((1,), (1,)), ((), ())), preferred_element_type=F32) + bias
            s_buf[pi, j] = s
            m = jnp.broadcast_to(jnp.max(s, axis=1, keepdims=True), (2 * BLK, LANES))
            scatter(lambda rows, v: ma_s.__setitem__((pi, rows, slice(None)), v), cur, length, m[:BLK])
            scatter(lambda rows, v: mb_s.__setitem__((pi, rows, slice(None)), v), cur, length, m[BLK:])

    def pv_stage(pi, d):
        for j in range(n_blk):
            _, length, cur, prev_slot, prev = chunks(d, zero + j)
            m = jnp.concatenate([gather(lambda rows: ma_s[0, rows, :], cur, length),
                                 gather(lambda rows: mb_s[0, rows, :], cur, length)], axis=0)
            p = jnp.exp2(s_buf[pi, j] - jnp.concatenate([m, m], axis=1)).astype(BF16)
            vv = jnp.concatenate([gather(lambda rows: vp[prev_slot, rows, :], prev, length),
                                  gather(lambda rows: vp[slot, rows, :], cur, length)], axis=0).astype(BF16)
            ol = jnp.dot(p, jnp.concatenate([vv, ones], axis=1), preferred_element_type=F32)
            for ref, val in ((acc_s, jnp.where(is_a, ol[:BLK, :LANES], ol[BLK:, :LANES])),
                             (l_s, jnp.where(is_a, ol[:BLK, LANES:], ol[BLK:, LANES:]))):
                scatter(lambda rows, v, ref=ref: ref.__setitem__((pi, rows, slice(None)), v), cur, length, val)

    def shared_max():
        for m_ref in (ma_s, mb_s):
            m_ref[0] = jnp.maximum(jnp.maximum(m_ref[0], m_ref[1]), m_ref[2])

    def all_scores():
        regroup(q_ref, store_q, qscale)
        regroup(k_ref, store_k)
        for pi, (_, d) in reversed(list(enumerate(PATTERNS))):
            qk_stage(pi, d)
        regroup(v_ref, store_v)
        shared_max()

    def all_pv():
        for pi, (_, d) in enumerate(PATTERNS):
            pv_stage(pi, d)
        for lo in range(4):
            for hi in range(4):
                rows = pl.ds((4 * hi + lo) * BLK, BLK)
                acc = acc_s[0, rows, :] + acc_s[1, rows, :] + acc_s[2, rows, :]
                l = l_s[0, rows, :] + l_s[1, rows, :] + l_s[2, rows, :]
                mid[lo, pl.ds(hi, BLK, stride=4), :] = acc / l
        for lo in range(4):
            qp[pl.ds(lo, quarter, stride=4), :] = mid[lo]
        o_ref[...] = (qp[...] * g_ref[...].astype(F32)).astype(o_ref.dtype)

    _region(all_scores)
    _region(all_pv)


def _attn_prompt(q, k, v, g, bias_tbl, batch, seq):
    n_span = seq // SPAN
    n_hp = N_HEADS // 2
    cur = lambda hp, b, s: (b * n_span + s, hp)
    blk = (SPAN, LANES)
    return pl.pallas_call(
        _attn_kernel,
        grid=(n_hp, batch, n_span),
        in_specs=[
            pl.BlockSpec(blk, cur),
            pl.BlockSpec(blk, cur),
            pl.BlockSpec(blk, cur),
            pl.BlockSpec(blk, cur),
            pl.BlockSpec((None, 2, len(PATTERNS), 2 * BLK, 2 * BLK), lambda hp, b, s: (hp, 0, 0, 0, 0)),
        ],
        out_specs=pl.BlockSpec(blk, cur),
        out_shape=jax.ShapeDtypeStruct((batch * seq, D_ATTN), BF16),
        scratch_shapes=[
            pltpu.VMEM(blk, F32),
            pltpu.VMEM((2,) + blk, F32),
            pltpu.VMEM((2,) + blk, F32),
            pltpu.VMEM((4, SPAN // 4, LANES), F32),
            pltpu.VMEM((len(PATTERNS),) + blk, F32),
            pltpu.VMEM((len(PATTERNS),) + blk, F32),
            pltpu.VMEM((len(PATTERNS),) + blk, F32),
            pltpu.VMEM((len(PATTERNS),) + blk, F32),
            pltpu.VMEM((len(PATTERNS), SPAN // BLK, 2 * BLK, 2 * BLK), F32),
        ],
        compiler_params=pltpu.CompilerParams(
            dimension_semantics=("arbitrary", "arbitrary", "arbitrary"), vmem_limit_bytes=VMEM_LIMIT),
    )(q, k, v, g, bias_tbl)


def _bias_lookup(rel_bias, dist):
    bucket = _t5_bucket(dist)[..., None]
    edges = jnp.arange(N_BUCKETS)
    onehot = ((bucket >= edges) & (bucket < edges + 1)).astype(F32)
    return jnp.einsum('...b,bh->h...', onehot, rel_bias.astype(F32), precision=lax.Precision.HIGHEST)


def _prompt_bias_table(rel_bias):
    tbls = []
    for w, d in PATTERNS:
        n_chunk = 16 // d
        n = jnp.arange(BLK)
        idx = (n % (BLK // n_chunk)) * n_chunk + n // (BLK // n_chunk)
        i = idx[:, None]
        j = jnp.concatenate([idx, BLK + idx])[None, :]
        rel = i + BLK - j
        band = (rel >= 0) & (rel <= w // d)
        bias = _bias_lookup(rel_bias, jnp.maximum(rel, 0) * d) * LOG2E
        normal = jnp.where(band[None], bias, NEG)
        first = jnp.where((band & (j >= BLK))[None], bias, NEG)
        tbls.append(jnp.stack([normal, first], axis=0))
    t = jnp.stack(tbls, axis=1)
    t = t.reshape(2, len(PATTERNS), N_HEADS // 2, 2 * BLK, 2 * BLK)
    return jnp.moveaxis(t, 2, 0)


def _ssd_kernel(xbc_ref, z_ref, dt_ref, cw_ref, cb_ref, dtb_ref, alog_ref, dsk_ref, nw_ref,
                e_ref, tril_ref, s_ref, h_ref, cbuf):
    @pl.when(pl.program_id(1) == 0)
    def _():
        cbuf[:, 0:8, :] = jnp.zeros((CONV_DIM // LANES, 8, LANES), F32)
        h_ref[...] = jnp.zeros_like(h_ref)

    def body(ci, carry):
        for k in range(SSD_CHUNKS_UNROLL):
            rows = pl.ds(pl.multiple_of((ci * SSD_CHUNKS_UNROLL + k) * CHUNK, CHUNK), CHUNK)
            _ssd_chunk(xbc_ref.at[rows], z_ref.at[rows], dt_ref.at[rows], cw_ref, cb_ref, dtb_ref, alog_ref,
                       dsk_ref, nw_ref, e_ref, tril_ref, s_ref.at[rows], h_ref, cbuf)
        return carry
    lax.fori_loop(0, SSD_CHUNKS_PER_STEP // SSD_CHUNKS_UNROLL, body, 0)


def _ssd_chunk(xbc_ref, z_ref, dt_ref, cw_ref, cb_ref, dtb_ref, alog_ref, dsk_ref, nw_ref,
               e_ref, tril_ref, s_ref, h_ref, cbuf):
    n_slab = CONV_DIM // LANES

    xc_slabs = []
    for j in range(n_slab):
        cols = slice(j * LANES, (j + 1) * LANES)
        xj = xbc_ref[:, cols]
        cbuf[j, 8:8 + CHUNK, :] = xj
        acc = cb_ref[:, cols] + xj * cw_ref[CONV_W - 1:CONV_W, cols]
        for i in range(CONV_W - 1):
            tap = cbuf[pl.ds(j, 1, stride=2), pl.ds(8 - (CONV_W - 1) + i, CHUNK), :][0]
            acc = acc + tap * cw_ref[i:i + 1, cols]
        xc_slabs.append(_silu(acc))
        cbuf[j, 0:8, :] = xj[CHUNK - 8:, :]

    n_x = D_SSD // LANES
    xs = jnp.concatenate(xc_slabs[:n_x], axis=1)
    lane = lax.broadcasted_iota(jnp.int32, (CHUNK, LANES), 1)
    sub = lax.broadcasted_iota(jnp.int32, (CHUNK, LANES), 0)
    head_lane = lane < N_HEADS
    dt = jnp.where(head_lane, _softplus(dt_ref[...] + dtb_ref[...]), 0.0)
    la = dt * (-jnp.exp(alog_ref[...]))

    def hi_lo(val):
        hi = val.astype(BF16).astype(F32)
        return (hi + pltpu.roll(val - hi, N_HEADS, axis=1)).astype(BF16)

    cs2 = jnp.dot(tril_ref[...], hi_lo(la), preferred_element_type=F32)
    a_cs = jnp.where(head_lane, cs2 + pltpu.roll(cs2, LANES - N_HEADS, axis=1), 0.0)
    ea = jnp.where(head_lane, jnp.exp(a_cs), 0.0)
    dte = jnp.where(head_lane, jnp.exp(a_cs[CHUNK - 1:CHUNK, :] - a_cs), 0.0)

    expanded = jnp.dot(jnp.concatenate([hi_lo(dt), hi_lo(ea), hi_lo(dte)], axis=0), e_ref[...],
                       preferred_element_type=F32)
    dtx, eax, dtex = expanded[:CHUNK], expanded[CHUNK:2 * CHUNK], expanded[2 * CHUNK:]
    xdt_f = xs * dtx
    xdt = xdt_f.astype(BF16)
    xdte = (xdt_f * dtex).astype(BF16)

    a_cs_t = a_cs.T
    causal = sub >= lane
    is_a = lane < HEAD_DIM
    ys = []
    for g in range(N_GROUPS):
        b_g = xc_slabs[n_x + g]
        c_g = xc_slabs[n_x + N_GROUPS + g]
        b_bf = b_g.astype(BF16)
        c_bf = c_g.astype(BF16)
        cb = lax.dot_general(c_bf, b_bf, (((1,), (1,)), ((), ())), preferred_element_type=F32)
        gcols = slice(g * 512, (g + 1) * 512)
        h_prev = h_ref[:, gcols]
        y_off = jnp.dot(c_bf, h_prev.astype(BF16), preferred_element_type=F32) * eax[:, gcols]
        st = jnp.dot(b_g.T.astype(BF16), xdte[:, gcols], preferred_element_type=F32)
        h_ref[:, gcols] = h_prev * eax[CHUNK - 1:CHUNK, gcols] + st
        for hp in range(4):
            pair = []
            for which in range(2):
                h = g * 8 + hp * 2 + which
                col = jnp.sum(jnp.where(lane == h, a_cs, 0.0), axis=1, keepdims=True)
                seg = col - a_cs_t[h:h + 1, :]
                lmat = jnp.exp(jnp.where(causal, seg, NEG))
                pair.append((cb * lmat).astype(BF16))
            x_pair = xdt[:, g * 512 + hp * LANES:g * 512 + (hp + 1) * LANES]
            y_a = jnp.dot(pair[0], x_pair, preferred_element_type=F32)
            y_b = jnp.dot(pair[1], x_pair, preferred_element_type=F32)
            ys.append(jnp.where(is_a, y_a, y_b) + y_off[:, hp * LANES:(hp + 1) * LANES])
    y = jnp.concatenate(ys, axis=1) + dsk_ref[...] * xs
    yz = y * z_ref[...].astype(F32)
    var = jnp.mean(yz * yz, axis=-1, keepdims=True)
    s_ref[...] = (yz * lax.rsqrt(var + EPS) * nw_ref[...]).astype(s_ref.dtype)


def _ssd_prompt(xbc, z, dt, cw, cb, dtb, alog, dsk, nw, emat, tril, batch, seq):
    tc = CHUNK * SSD_CHUNKS_PER_STEP
    assert seq % tc == 0
    nc = seq // tc
    row = lambda b, c: (b * nc + c, 0)
    const = lambda b, c: (0, 0)
    return pl.pallas_call(
        _ssd_kernel,
        grid=(batch, nc),
        in_specs=[
            pl.BlockSpec((tc, CONV_DIM), row),
            pl.BlockSpec((tc, D_SSD), row),
            pl.BlockSpec((tc, LANES), row),
            pl.BlockSpec((CONV_W, CONV_DIM), const),
            pl.BlockSpec((1, CONV_DIM), const),
            pl.BlockSpec((1, LANES), const),
            pl.BlockSpec((1, LANES), const),
            pl.BlockSpec((1, D_SSD), const),
            pl.BlockSpec((1, D_SSD), const),
            pl.BlockSpec((LANES, D_SSD), const),
            pl.BlockSpec((CHUNK, CHUNK), const),
        ],
        out_specs=[
            pl.BlockSpec((tc, D_SSD), row),
            pl.BlockSpec((None, D_STATE, D_SSD), lambda b, c: (b, 0, 0)),
        ],
        out_shape=(jax.ShapeDtypeStruct((batch * seq, D_SSD), BF16),
                   jax.ShapeDtypeStruct((batch, D_STATE, D_SSD), F32)),
        scratch_shapes=[pltpu.VMEM((CONV_DIM // LANES, CHUNK + 8, LANES), F32)],
        compiler_params=pltpu.CompilerParams(
            dimension_semantics=("arbitrary", "arbitrary"), vmem_limit_bytes=VMEM_LIMIT),
    )(xbc, z, dt, cw, cb, dtb, alog, dsk, nw, emat, tril)


def _attn_sample_heads(hh, n_heads, heads, qt_ref, knt_ref, vnt_ref, gt_ref, k_ref, v_ref, btbl_ref, bias0_ref,
                       o_ref):
    lane = lax.broadcasted_iota(jnp.int32, (HEAD_DIM, LANES), 1)
    lane1 = lax.broadcasted_iota(jnp.int32, (1, LANES), 1)
    qt = qt_ref[...] * (HEAD_DIM ** -0.5)
    n_pat = float(len(PATTERNS))
    for j in heads:
        h = hh * n_heads + j
        pick = lane == h

        def col(val, pick=pick):
            return jnp.sum(jnp.where(pick, val, 0.0), axis=1, keepdims=True)

        qc, knc, vnc, gc = col(qt), col(knt_ref[...]), col(vnt_ref[...]), col(gt_ref[...])
        b0 = jnp.sum(jnp.where(lane1 == h, bias0_ref[...], 0.0), axis=1, keepdims=True)
        s0 = jnp.sum(qc * knc, axis=0, keepdims=True) + b0
        s = jnp.sum(k_ref[j] * qc, axis=0, keepdims=True)
        sp = [s + btbl_ref[pi, pl.ds(h, 1), :] for pi in range(len(PATTERNS))]
        m = s0
        for x in sp:
            m = jnp.maximum(m, jnp.max(x, axis=1, keepdims=True))
        p0 = n_pat * jnp.exp(s0 - m)
        pw = jnp.exp(sp[0] - m)
        for x in sp[1:]:
            pw = pw + jnp.exp(x - m)
        l = jnp.sum(pw, axis=1, keepdims=True) + p0
        oc = (jnp.sum(v_ref[j] * pw, axis=1, keepdims=True) + p0 * vnc) / l
        o_ref[...] = jnp.where(pick, oc * gc, o_ref[...])


def _sample_bias_tables(rel_bias, n_past):
    dist = n_past - jnp.arange(n_past)
    bias = _bias_lookup(rel_bias, dist)
    tbls = [jnp.where(((dist % d == 0) & (dist <= w))[None], bias, NEG) for w, d in PATTERNS]
    bias0 = _bias_lookup(rel_bias, jnp.zeros((1,), jnp.int32))
    return jnp.stack(tbls, axis=0), jnp.pad(bias0.reshape(1, N_HEADS), ((0, 0), (0, LANES - N_HEADS)))


def _ssd_sample_kernel(xbc_ref, z_ref, dt_ref, sc_ref, h_ref, cw_ref, cb_ref, dtb_ref, alog_ref, dsk_ref,
                       nw_ref, e_ref, s_ref, conv_out_ref, h_out_ref):
    xnew = xbc_ref[...]
    sc = sc_ref[...]
    acc = cb_ref[...] + xnew * cw_ref[CONV_W - 1:CONV_W, :]
    for i in range(CONV_W - 1):
        acc = acc + sc[i:i + 1, :] * cw_ref[i:i + 1, :]
    xc = _silu(acc)
    conv_out_ref[0:CONV_W - 2, :] = sc[1:CONV_W - 1, :]
    conv_out_ref[CONV_W - 2:CONV_W - 1, :] = xnew

    xs = xc[:, :D_SSD]
    lane1 = lax.broadcasted_iota(jnp.int32, (1, LANES), 1)
    dt = jnp.where(lane1 < N_HEADS, _softplus(dt_ref[...] + dtb_ref[...]), 0.0)
    da = jnp.where(lane1 < N_HEADS, jnp.exp(dt * (-jnp.exp(alog_ref[...]))), 0.0)

    def expand(val):
        v8 = jnp.broadcast_to(val, (8, LANES))
        out = jnp.zeros((8, D_SSD), F32)
        for _ in range(3):
            part = v8.astype(BF16)
            out = out + jnp.dot(part, e_ref[...], preferred_element_type=F32)
            v8 = v8 - part.astype(F32)
        return out[0:1, :]

    xdt = xs * expand(dt)
    dax = expand(da)

    lane = lax.broadcasted_iota(jnp.int32, (HEAD_DIM, LANES), 1)
    sub = lax.broadcasted_iota(jnp.int32, (HEAD_DIM, LANES), 0)
    eye2 = (lane % HEAD_DIM) == sub
    is_a = lane < HEAD_DIM

    def to_cols(row):
        mat = jnp.where(eye2, jnp.broadcast_to(row, (HEAD_DIM, LANES)), 0.0)
        col_a = jnp.sum(jnp.where(is_a, mat, 0.0), axis=1, keepdims=True)
        col_b = jnp.sum(jnp.where(is_a, 0.0, mat), axis=1, keepdims=True)
        return col_a, col_b

    y_rows = []
    for hp in range(N_HEADS // 2):
        g = hp // 4
        b_row = xc[:, D_SSD + g * D_STATE:D_SSD + (g + 1) * D_STATE]
        c_row = xc[:, D_SSD + (N_GROUPS + g) * D_STATE:D_SSD + (N_GROUPS + g + 1) * D_STATE]
        cols = slice(hp * LANES, (hp + 1) * LANES)
        x_cols = to_cols(xdt[:, cols])
        d_cols = to_cols(dax[:, cols])
        y_cols = []
        for which in range(2):
            h = hp * 2 + which
            h_new = h_ref[h] * d_cols[which] + x_cols[which] * b_row
            h_out_ref[h] = h_new
            y_cols.append(jnp.sum(h_new * c_row, axis=1, keepdims=True))
        y_mat = jnp.where(eye2, jnp.where(is_a, y_cols[0], y_cols[1]), 0.0)
        y_rows.append(jnp.sum(y_mat, axis=0, keepdims=True))
    y = jnp.concatenate(y_rows, axis=1) + dsk_ref[...] * xs
    yz = y * z_ref[...].astype(F32)
    var = jnp.mean(yz * yz, axis=-1, keepdims=True)
    s_ref[...] = (yz * lax.rsqrt(var + EPS) * nw_ref[...]).astype(s_ref.dtype)


def _ssd_sample_specs(b):
    const = lambda i: (0, 0)
    tok = lambda width: pl.BlockSpec((None, 1, width), lambda i: (i, 0, 0))
    conv_spec = pl.BlockSpec((None, CONV_W - 1, CONV_DIM), lambda i: (i, 0, 0))
    ssm_spec = pl.BlockSpec((None, N_HEADS, HEAD_DIM, D_STATE), lambda i: (i, 0, 0, 0))
    in_specs = [
        tok(CONV_DIM), tok(D_SSD), tok(LANES), conv_spec, ssm_spec,
        pl.BlockSpec((CONV_W, CONV_DIM), const),
        pl.BlockSpec((1, CONV_DIM), const),
        pl.BlockSpec((1, LANES), const),
        pl.BlockSpec((1, LANES), const),
        pl.BlockSpec((1, D_SSD), const),
        pl.BlockSpec((1, D_SSD), const),
        pl.BlockSpec((LANES, D_SSD), const),
    ]
    assert len(in_specs) == N_SSD_SAMPLE_IN
    out_shapes = (jax.ShapeDtypeStruct((b, 1, D_SSD), BF16),
                  jax.ShapeDtypeStruct((b, CONV_W - 1, CONV_DIM), F32),
                  jax.ShapeDtypeStruct((b, N_HEADS, HEAD_DIM, D_STATE), F32))
    return in_specs, [tok(D_SSD), conv_spec, ssm_spec], out_shapes


def kernel(x_prompt, x_sample, cache_win_k, cache_win_v, state_conv, state_ssm, norm_w, w_in, q_norm_w,
           k_norm_w, rel_bias, conv_w, conv_b, dt_bias, a_log, d_skip, ssd_norm_w, w_out):
    assert x_prompt.shape[-1] == D_MODEL and w_in.shape[0] == 1, "single-layer model of width 1024 only"
    batch, seq, _ = x_prompt.shape
    dec_batch, dec_seq, _ = x_sample.shape
    assert dec_seq == 1 and seq % SPAN == 0 and cache_win_k.shape[2] == WINDOW_MAX

    w_pad = jnp.pad(w_in[0].astype(BF16), ((0, 0), (0, D_IN_PAD - D_IN_PROJ)))
    w_out_b = w_out[0].astype(BF16)
    nw = norm_w[0].reshape(1, D_MODEL)
    qnw = jnp.tile(q_norm_w[0], 512 // HEAD_DIM).reshape(1, 512)
    knw = jnp.tile(k_norm_w[0], 512 // HEAD_DIM).reshape(1, 512)
    cw, cb = conv_w[0], conv_b[0].reshape(1, CONV_DIM)
    pad_heads = lambda a: jnp.pad(a.reshape(1, N_HEADS), ((0, 0), (0, LANES - N_HEADS)))
    dtb, alog = pad_heads(dt_bias[0]), pad_heads(a_log[0])
    dsk = jnp.repeat(d_skip[0], HEAD_DIM).reshape(1, D_SSD)
    snw = ssd_norm_w[0].reshape(1, D_SSD)
    erow = jnp.arange(LANES)[:, None]
    emat = ((erow % N_HEADS == (jnp.arange(D_SSD) // HEAD_DIM)[None, :]) & (erow < 2 * N_HEADS)).astype(BF16)
    tril = (jnp.arange(CHUNK)[:, None] >= jnp.arange(CHUNK)[None, :]).astype(BF16)

    xs2 = x_sample.reshape(dec_batch, D_MODEL)
    qs, ks, vs, gs, zs, xbcs, dts = _inproj(xs2, nw, w_pad, qnw, knw, tm=dec_batch)
    tok_t = lambda t: jnp.pad(jnp.swapaxes(t.astype(F32).reshape(dec_batch, N_HEADS, HEAD_DIM), 1, 2),
                              ((0, 0), (0, 0), (0, LANES - N_HEADS)))
    cache_t = lambda c: jnp.transpose(c[0], (0, 2, 3, 1))
    btbl, bias0 = _sample_bias_tables(rel_bias, cache_win_k.shape[2])
    sample_attn_args = (tok_t(qs), tok_t(ks), tok_t(vs), tok_t(gs), cache_t(cache_win_k), cache_t(cache_win_v),
                        btbl, bias0)

    xp = x_prompt.reshape(batch * seq, D_MODEL)
    nwin = min(WINDOW_MAX, seq)
    q, k, v, g, z, xbc, dt, k_win, v_win, a_t = _inproj(xp, nw, w_pad, qnw, knw, tm=256, window=(seq, nwin),
                                                        sample=sample_attn_args)
    a = _attn_prompt(q, k, v, g, _prompt_bias_table(rel_bias), batch, seq)
    s, h_fin = _ssd_prompt(xbc, z, dt, cw, cb, dtb, alog, dsk, snw, emat, tril, batch, seq)
    ssd_sample_args = (xbcs.reshape(dec_batch, 1, CONV_DIM), zs.reshape(dec_batch, 1, D_SSD),
                       dts.reshape(dec_batch, 1, LANES), state_conv[0], state_ssm[0],
                       cw, cb, dtb, alog, dsk, snw, emat)
    assert (batch * seq) % dec_batch == 0
    y_p, s_s, conv_s, h_s = _outproj(xp, a, s, w_out_b, tm=batch * seq // dec_batch, ssd_sample=ssd_sample_args)
    y_p = y_p.reshape(batch, seq, D_MODEL)
    heads = lambda t: jnp.transpose(t.reshape(1, batch, N_HEADS, HEAD_DIM, nwin), (0, 1, 4, 2, 3))
    kp, vp = heads(k_win), heads(v_win)
    cp = xbc.reshape(batch, seq, CONV_DIM)[None, :, seq - (CONV_W - 1):]
    hp = jnp.swapaxes(h_fin, 1, 2).reshape(batch, N_HEADS, HEAD_DIM, D_STATE)[None]

    a_s = jnp.swapaxes(a_t[:, :, :N_HEADS], 1, 2)
    (y_s,) = _outproj(xs2, a_s.reshape(dec_batch, D_ATTN).astype(BF16), s_s.reshape(dec_batch, D_SSD),
                      w_out_b, tm=dec_batch)
    y_s = y_s.reshape(dec_batch, 1, D_MODEL)
    k_s = ks.reshape(1, dec_batch, 1, N_HEADS, HEAD_DIM)
    v_s = vs.reshape(1, dec_batch, 1, N_HEADS, HEAD_DIM)
    return (y_p, y_s, kp, vp, cp, hp, k_s, v_s, conv_s[None], h_s[None])
```

```python
import functools
import math

import jax
import jax.numpy as jnp
from jax import lax
from jax.experimental import pallas as pl
from jax.experimental.pallas import tpu as pltpu

F32 = jnp.float32
BF16 = jnp.bfloat16

D_MODEL = 1024
D_ATTN = 1024
D_SSD = 1024
HEAD_DIM = 64
N_HEADS = 16
PATTERNS = ((128, 1), (512, 4), (2048, 16))
WINDOW_MAX = 2048
BLK = 128
N_BUCKETS = 32
D_STATE = 128
N_GROUPS = 2
CONV_W = 4
CONV_DIM = D_SSD + 2 * N_GROUPS * D_STATE
CHUNK = 128
EPS = 1e-6
D_IN_PROJ = 4 * D_ATTN + D_SSD + CONV_DIM + N_HEADS
LANES = 128
D_IN_PAD = D_IN_PROJ - N_HEADS + LANES
SPAN = BLK * 16
NEG = -1e30
LOG2E = math.log2(math.e)
SSD_CHUNKS_PER_STEP = 8
SSD_CHUNKS_UNROLL = 8
VMEM_LIMIT = 56 * 1024 * 1024


def _silu(x):
    h = 0.5 * x
    return h + h * jnp.tanh(h)


def _softplus(x):
    return jnp.maximum(x, 0.0) + jnp.log(1.0 + jnp.exp(-jnp.abs(x)))


def _region(fn):
    def body(i, carry):
        fn()
        return carry
    lax.fori_loop(0, 1 + jnp.minimum(pl.program_id(0), 0), body, 0)


def _t5_bucket(dist):
    max_exact = N_BUCKETS // 2
    d_f = jnp.maximum(dist, 1).astype(F32)
    large = max_exact + (jnp.log(d_f / max_exact) / math.log(WINDOW_MAX / max_exact)
                         * (N_BUCKETS - max_exact)).astype(jnp.int32)
    large = jnp.minimum(large, N_BUCKETS - 1)
    return jnp.where(dist < max_exact, dist, large)


def _inproj_kernel(*refs, has_window, sample_heads):
    x_ref, nw_ref, w_ref, qnw_ref, knw_ref = refs[:5]
    n_in = 5 + (8 if sample_heads else 0)
    q_ref, k_ref, v_ref, g_ref, z_ref, xbc_ref, dt_ref = refs[n_in:n_in + 7]
    kt_ref, vt_ref = refs[n_in + 7:n_in + 9] if has_window else (None, None)
    if sample_heads:
        @pl.when(pl.program_id(0) % (N_HEADS // sample_heads) == 0)
        def _():
            refs[-1][...] = jnp.zeros_like(refs[-1])
    x = x_ref[...]
    h = (x * nw_ref[...]).astype(BF16)
    r = lax.rsqrt(jnp.mean(x * x, axis=-1, keepdims=True) + EPS)

    def seg(c0, width):
        return jnp.dot(h, w_ref[:, c0:c0 + width], preferred_element_type=F32) * r

    is_a = lax.broadcasted_iota(jnp.int32, (x.shape[0], LANES), 1) < HEAD_DIM

    def head_rms(pj):
        p2 = pj * pj
        ss_a = jnp.sum(jnp.where(is_a, p2, 0.0), axis=1, keepdims=True)
        ss_b = jnp.sum(jnp.where(is_a, 0.0, p2), axis=1, keepdims=True)
        return jnp.where(is_a, lax.rsqrt(ss_a * (1.0 / HEAD_DIM) + EPS), lax.rsqrt(ss_b * (1.0 / HEAD_DIM) + EPS))

    for out_ref, base, hw_ref in ((q_ref, 0, qnw_ref), (k_ref, D_ATTN, knw_ref)):
        for c in range(2):
            p = seg(base + 512 * c, 512)
            rs = jnp.concatenate([head_rms(p[:, LANES * j:LANES * (j + 1)]) for j in range(512 // LANES)], axis=1)
            normed = p * rs * hw_ref[...]
            out_ref[:, 512 * c:512 * (c + 1)] = normed
            if out_ref is k_ref and kt_ref is not None:
                kt_ref[512 * c:512 * (c + 1), :] = normed.T

    def v_g_z(c):
        v_c = seg(2 * D_ATTN + 512 * c, 512)
        v_ref[:, 512 * c:512 * (c + 1)] = v_c
        if vt_ref is not None:
            vt_ref[512 * c:512 * (c + 1), :] = v_c.T
        g_ref[:, 512 * c:512 * (c + 1)] = _silu(seg(3 * D_ATTN + 512 * c, 512)).astype(g_ref.dtype)
        z_ref[:, 512 * c:512 * (c + 1)] = _silu(seg(4 * D_ATTN + 512 * c, 512)).astype(z_ref.dtype)

    def xbc(c):
        xbc_ref[:, 512 * c:512 * (c + 1)] = seg(5 * D_ATTN + 512 * c, 512)

    def xbc_tail():
        xbc(1)
        xbc(2)
        dt_ref[...] = seg(5 * D_ATTN + CONV_DIM, LANES)

    plain = [functools.partial(v_g_z, 0), functools.partial(v_g_z, 1), functools.partial(xbc, 0), xbc_tail]
    for i, segment in enumerate(plain):
        if sample_heads:
            hh = pl.program_id(0) % (N_HEADS // sample_heads)
            _attn_sample_heads(hh, sample_heads, range(i, sample_heads, len(plain)), *refs[5:n_in], refs[-1])
        segment()


def _inproj(x2d, nw, w_pad, qnw, knw, tm, window=None, sample=None):
    t = x2d.shape[0]
    row = lambda i: (i, 0)
    const = lambda i: (0, 0)
    sample_heads, sample_specs, sample_shapes, sample_out = 0, [], (), []
    if sample is not None:
        dec_batch, n_past = sample[4].shape[0], sample[4].shape[3]
        steps = t // tm
        assert steps % dec_batch == 0 and N_HEADS % (steps // dec_batch) == 0
        n_sub = steps // dec_batch
        sample_heads = N_HEADS // n_sub
        tok = pl.BlockSpec((None, HEAD_DIM, LANES), lambda i: (i // n_sub, 0, 0))
        cache = pl.BlockSpec((None, sample_heads, HEAD_DIM, n_past), lambda i: (i // n_sub, i % n_sub, 0, 0))
        sample_specs = [tok, tok, tok, tok, cache, cache,
                        pl.BlockSpec((len(PATTERNS), N_HEADS, n_past), lambda i: (0, 0, 0)),
                        pl.BlockSpec((1, LANES), const)]
        sample_shapes = (jax.ShapeDtypeStruct((dec_batch, HEAD_DIM, LANES), F32),)
        sample_out = [tok]
    win_shapes, win_specs = (), []
    if window is not None:
        seq, nwin = window
        assert seq % tm == 0 and nwin % tm == 0
        per_seq, first = seq // tm, (seq - nwin) // tm
        win_spec = pl.BlockSpec((None, D_ATTN, tm),
                                lambda i: (i // per_seq, 0, jnp.maximum(i % per_seq - first, 0)))
        win_shapes = (jax.ShapeDtypeStruct((t // seq, D_ATTN, nwin), F32),) * 2
        win_specs = [win_spec, win_spec]
    outs = (
        jax.ShapeDtypeStruct((t, D_ATTN), F32),
        jax.ShapeDtypeStruct((t, D_ATTN), F32),
        jax.ShapeDtypeStruct((t, D_ATTN), F32),
        jax.ShapeDtypeStruct((t, D_ATTN), BF16),
        jax.ShapeDtypeStruct((t, D_SSD), BF16),
        jax.ShapeDtypeStruct((t, CONV_DIM), F32),
        jax.ShapeDtypeStruct((t, LANES), F32),
    )
    return pl.pallas_call(
        functools.partial(_inproj_kernel, has_window=window is not None, sample_heads=sample_heads),
        grid=(t // tm,),
        in_specs=[
            pl.BlockSpec((tm, D_MODEL), row),
            pl.BlockSpec((1, D_MODEL), const),
            pl.BlockSpec((D_MODEL, D_IN_PAD), const, pipeline_mode=pl.Buffered(1)),
            pl.BlockSpec((1, 512), const),
            pl.BlockSpec((1, 512), const),
        ] + sample_specs,
        out_specs=[
            pl.BlockSpec((tm, D_ATTN), row),
            pl.BlockSpec((tm, D_ATTN), row),
            pl.BlockSpec((tm, D_ATTN), row),
            pl.BlockSpec((tm, D_ATTN), row),
            pl.BlockSpec((tm, D_SSD), row),
            pl.BlockSpec((tm, CONV_DIM), row),
            pl.BlockSpec((tm, LANES), row),
        ] + win_specs + sample_out,
        out_shape=outs + win_shapes + sample_shapes,
        compiler_params=pltpu.CompilerParams(
            dimension_semantics=("arbitrary",), vmem_limit_bytes=VMEM_LIMIT),
    )(x2d, nw, w_pad, qnw, knw, *(sample or ()))


N_SSD_SAMPLE_IN = 12


def _outproj_kernel(x_ref, a_ref, s_ref, w_ref, *rest):
    if len(rest) > 1:
        _ssd_sample_kernel(*rest[:N_SSD_SAMPLE_IN], *rest[N_SSD_SAMPLE_IN + 1:])
    y_ref = rest[N_SSD_SAMPLE_IN] if len(rest) > 1 else rest[0]
    y_ref[...] = (x_ref[...]
                  + jnp.dot(a_ref[...], w_ref[0:D_ATTN, :], preferred_element_type=F32)
                  + jnp.dot(s_ref[...], w_ref[D_ATTN:, :], preferred_element_type=F32))


def _outproj(x2d, a, s, w_out_b, tm, ssd_sample=None):
    t = x2d.shape[0]
    row = lambda i: (i, 0)
    extra_in, extra_out, extra_shapes = [], [], ()
    if ssd_sample is not None:
        assert t // tm == ssd_sample[0].shape[0], "one sampled sequence per grid step"
        extra_in, extra_out, extra_shapes = _ssd_sample_specs(ssd_sample[0].shape[0])
    return pl.pallas_call(
        _outproj_kernel,
        grid=(t // tm,),
        in_specs=[
            pl.BlockSpec((tm, D_MODEL), row),
            pl.BlockSpec((tm, D_ATTN), row),
            pl.BlockSpec((tm, D_SSD), row),
            pl.BlockSpec((D_ATTN + D_SSD, D_MODEL), lambda i: (0, 0), pipeline_mode=pl.Buffered(1)),
        ] + extra_in,
        out_specs=[pl.BlockSpec((tm, D_MODEL), row)] + extra_out,
        out_shape=(jax.ShapeDtypeStruct((t, D_MODEL), F32),) + extra_shapes,
        compiler_params=pltpu.CompilerParams(
            dimension_semantics=("arbitrary",), vmem_limit_bytes=VMEM_LIMIT),
    )(x2d, a, s, w_out_b, *(ssd_sample or ()))


def _attn_kernel(q_ref, k_ref, v_ref, g_ref, bias_ref, o_ref, qp, kp, vp, mid, ma_s, mb_s, l_s, acc_s, s_buf):
    s_idx = pl.program_id(2)
    first = (s_idx == 0).astype(jnp.int32)
    slot = s_idx % 2
    pslot = 1 - slot
    lane = lax.broadcasted_iota(jnp.int32, (BLK, LANES), 1)
    is_a = lane < HEAD_DIM
    qscale = HEAD_DIM ** -0.5 * LOG2E
    ones = jnp.ones((2 * BLK, LANES), BF16)
    quarter = SPAN // 4

    @pl.when(s_idx == 0)
    def _():
        kp[pslot] = jnp.zeros((SPAN, LANES), F32)
        vp[pslot] = jnp.zeros((SPAN, LANES), F32)

    def regroup(src_ref, store, scale=None):
        for lo in range(4):
            mid[lo] = src_ref[pl.ds(lo, quarter, stride=4), :]
        for lo in range(4):
            for hi in range(4):
                val = mid[lo, pl.ds(hi, BLK, stride=4), :]
                store(4 * hi + lo, val if scale is None else val * scale)

    def store_q(r, val):
        qp[pl.ds(r * BLK, BLK), :] = val

    def store_k(r, val):
        kp[slot, pl.ds(r * BLK, BLK), :] = val

    def store_v(r, val):
        vp[slot, pl.ds(r * BLK, BLK), :] = val

    def chunks(d, blk):
        n_chunk = 16 // d
        length = BLK // n_chunk
        u, r = blk // d, blk % d
        is_u0 = u == 0
        cur = [pl.multiple_of((d * c + r) * BLK + u * length, 8) for c in range(n_chunk)]
        back = jnp.where(is_u0, BLK - length, (u - 1) * length)
        prev = [pl.multiple_of((d * c + r) * BLK + back, 8) for c in range(n_chunk)]
        return is_u0, length, cur, jnp.where(is_u0, pslot, slot), prev

    def gather(read, starts, length):
        return jnp.concatenate([read(pl.ds(st, length)) for st in starts], axis=0)

    def scatter(write, starts, length, val):
        for c, st in enumerate(starts):
            write(pl.ds(st, length), val[c * length:(c + 1) * length])

    n_blk = SPAN // BLK
    zero = jnp.int32(0)

    def qk_stage(pi, d):
        for j in range(n_blk):
            is_u0, length, cur, prev_slot, prev = chunks(d, zero + j)
            qsc = gather(lambda rows: qp[rows, :], cur, length)
            qq = jnp.concatenate([jnp.where(is_a, qsc, 0.0), jnp.where(is_a, 0.0, qsc)], axis=0).astype(BF16)
            kk = jnp.concatenate([gather(lambda rows: kp[prev_slot, rows, :], prev, length),
                                  gather(lambda rows: kp[slot, rows, :], cur, length)], axis=0).astype(BF16)
            bias = bias_ref[jnp.where(is_u0, first, 0), pi]
            s = lax.dot_general(qq, kk, (((1,), (1,)), ((), ())), preferred_element_type=F32) + bias
            s_buf[pi, j] = s
            m = jnp.broadcast_to(jnp.max(s, axis=1, keepdims=True), (2 * BLK, LANES))
            scatter(lambda rows, v: ma_s.__setitem__((pi, rows, slice(None)), v), cur, length, m[:BLK])
            scatter(lambda rows, v: mb_s.__setitem__((pi, rows, slice(None)), v), cur, length, m[BLK:])

    def pv_stage(pi, d):
        for j in range(n_blk):
            _, length, cur, prev_slot, prev = chunks(d, zero + j)
            m = jnp.concatenate([gather(lambda rows: ma_s[0, rows, :], cur, length),
                                 gather(lambda rows: mb_s[0, rows, :], cur, length)], axis=0)
            p = jnp.exp2(s_buf[pi, j] - jnp.concatenate([m, m], axis=1)).astype(BF16)
            vv = jnp.concatenate([gather(lambda rows: vp[prev_slot, rows, :], prev, length),
                                  gather(lambda rows: vp[slot, rows, :], cur, length)], axis=0).astype(BF16)
            ol = jnp.dot(p, jnp.concatenate([vv, ones], axis=1), preferred_element_type=F32)
            for ref, val in ((acc_s, jnp.where(is_a, ol[:BLK, :LANES], ol[BLK:, :LANES])),
                             (l_s, jnp.where(is_a, ol[:BLK, LANES:], ol[BLK:, LANES:]))):
                scatter(lambda rows, v, ref=ref: ref.__setitem__((pi, rows, slice(None)), v), cur, length, val)

    def shared_max():
        for m_ref in (ma_s, mb_s):
            m_ref[0] = jnp.maximum(jnp.maximum(m_ref[0], m_ref[1]), m_ref[2])

    def all_scores():
        regroup(q_ref, store_q, qscale)
        regroup(k_ref, store_k)
        for pi, (_, d) in reversed(list(enumerate(PATTERNS))):
            qk_stage(pi, d)
        regroup(v_ref, store_v)
        shared_max()

    def all_pv():
        for pi, (_, d) in enumerate(PATTERNS):
            pv_stage(pi, d)
        for lo in range(4):
            for hi in range(4):
                rows = pl.ds((4 * hi + lo) * BLK, BLK)
                acc = acc_s[0, rows, :] + acc_s[1, rows, :] + acc_s[2, rows, :]
                l = l_s[0, rows, :] + l_s[1, rows, :] + l_s[2, rows, :]
                mid[lo, pl.ds(hi, BLK, stride=4), :] = acc / l
        for lo in range(4):
            qp[pl.ds(lo, quarter, stride=4), :] = mid[lo]
        o_ref[...] = (qp[...] * g_ref[...].astype(F32)).astype(o_ref.dtype)

    _region(all_scores)
    _region(all_pv)


def _attn_prompt(q, k, v, g, bias_tbl, batch, seq):
    n_span = seq // SPAN
    n_hp = N_HEADS // 2
    cur = lambda hp, b, s: (b * n_span + s, hp)
    blk = (SPAN, LANES)
    return pl.pallas_call(
        _attn_kernel,
        grid=(n_hp, batch, n_span),
        in_specs=[
            pl.BlockSpec(blk, cur),
            pl.BlockSpec(blk, cur),
            pl.BlockSpec(blk, cur),
            pl.BlockSpec(blk, cur),
            pl.BlockSpec((None, 2, len(PATTERNS), 2 * BLK, 2 * BLK), lambda hp, b, s: (hp, 0, 0, 0, 0)),
        ],
        out_specs=pl.BlockSpec(blk, cur),
        out_shape=jax.ShapeDtypeStruct((batch * seq, D_ATTN), BF16),
        scratch_shapes=[
            pltpu.VMEM(blk, F32),
            pltpu.VMEM((2,) + blk, F32),
            pltpu.VMEM((2,) + blk, F32),
            pltpu.VMEM((4, SPAN // 4, LANES), F32),
            pltpu.VMEM((len(PATTERNS),) + blk, F32),
            pltpu.VMEM((len(PATTERNS),) + blk, F32),
            pltpu.VMEM((len(PATTERNS),) + blk, F32),
            pltpu.VMEM((len(PATTERNS),) + blk, F32),
            pltpu.VMEM((len(PATTERNS), SPAN // BLK, 2 * BLK, 2 * BLK), F32),
        ],
        compiler_params=pltpu.CompilerParams(
            dimension_semantics=("arbitrary", "arbitrary", "arbitrary"), vmem_limit_bytes=VMEM_LIMIT),
    )(q, k, v, g, bias_tbl)


def _bias_lookup(rel_bias, dist):
    bucket = _t5_bucket(dist)[..., None]
    edges = jnp.arange(N_BUCKETS)
    onehot = ((bucket >= edges) & (bucket < edges + 1)).astype(F32)
    return jnp.einsum('...b,bh->h...', onehot, rel_bias.astype(F32), precision=lax.Precision.HIGHEST)


def _prompt_bias_table(rel_bias):
    tbls = []
    for w, d in PATTERNS:
        n_chunk = 16 // d
        n = jnp.arange(BLK)
        idx = (n % (BLK // n_chunk)) * n_chunk + n // (BLK // n_chunk)
        i = idx[:, None]
        j = jnp.concatenate([idx, BLK + idx])[None, :]
        rel = i + BLK - j
        band = (rel >= 0) & (rel <= w // d)
        bias = _bias_lookup(rel_bias, jnp.maximum(rel, 0) * d) * LOG2E
        normal = jnp.where(band[None], bias, NEG)
        first = jnp.where((band & (j >= BLK))[None], bias, NEG)
        tbls.append(jnp.stack([normal, first], axis=0))
    t = jnp.stack(tbls, axis=1)
    t = t.reshape(2, len(PATTERNS), N_HEADS // 2, 2 * BLK, 2 * BLK)
    return jnp.moveaxis(t, 2, 0)


def _ssd_kernel(xbc_ref, z_ref, dt_ref, cw_ref, cb_ref, dtb_ref, alog_ref, dsk_ref, nw_ref,
                e_ref, tril_ref, s_ref, h_ref, cbuf):
    @pl.when(pl.program_id(1) == 0)
    def _():
        cbuf[:, 0:8, :] = jnp.zeros((CONV_DIM // LANES, 8, LANES), F32)
        h_ref[...] = jnp.zeros_like(h_ref)

    def body(ci, carry):
        for k in range(SSD_CHUNKS_UNROLL):
            rows = pl.ds(pl.multiple_of((ci * SSD_CHUNKS_UNROLL + k) * CHUNK, CHUNK), CHUNK)
            _ssd_chunk(xbc_ref.at[rows], z_ref.at[rows], dt_ref.at[rows], cw_ref, cb_ref, dtb_ref, alog_ref,
                       dsk_ref, nw_ref, e_ref, tril_ref, s_ref.at[rows], h_ref, cbuf)
        return carry
    lax.fori_loop(0, SSD_CHUNKS_PER_STEP // SSD_CHUNKS_UNROLL, body, 0)


def _ssd_chunk(xbc_ref, z_ref, dt_ref, cw_ref, cb_ref, dtb_ref, alog_ref, dsk_ref, nw_ref,
               e_ref, tril_ref, s_ref, h_ref, cbuf):
    n_slab = CONV_DIM // LANES

    xc_slabs = []
    for j in range(n_slab):
        cols = slice(j * LANES, (j + 1) * LANES)
        xj = xbc_ref[:, cols]
        cbuf[j, 8:8 + CHUNK, :] = xj
        acc = cb_ref[:, cols] + xj * cw_ref[CONV_W - 1:CONV_W, cols]
        for i in range(CONV_W - 1):
            tap = cbuf[pl.ds(j, 1, stride=2), pl.ds(8 - (CONV_W - 1) + i, CHUNK), :][0]
            acc = acc + tap * cw_ref[i:i + 1, cols]
        xc_slabs.append(_silu(acc))
        cbuf[j, 0:8, :] = xj[CHUNK - 8:, :]

    n_x = D_SSD // LANES
    lane =lax.broadcasted_iota(jnp.int32, (CHUNK, LANES), 1)
    sub = lax.broadcasted_iota(jnp.int32, (CHUNK, LANES), 0)
    head_lane = lane < N_HEADS
    dt = jnp.where(head_lane, _softplus(dt_ref[...] + dtb_ref[...]), 0.0)
    la = dt * (-jnp.exp(alog_ref[...]))

    def hi_lo(val):
        hi = val.astype(BF16).astype(F32)
        return (hi + pltpu.roll(val - hi, N_HEADS, axis=1)).astype(BF16)

    cs2 = jnp.dot(tril_ref[...], hi_lo(la), preferred_element_type=F32)
    a_cs = jnp.where(head_lane, cs2 + pltpu.roll(cs2, LANES - N_HEADS, axis=1), 0.0)
    ea = jnp.where(head_lane, jnp.exp(a_cs), 0.0)
    dte = jnp.where(head_lane, jnp.exp(a_cs[CHUNK - 1:CHUNK, :] - a_cs), 0.0)

    expanded = jnp.dot(jnp.concatenate([hi_lo(dt), hi_lo(ea), hi_lo(dte)], axis=0), e_ref[...],
                       preferred_element_type=F32)
    dtx, eax, dtex = expanded[:CHUNK], expanded[CHUNK:2 * CHUNK], expanded[2 * CHUNK:]
    slab = lambda val, j: val[:, j * LANES:(j + 1) * LANES]
    xdt_slabs = [xc_slabs[j] * slab(dtx, j) for j in range(n_x)]

    a_cs_t = a_cs.T
    causal = sub >= lane
    is_a = lane < HEAD_DIM
    ys = []
    for g in range(N_GROUPS):
        b_g = xc_slabs[n_x + g]
        c_g = xc_slabs[n_x + N_GROUPS + g]
        b_bf = b_g.astype(BF16)
        c_bf = c_g.astype(BF16)
        cb = lax.dot_general(c_bf, b_bf, (((1,), (1,)), ((), ())), preferred_element_type=F32)
        gcols = slice(g * 512, (g + 1) * 512)
        h_prev = h_ref[:, gcols]
        y_off = jnp.dot(c_bf, h_prev.astype(BF16), preferred_element_type=F32) * eax[:, gcols]
        xdte_g = jnp.concatenate([(xdt_slabs[g * 4 + hp] * slab(dtex, g * 4 + hp)).astype(BF16) for hp in range(4)],
                                 axis=1)
        st = jnp.dot(b_g.T.astype(BF16), xdte_g, preferred_element_type=F32)
        h_ref[:, gcols] = h_prev * eax[CHUNK - 1:CHUNK, gcols] + st
        for hp in range(4):
            pair = []
            for which in range(2):
                h = g * 8 + hp * 2 + which
                col = jnp.sum(jnp.where(lane == h, a_cs, 0.0), axis=1, keepdims=True)
                seg = col - a_cs_t[h:h + 1, :]
                lmat = jnp.exp(jnp.where(causal, seg, NEG))
                pair.append((cb * lmat).astype(BF16))
            x_pair = xdt_slabs[g * 4 + hp].astype(BF16)
            y_a = jnp.dot(pair[0], x_pair, preferred_element_type=F32)
            y_b = jnp.dot(pair[1], x_pair, preferred_element_type=F32)
            ys.append(jnp.where(is_a, y_a, y_b) + y_off[:, hp * LANES:(hp + 1) * LANES]
                      + slab(dsk_ref, g * 4 + hp) * xc_slabs[g * 4 + hp])
    y = jnp.concatenate(ys, axis=1)
    yz = y * z_ref[...].astype(F32)
    var = jnp.mean(yz * yz, axis=-1, keepdims=True)
    s_ref[...] = (yz * lax.rsqrt(var + EPS) * nw_ref[...]).astype(s_ref.dtype)


def _ssd_prompt(xbc, z, dt, cw, cb, dtb, alog, dsk, nw, emat, tril, batch, seq):
    tc = CHUNK * SSD_CHUNKS_PER_STEP
    assert seq % tc == 0
    nc = seq // tc
    row = lambda b, c: (b * nc + c, 0)
    const = lambda b, c: (0, 0)
    return pl.pallas_call(
        _ssd_kernel,
        grid=(batch, nc),
        in_specs=[
            pl.BlockSpec((tc, CONV_DIM), row),
            pl.BlockSpec((tc, D_SSD), row),
            pl.BlockSpec((tc, LANES), row),
            pl.BlockSpec((CONV_W, CONV_DIM), const),
            pl.BlockSpec((1, CONV_DIM), const),
            pl.BlockSpec((1, LANES), const),
            pl.BlockSpec((1, LANES), const),
            pl.BlockSpec((1, D_SSD), const),
            pl.BlockSpec((1, D_SSD), const),
            pl.BlockSpec((LANES, D_SSD), const),
            pl.BlockSpec((CHUNK, CHUNK), const),
        ],
        out_specs=[
            pl.BlockSpec((tc, D_SSD), row),
            pl.BlockSpec((None, D_STATE, D_SSD), lambda b, c: (b, 0, 0)),
        ],
        out_shape=(jax.ShapeDtypeStruct((batch * seq, D_SSD), BF16),
                   jax.ShapeDtypeStruct((batch, D_STATE, D_SSD), F32)),
        scratch_shapes=[pltpu.VMEM((CONV_DIM // LANES, CHUNK + 8, LANES), F32)],
        compiler_params=pltpu.CompilerParams(
            dimension_semantics=("arbitrary", "arbitrary"), vmem_limit_bytes=VMEM_LIMIT),
    )(xbc, z, dt, cw, cb, dtb, alog, dsk, nw, emat, tril)


def _attn_sample_heads(hh, n_heads, heads, qt_ref, knt_ref, vnt_ref, gt_ref, k_ref, v_ref, btbl_ref, bias0_ref,
                       o_ref):
    lane = lax.broadcasted_iota(jnp.int32, (HEAD_DIM, LANES), 1)
    lane1 = lax.broadcasted_iota(jnp.int32, (1, LANES), 1)
    qt = qt_ref[...] * (HEAD_DIM ** -0.5)
    n_pat = float(len(PATTERNS))
    for j in heads:
        h = hh * n_heads + j
        pick = lane == h

        def col(val, pick=pick):
            return jnp.sum(jnp.where(pick, val, 0.0), axis=1, keepdims=True)

        qc, knc, vnc, gc = col(qt), col(knt_ref[...]), col(vnt_ref[...]), col(gt_ref[...])
        b0 = jnp.sum(jnp.where(lane1 == h, bias0_ref[...], 0.0), axis=1, keepdims=True)
        s0 = jnp.sum(qc * knc, axis=0, keepdims=True) + b0
        s = jnp.sum(k_ref[j] * qc, axis=0, keepdims=True)
        sp = [s + btbl_ref[pi, pl.ds(h, 1), :] for pi in range(len(PATTERNS))]
        m = s0
        for x in sp:
            m = jnp.maximum(m, jnp.max(x, axis=1, keepdims=True))
        p0 = n_pat * jnp.exp(s0 - m)
        pw = jnp.exp(sp[0] - m)
        for x in sp[1:]:
            pw = pw + jnp.exp(x - m)
        l = jnp.sum(pw, axis=1, keepdims=True) + p0
        oc = (jnp.sum(v_ref[j] * pw, axis=1, keepdims=True) + p0 * vnc) / l
        o_ref[...] = jnp.where(pick, oc * gc, o_ref[...])


def _sample_bias_tables(rel_bias, n_past):
    dist = n_past - jnp.arange(n_past)
    bias = _bias_lookup(rel_bias, dist)
    tbls = [jnp.where(((dist % d == 0) & (dist <= w))[None], bias, NEG) for w, d in PATTERNS]
    bias0 = _bias_lookup(rel_bias, jnp.zeros((1,), jnp.int32))
    return jnp.stack(tbls, axis=0), jnp.pad(bias0.reshape(1, N_HEADS), ((0, 0), (0, LANES - N_HEADS)))


def _ssd_sample_kernel(xbc_ref, z_ref, dt_ref, sc_ref, h_ref, cw_ref, cb_ref, dtb_ref, alog_ref, dsk_ref,
                       nw_ref, e_ref, s_ref, conv_out_ref, h_out_ref):
    xnew = xbc_ref[...]
    sc = sc_ref[...]
    acc = cb_ref[...] + xnew * cw_ref[CONV_W - 1:CONV_W, :]
    for i in range(CONV_W - 1):
        acc = acc + sc[i:i + 1, :] * cw_ref[i:i + 1, :]
    xc = _silu(acc)
    conv_out_ref[0:CONV_W - 2, :] = sc[1:CONV_W - 1, :]
    conv_out_ref[CONV_W - 2:CONV_W - 1, :] = xnew

    xs = xc[:, :D_SSD]
    lane1 = lax.broadcasted_iota(jnp.int32, (1, LANES), 1)
    dt = jnp.where(lane1 < N_HEADS, _softplus(dt_ref[...] + dtb_ref[...]), 0.0)
    da = jnp.where(lane1 < N_HEADS, jnp.exp(dt * (-jnp.exp(alog_ref[...]))), 0.0)

    def expand(val):
        v8 = jnp.broadcast_to(val, (8, LANES))
        out = jnp.zeros((8, D_SSD), F32)
        for _ in range(3):
            part = v8.astype(BF16)
            out = out + jnp.dot(part, e_ref[...], preferred_element_type=F32)
            v8 = v8 - part.astype(F32)
        return out[0:1, :]

    xdt = xs * expand(dt)
    dax = expand(da)

    lane = lax.broadcasted_iota(jnp.int32, (HEAD_DIM, LANES), 1)
    sub = lax.broadcasted_iota(jnp.int32, (HEAD_DIM, LANES), 0)
    eye2 = (lane % HEAD_DIM) == sub
    is_a = lane < HEAD_DIM

    def to_cols(row):
        mat = jnp.where(eye2, jnp.broadcast_to(row, (HEAD_DIM, LANES)), 0.0)
        col_a = jnp.sum(jnp.where(is_a, mat, 0.0), axis=1, keepdims=True)
        col_b = jnp.sum(jnp.where(is_a, 0.0, mat), axis=1, keepdims=True)
        return col_a, col_b

    y_rows = []
    for hp in range(N_HEADS // 2):
        g = hp // 4
        b_row = xc[:, D_SSD + g * D_STATE:D_SSD + (g + 1) * D_STATE]
        c_row = xc[:, D_SSD + (N_GROUPS + g) * D_STATE:D_SSD + (N_GROUPS + g + 1) * D_STATE]
        cols = slice(hp * LANES, (hp + 1) * LANES)
        x_cols = to_cols(xdt[:, cols])
        d_cols = to_cols(dax[:, cols])
        y_cols = []
        for which in range(2):
            h = hp * 2 + which
            h_new = h_ref[h] * d_cols[which] + x_cols[which] * b_row
            h_out_ref[h] = h_new
            y_cols.append(jnp.sum(h_new * c_row, axis=1, keepdims=True))
        y_mat = jnp.where(eye2, jnp.where(is_a, y_cols[0], y_cols[1]), 0.0)
        y_rows.append(jnp.sum(y_mat, axis=0, keepdims=True))
    y = jnp.concatenate(y_rows, axis=1) + dsk_ref[...] * xs
    yz = y * z_ref[...].astype(F32)
    var = jnp.mean(yz * yz, axis=-1, keepdims=True)
    s_ref[...] = (yz * lax.rsqrt(var + EPS) * nw_ref[...]).astype(s_ref.dtype)


def _ssd_sample_specs(b):
    const = lambda i: (0, 0)
    tok = lambda width: pl.BlockSpec((None, 1, width), lambda i: (i, 0, 0))
    conv_spec = pl.BlockSpec((None, CONV_W - 1, CONV_DIM), lambda i: (i, 0, 0))
    ssm_spec = pl.BlockSpec((None, N_HEADS, HEAD_DIM, D_STATE), lambda i: (i, 0, 0, 0))
    in_specs = [
        tok(CONV_DIM), tok(D_SSD), tok(LANES), conv_spec, ssm_spec,
        pl.BlockSpec((CONV_W, CONV_DIM), const),
        pl.BlockSpec((1, CONV_DIM), const),
        pl.BlockSpec((1, LANES), const),
        pl.BlockSpec((1, LANES), const),
        pl.BlockSpec((1, D_SSD), const),
        pl.BlockSpec((1, D_SSD), const),
        pl.BlockSpec((LANES, D_SSD), const),
    ]
    assert len(in_specs) == N_SSD_SAMPLE_IN
    out_shapes = (jax.ShapeDtypeStruct((b, 1, D_SSD), BF16),
                  jax.ShapeDtypeStruct((b, CONV_W - 1, CONV_DIM), F32),
                  jax.ShapeDtypeStruct((b, N_HEADS, HEAD_DIM, D_STATE), F32))
    return in_specs, [tok(D_SSD), conv_spec, ssm_spec], out_shapes


def kernel(x_prompt, x_sample, cache_win_k, cache_win_v, state_conv, state_ssm, norm_w, w_in, q_norm_w,
           k_norm_w, rel_bias, conv_w, conv_b, dt_bias, a_log, d_skip, ssd_norm_w, w_out):
    assert x_prompt.shape[-1] == D_MODEL and w_in.shape[0] == 1, "single-layer model of width 1024 only"
    batch, seq, _ = x_prompt.shape
    dec_batch, dec_seq, _ = x_sample.shape
    assert dec_seq == 1 and seq % SPAN == 0 and cache_win_k.shape[2] == WINDOW_MAX

    w_pad = jnp.pad(w_in[0].astype(BF16), ((0, 0), (0, D_IN_PAD - D_IN_PROJ)))
    w_out_b = w_out[0].astype(BF16)
    nw = norm_w[0].reshape(1, D_MODEL)
    qnw = jnp.tile(q_norm_w[0], 512 // HEAD_DIM).reshape(1, 512)
    knw = jnp.tile(k_norm_w[0], 512 // HEAD_DIM).reshape(1, 512)
    cw, cb = conv_w[0], conv_b[0].reshape(1, CONV_DIM)
    pad_heads = lambda a: jnp.pad(a.reshape(1, N_HEADS), ((0, 0), (0, LANES - N_HEADS)))
    dtb, alog = pad_heads(dt_bias[0]), pad_heads(a_log[0])
    dsk = jnp.repeat(d_skip[0], HEAD_DIM).reshape(1, D_SSD)
    snw = ssd_norm_w[0].reshape(1, D_SSD)
    erow = jnp.arange(LANES)[:, None]
    emat = ((erow % N_HEADS == (jnp.arange(D_SSD) // HEAD_DIM)[None, :]) & (erow < 2 * N_HEADS)).astype(BF16)
    tril = (jnp.arange(CHUNK)[:, None] >= jnp.arange(CHUNK)[None, :]).astype(BF16)

    xs2 = x_sample.reshape(dec_batch, D_MODEL)
    qs, ks, vs, gs, zs, xbcs, dts = _inproj(xs2, nw, w_pad, qnw, knw, tm=dec_batch)
    tok_t = lambda t: jnp.pad(jnp.swapaxes(t.astype(F32).reshape(dec_batch, N_HEADS, HEAD_DIM), 1, 2),
                              ((0, 0), (0, 0), (0, LANES - N_HEADS)))
    cache_t = lambda c: jnp.transpose(c[0], (0, 2, 3, 1))
    btbl, bias0 = _sample_bias_tables(rel_bias, cache_win_k.shape[2])
    sample_attn_args = (tok_t(qs), tok_t(ks), tok_t(vs), tok_t(gs), cache_t(cache_win_k), cache_t(cache_win_v),
                        btbl, bias0)

    xp = x_prompt.reshape(batch * seq, D_MODEL)
    nwin = min(WINDOW_MAX, seq)
    q, k, v, g, z, xbc, dt, k_win, v_win, a_t = _inproj(xp, nw, w_pad, qnw, knw, tm=256, window=(seq, nwin),
                                                        sample=sample_attn_args)
    a = _attn_prompt(q, k, v, g, _prompt_bias_table(rel_bias), batch, seq)
    s, h_fin = _ssd_prompt(xbc, z, dt, cw, cb, dtb, alog, dsk, snw, emat, tril, batch, seq)
    ssd_sample_args = (xbcs.reshape(dec_batch, 1, CONV_DIM), zs.reshape(dec_batch, 1, D_SSD),
                       dts.reshape(dec_batch, 1, LANES), state_conv[0], state_ssm[0],
                       cw, cb, dtb, alog, dsk, snw, emat)
    assert (batch * seq) % dec_batch == 0
    y_p, s_s, conv_s, h_s = _outproj(xp, a, s, w_out_b, tm=batch * seq // dec_batch, ssd_sample=ssd_sample_args)
    y_p = y_p.reshape(batch, seq, D_MODEL)
    heads = lambda t: jnp.transpose(t.reshape(1, batch, N_HEADS, HEAD_DIM, nwin), (0, 1, 4, 2, 3))
    kp, vp = heads(k_win), heads(v_win)
    cp = xbc.reshape(batch, seq, CONV_DIM)[None, :, seq - (CONV_W - 1):]
    hp = jnp.swapaxes(h_fin, 1, 2).reshape(batch, N_HEADS, HEAD_DIM, D_STATE)[None]

    a_s = jnp.swapaxes(a_t[:, :, :N_HEADS], 1, 2)
    (y_s,) = _outproj(xs2, a_s.reshape(dec_batch, D_ATTN).astype(BF16), s_s.reshape(dec_batch, D_SSD),
                      w_out_b, tm=dec_batch)
    y_s = y_s.reshape(dec_batch, 1, D_MODEL)
    k_s = ks.reshape(1, dec_batch, 1, N_HEADS, HEAD_DIM)
    v_s = vs.reshape(1, dec_batch, 1, N_HEADS, HEAD_DIM)
    return (y_p, y_s, kp, vp, cp, hp, k_s, v_s, conv_s[None], h_s[None])
```
